```python
import jax, jax.numpy as jnp
from jax import lax
import numpy as np

D_MODEL = 1024
BATCH = 4
SEQ = 4096
DEPTH = 2

GRID_W = 64
CTX_LEN = 256
N_Q_HEADS = 8
N_KV_HEADS = 2
HEAD_DIM = 64
ATTN_WIDTH = N_Q_HEADS * HEAD_DIM
KV_WIDTH = N_KV_HEADS * HEAD_DIM
Q_BLOCK = 128
ROPE_THETA = 10000.0
POOL_WINDOWS = (2, 4, 8, 16)
N_POOL_GROUPS = 4
POOL_GROUP_DIM = 128
POOL_WIDTH = N_POOL_GROUPS * POOL_GROUP_DIM
MIX_IN_WIDTH = ATTN_WIDTH + 2 * KV_WIDTH + POOL_WIDTH
MIX_OUT_WIDTH = ATTN_WIDTH + POOL_WIDTH
CONV_WIDTH = D_MODEL
CONV_K = 3
N_EXPERTS = 16
EXPERT_FF = 1024
CAPACITY_FACTOR = 2
NORM_EPS = 1e-6
DEEPNORM_ALPHA = (2 * DEPTH) ** 0.25
DEEPNORM_BETA = (8 * DEPTH) ** -0.25

kernel_name = 'hybrid_flow_trunk_gqa_pool_shortconv_ecmoe'


def _layer_norm(x, g, b):
    xf = x.astype(jnp.float32)
    mu = jnp.mean(xf, axis=-1, keepdims=True)
    xc = xf - mu
    var = jnp.mean(xc * xc, axis=-1, keepdims=True)
    y = xc * lax.rsqrt(var + NORM_EPS) * g.astype(jnp.float32) + b.astype(jnp.float32)
    return y.astype(x.dtype)


def _rms_norm(x, g):
    xf = x.astype(jnp.float32)
    y = xf * lax.rsqrt(jnp.mean(xf * xf, axis=-1, keepdims=True) + NORM_EPS)
    return (y * g.astype(jnp.float32)).astype(x.dtype)


def _modulation(cond, w, b):
    m = (cond @ w + b)[:, None, :]
    return jnp.split(m, 6, axis=-1)


def _modulate(h, shift, scale):
    return h * (1.0 + scale) + shift


def _post_norm(h, branch, g, b):
    return _layer_norm(DEEPNORM_ALPHA * h + branch, g, b)


def _axial_rope_tables(rows):
    row = jnp.broadcast_to(jnp.arange(rows, dtype=jnp.float32)[:, None], (rows, GRID_W)).reshape(-1)
    col = jnp.broadcast_to(jnp.arange(GRID_W, dtype=jnp.float32)[None, :], (rows, GRID_W)).reshape(-1)
    axis_dim = HEAD_DIM // 2
    inv_freq = ROPE_THETA ** (-jnp.arange(0, axis_dim, 2, dtype=jnp.float32) / axis_dim)
    ang = jnp.concatenate([row[:, None] * inv_freq, col[:, None] * inv_freq], axis=-1)
    return jnp.cos(ang), jnp.sin(ang)


def _apply_rope(x, cos, sin):
    xf = x.astype(jnp.float32).reshape(x.shape[:-1] + (HEAD_DIM // 2, 2))
    x0, x1 = xf[..., 0], xf[..., 1]
    c = cos[None, :, None, :]
    s = sin[None, :, None, :]
    out = jnp.stack([x0 * c - x1 * s, x0 * s + x1 * c], axis=-1).reshape(x.shape)
    return out.astype(x.dtype)


def _blocked_attention(q, k, v):
    b, lq = q.shape[:2]
    grp = N_Q_HEADS // N_KV_HEADS
    nb = lq // Q_BLOCK
    qb = q.reshape(b, nb, Q_BLOCK, N_KV_HEADS, grp, HEAD_DIM).transpose(1, 0, 2, 3, 4, 5)
    scale = HEAD_DIM ** -0.5

    def one_block(qi):
        s = jnp.einsum('bqhgd,bkhd->bhgqk', qi, k, preferred_element_type=jnp.float32) * scale
        p = jax.nn.softmax(s, axis=-1)
        return jnp.einsum('bhgqk,bkhd->bqhgd', p.astype(v.dtype), v)

    o = lax.map(one_block, qb)
    return o.transpose(1, 0, 2, 3, 4, 5).reshape(b, lq, ATTN_WIDTH)


def _pool_branch(p, w_grp, p_scale):
    b, l, _ = p.shape
    pf = p.astype(jnp.float32)
    cs = jnp.concatenate([jnp.zeros((b, 1, POOL_WIDTH), jnp.float32), jnp.cumsum(pf, axis=1)], axis=1)
    t = jnp.arange(l)
    means = []
    for g, w in enumerate(POOL_WINDOWS):
        lo = jnp.maximum(t - w // 2, 0)
        hi = jnp.minimum(t + w // 2, l)
        seg = cs[:, :, g * POOL_GROUP_DIM:(g + 1) * POOL_GROUP_DIM]
        cnt = (hi - lo).astype(jnp.float32)[None, :, None]
        means.append((seg[:, hi] - seg[:, lo]) / cnt)
    pooled = (jnp.concatenate(means, axis=-1) - pf).astype(p.dtype)
    pooled = pooled.reshape(b, l, N_POOL_GROUPS, POOL_GROUP_DIM)
    out = jnp.einsum('blgc,gcd->blgd', pooled, w_grp).reshape(b, l, POOL_WIDTH)
    return out * p_scale


def _split_mix_in(h):
    return jnp.split(h, [ATTN_WIDTH, ATTN_WIDTH + KV_WIDTH, ATTN_WIDTH + 2 * KV_WIDTH], axis=-1)


def _attn_pool_mixer(u_lat, u_ctx, cos, sin, w_in, q_g, k_g, w_grp, p_scale, w_out, ctx_out):
    b, s, _ = u_lat.shape
    lc = u_ctx.shape[1]
    q, k, v, p = _split_mix_in(u_lat @ w_in)
    q = _apply_rope(_rms_norm(q.reshape(b, s, N_Q_HEADS, HEAD_DIM), q_g), cos, sin)
    k = _apply_rope(_rms_norm(k.reshape(b, s, N_KV_HEADS, HEAD_DIM), k_g), cos, sin)
    v = v.reshape(b, s, N_KV_HEADS, HEAD_DIM)
    if ctx_out:
        qc, kc, vc, pc = _split_mix_in(u_ctx @ w_in)
    else:
        kc, vc = jnp.split(u_ctx @ w_in[:, ATTN_WIDTH:ATTN_WIDTH + 2 * KV_WIDTH], 2, axis=-1)
    kc = _rms_norm(kc.reshape(b, lc, N_KV_HEADS, HEAD_DIM), k_g)
    vc = vc.reshape(b, lc, N_KV_HEADS, HEAD_DIM)
    k_all = jnp.concatenate([kc, k], axis=1)
    v_all = jnp.concatenate([vc, v], axis=1)
    a_lat = _blocked_attention(q, k_all, v_all)
    y_lat = jnp.concatenate([a_lat, _pool_branch(p, w_grp, p_scale)], axis=-1) @ w_out
    y_ctx = None
    if ctx_out:
        qc = _rms_norm(qc.reshape(b, lc, N_Q_HEADS, HEAD_DIM), q_g)
        a_ctx = _blocked_attention(qc, kc, vc)
        y_ctx = jnp.concatenate([a_ctx, _pool_branch(pc, w_grp, p_scale)], axis=-1) @ w_out
    return y_lat, y_ctx


def _short_conv_mixer(u, w_in, conv_w, w_out):
    b_gate, c_gate, x_in = jnp.split(u @ w_in, 3, axis=-1)
    z = lax.conv_general_dilated(c_gate * x_in, conv_w[:, None, :], window_strides=(1,),
                                 padding=((CONV_K // 2, CONV_K // 2),),
                                 dimension_numbers=('NWC', 'WIO', 'NWC'),
                                 feature_group_count=CONV_WIDTH)
    return (b_gate * z) @ w_out


def _expert_choice_ffn(u, w_router, w_g, w_u, w_d):
    b, n, _ = u.shape
    cap = CAPACITY_FACTOR * n // N_EXPERTS
    logits = jnp.einsum('bnd,de->ben', u, w_router, preferred_element_type=jnp.float32)
    aff = jax.nn.softmax(logits, axis=1)
    gate, idx = lax.top_k(aff, cap)
    bidx = jnp.arange(b)[:, None, None]
    xe = u[bidx, idx]
    hg = jnp.einsum('becd,edf->becf', xe, w_g)
    hu = jnp.einsum('becd,edf->becf', xe, w_u)
    ye = jnp.einsum('becf,efd->becd', jax.nn.silu(hg) * hu, w_d) * gate[..., None].astype(u.dtype)
    return jnp.zeros_like(u).at[bidx, idx].add(ye)


def setup_inputs(seed: int = 0) -> dict:
    key = jax.random.key(seed)
    ks = jax.random.split(key, 24)
    f32 = jnp.float32
    n_even = (DEPTH + 1) // 2
    n_odd = DEPTH // 2

    def dense(k, shape, fan_in, mult=1.0):
        return jax.random.normal(k, shape, f32) * (mult * fan_in ** -0.5)

    def gain(k, shape):
        return 1.0 + 0.02 * jax.random.normal(k, shape, f32)

    def small(k, shape):
        return 0.02 * jax.random.normal(k, shape, f32)

    return {
        'x': jax.random.normal(ks[0], (BATCH, SEQ, D_MODEL), f32),
        'c': jax.random.normal(ks[1], (BATCH, D_MODEL), f32),
        'ctx': jax.random.normal(ks[2], (BATCH, CTX_LEN, D_MODEL), f32),
        'c_ctx': jax.random.normal(ks[3], (D_MODEL,), f32),
        'w_mod': dense(ks[4], (DEPTH, D_MODEL, 6 * D_MODEL), D_MODEL, 0.5),
        'b_mod': 0.01 * jax.random.normal(ks[5], (DEPTH, 6 * D_MODEL), f32),
        'ln_mix_g': gain(ks[6], (DEPTH, D_MODEL)),
        'ln_mix_b': small(ks[7], (DEPTH, D_MODEL)),
        'ln_ffn_g': gain(ks[8], (DEPTH, D_MODEL)),
        'ln_ffn_b': small(ks[9], (DEPTH, D_MODEL)),
        'w_mix_in': dense(ks[10], (n_even, D_MODEL, MIX_IN_WIDTH), D_MODEL),
        'q_norm_g': gain(ks[11], (n_even, HEAD_DIM)),
        'k_norm_g': gain(ks[12], (n_even, HEAD_DIM)),
        'w_pool_grp': dense(ks[13], (n_even, N_POOL_GROUPS, POOL_GROUP_DIM, POOL_GROUP_DIM), POOL_GROUP_DIM),
        'pool_scale': gain(ks[14], (n_even, POOL_WIDTH)),
        'w_mix_out': dense(ks[15], (n_even, MIX_OUT_WIDTH, D_MODEL), MIX_OUT_WIDTH, DEEPNORM_BETA),
        'w_conv_in': dense(ks[16], (n_odd, D_MODEL, 3 * CONV_WIDTH), D_MODEL),
        'conv_w': dense(ks[17], (n_odd, CONV_K, CONV_WIDTH), CONV_K),
        'w_conv_out': dense(ks[18], (n_odd, CONV_WIDTH, D_MODEL), CONV_WIDTH, DEEPNORM_BETA),
        'w_router': dense(ks[19], (DEPTH, D_MODEL, N_EXPERTS), D_MODEL),
        'w_exp_gate': dense(ks[20], (DEPTH, N_EXPERTS, D_MODEL, EXPERT_FF), D_MODEL),
        'w_exp_up': dense(ks[21], (DEPTH, N_EXPERTS, D_MODEL, EXPERT_FF), D_MODEL),
        'w_exp_down': dense(ks[22], (DEPTH, N_EXPERTS, EXPERT_FF, D_MODEL), EXPERT_FF, DEEPNORM_BETA),
    }


def reference(x, c, ctx, c_ctx, w_mod, b_mod, ln_mix_g, ln_mix_b, ln_ffn_g, ln_ffn_b,
              w_mix_in, q_norm_g, k_norm_g, w_pool_grp, pool_scale, w_mix_out,
              w_conv_in, conv_w, w_conv_out,
              w_router, w_exp_gate, w_exp_up, w_exp_down):
    rows = x.shape[1] // GRID_W
    cos, sin = _axial_rope_tables(rows)
    cond_lat = jax.nn.silu(c)
    cond_ctx = jax.nn.silu(c_ctx)[None, :]
    h_lat, h_ctx = x, ctx
    for layer in range(DEPTH):
        is_even = layer % 2 == 0
        ctx_out = any(j % 2 == 0 for j in range(layer + 1, DEPTH))
        ctx_in = is_even or ctx_out
        m_lat = _modulation(cond_lat, w_mod[layer], b_mod[layer])
        u_lat = _modulate(h_lat, m_lat[0], m_lat[1])
        u_ctx = None
        if ctx_in:
            m_ctx = _modulation(cond_ctx, w_mod[layer], b_mod[layer])
            u_ctx = _modulate(h_ctx, m_ctx[0], m_ctx[1])
        if is_even:
            e = layer // 2
            y_lat, y_ctx = _attn_pool_mixer(u_lat, u_ctx, cos, sin, w_mix_in[e], q_norm_g[e], k_norm_g[e],
                                            w_pool_grp[e], pool_scale[e], w_mix_out[e], ctx_out)
        else:
            o = layer // 2
            y_lat = _short_conv_mixer(u_lat, w_conv_in[o], conv_w[o], w_conv_out[o])
            y_ctx = _short_conv_mixer(u_ctx, w_conv_in[o], conv_w[o], w_conv_out[o]) if ctx_out else None
        h_lat = _post_norm(h_lat, m_lat[2] * y_lat, ln_mix_g[layer], ln_mix_b[layer])
        f_lat = _expert_choice_ffn(_modulate(h_lat, m_lat[3], m_lat[4]), w_router[layer],
                                   w_exp_gate[layer], w_exp_up[layer], w_exp_down[layer])
        h_lat = _post_norm(h_lat, m_lat[5] * f_lat, ln_ffn_g[layer], ln_ffn_b[layer])
        if ctx_out:
            h_ctx = _post_norm(h_ctx, m_ctx[2] * y_ctx, ln_mix_g[layer], ln_mix_b[layer])
            f_ctx = _expert_choice_ffn(_modulate(h_ctx, m_ctx[3], m_ctx[4]), w_router[layer],
                                       w_exp_gate[layer], w_exp_up[layer], w_exp_down[layer])
            h_ctx = _post_norm(h_ctx, m_ctx[5] * f_ctx, ln_ffn_g[layer], ln_ffn_b[layer])
    return h_lat
```

```python
import functools

import jax
import jax.numpy as jnp
from jax import lax
from jax.experimental import pallas as pl
from jax.experimental.pallas import tpu as pltpu

F32, BF16, I32 = jnp.float32, jnp.bfloat16, jnp.int32
HIGHEST = lax.Precision.HIGHEST

D_MODEL = 1024
DEPTH = 2
GRID_W = 64
N_Q_HEADS = 8
N_KV_HEADS = 2
HEAD_DIM = 64
ATTN_WIDTH = N_Q_HEADS * HEAD_DIM
KV_WIDTH = N_KV_HEADS * HEAD_DIM
ROPE_THETA = 10000.0
POOL_WINDOWS = (2, 4, 8, 16)
POOL_GROUP_DIM = 128
POOL_WIDTH = len(POOL_WINDOWS) * POOL_GROUP_DIM
MIX_IN_WIDTH = ATTN_WIDTH + 2 * KV_WIDTH + POOL_WIDTH
N_EXPERTS = 16
EXPERT_FF = 1024
CAPACITY_FACTOR = 2
NORM_EPS = 1e-6
DEEPNORM_ALPHA = (2 * DEPTH) ** 0.25

LANES = 128
SUBLANES = 8
HALO = SUBLANES
VMEM_LIMIT = 56 * 1024 * 1024

_NT = (((1,), (1,)), ((), ()))


def _cparams(sem):
    return pltpu.CompilerParams(dimension_semantics=sem, vmem_limit_bytes=VMEM_LIMIT)


def _layer_norm(z, g, b):
    mu = jnp.mean(z, axis=-1, keepdims=True)
    zc = z - mu
    var = jnp.mean(zc * zc, axis=-1, keepdims=True)
    return zc * lax.rsqrt(var + NORM_EPS) * g + b


def _mod_kernel(c_ref, w_ref, b_ref, o_ref):
    c = c_ref[...]
    cond = c * jax.nn.sigmoid(c)
    o_ref[...] = jnp.dot(cond, w_ref[...], precision=HIGHEST, preferred_element_type=F32) + b_ref[...]


def _modulation(cond_rows, w_mod, b_mod):
    n_out = w_mod.shape[-1]
    tn = D_MODEL
    return pl.pallas_call(
        _mod_kernel,
        grid=(DEPTH, n_out // tn),
        in_specs=[
            pl.BlockSpec((SUBLANES, D_MODEL), lambda l, j: (0, 0)),
            pl.BlockSpec((None, D_MODEL, tn), lambda l, j: (l, 0, j)),
            pl.BlockSpec((None, 1, tn), lambda l, j: (l, 0, j)),
        ],
        out_specs=pl.BlockSpec((None, SUBLANES, tn), lambda l, j: (l, 0, j)),
        out_shape=jax.ShapeDtypeStruct((DEPTH, SUBLANES, n_out), F32),
        compiler_params=_cparams(("arbitrary", "arbitrary")),
        name="modulation",
    )(cond_rows, w_mod, b_mod.reshape(DEPTH, 1, n_out))


def _group_mean_sq(t, bd):
    t2 = t * t
    hi = t2.astype(BF16)
    lo = (t2 - hi.astype(F32)).astype(BF16)
    return (jnp.dot(hi, bd, preferred_element_type=F32) + jnp.dot(lo, bd, preferred_element_type=F32))


def _rope_chunk(y, c, s_signed):
    n = y.shape[-1]
    lane = lax.broadcasted_iota(I32, y.shape, 1)
    nxt = pltpu.roll(y, n - 1, axis=1)
    prv = pltpu.roll(y, 1, axis=1)
    partner = jnp.where((lane & 1) == 0, nxt, prv)
    return y * c + partner * s_signed


def _l0_in_kernel(x_ref, sh_ref, sc_ref, w_ref, bd_ref, qg_ref, kg_ref, cos_ref, sin_ref,
                  q_ref, k_ref, v_ref, p_ref):
    u = x_ref[...] * (1.0 + sc_ref[...]) + sh_ref[...]
    h = jnp.dot(u.astype(BF16), w_ref[...], preferred_element_type=F32)
    q = h[:, :ATTN_WIDTH]
    k = h[:, ATTN_WIDTH:ATTN_WIDTH + KV_WIDTH]
    bd = bd_ref[...]
    qn = q * lax.rsqrt(_group_mean_sq(q, bd) + NORM_EPS) * qg_ref[...]
    kn = k * lax.rsqrt(_group_mean_sq(k, bd[:KV_WIDTH, :KV_WIDTH]) + NORM_EPS) * kg_ref[...]
    scale = HEAD_DIM ** -0.5
    for c in range(ATTN_WIDTH // LANES):
        sl = slice(c * LANES, (c + 1) * LANES)
        q_ref[:, sl] = (_rope_chunk(qn[:, sl], cos_ref[:, sl], sin_ref[:, sl]) * scale).astype(BF16)
    k_ref[...] = _rope_chunk(kn, cos_ref[:, :KV_WIDTH], sin_ref[:, :KV_WIDTH]).astype(BF16)
    v_ref[...] = h[:, ATTN_WIDTH + KV_WIDTH:ATTN_WIDTH + 2 * KV_WIDTH].astype(BF16)
    p_ref[...] = h[:, ATTN_WIDTH + 2 * KV_WIDTH:]


def _l0_in_proj(x, shift, scale, w_in_bf, bd, qg, kg, cos_t, sin_t, ts):
    b, l, _ = x.shape
    per_batch = shift.shape[0] == b
    mod_map = (lambda bi, si: (bi, 0, 0)) if per_batch else (lambda bi, si: (0, 0, 0))
    full = lambda bi, si: (0, 0)
    return pl.pallas_call(
        _l0_in_kernel,
        grid=(b, l // ts),
        in_specs=[
            pl.BlockSpec((None, ts, D_MODEL), lambda bi, si: (bi, si, 0)),
            pl.BlockSpec((None, 1, D_MODEL), mod_map),
            pl.BlockSpec((None, 1, D_MODEL), mod_map),
            pl.BlockSpec((D_MODEL, MIX_IN_WIDTH), full),
            pl.BlockSpec((ATTN_WIDTH, ATTN_WIDTH), full),
            pl.BlockSpec((1, ATTN_WIDTH), full),
            pl.BlockSpec((1, KV_WIDTH), full),
            pl.BlockSpec((ts, ATTN_WIDTH), lambda bi, si: (si, 0)),
            pl.BlockSpec((ts, ATTN_WIDTH), lambda bi, si: (si, 0)),
        ],
        out_specs=[
            pl.BlockSpec((None, ts, ATTN_WIDTH), lambda bi, si: (bi, si, 0)),
            pl.BlockSpec((None, ts, KV_WIDTH), lambda bi, si: (bi, si, 0)),
            pl.BlockSpec((None, ts, KV_WIDTH), lambda bi, si: (bi, si, 0)),
            pl.BlockSpec((None, ts, POOL_WIDTH), lambda bi, si: (bi, si, 0)),
        ],
        out_shape=[
            jax.ShapeDtypeStruct((b, l, ATTN_WIDTH), BF16),
            jax.ShapeDtypeStruct((b, l, KV_WIDTH), BF16),
            jax.ShapeDtypeStruct((b, l, KV_WIDTH), BF16),
            jax.ShapeDtypeStruct((b, l, POOL_WIDTH), F32),
        ],
        compiler_params=_cparams(("arbitrary", "arbitrary")),
        name="l0_in_proj",
    )(x, shift, scale, w_in_bf, bd, qg, kg, cos_t, sin_t)


def _attn_kernel(q_ref, k_ref, v_ref, o_ref):
    heads_per_kv = N_Q_HEADS // N_KV_HEADS
    for pr in range(ATTN_WIDTH // LANES):
        g = (2 * pr) // heads_per_kv
        sl = slice(pr * LANES, (pr + 1) * LANES)
        qp = q_ref[:, sl]
        acc = jnp.zeros(qp.shape, F32)
        for par in range(2):
            s = lax.dot_general(qp, k_ref[2 * g + par], _NT, preferred_element_type=F32)
            m = jnp.max(s, axis=-1, keepdims=True)
            p = jnp.exp(s - m)
            l = jnp.sum(p, axis=-1, keepdims=True)
            o = jnp.dot(p.astype(BF16), v_ref[2 * g + par], preferred_element_type=F32)
            acc = acc + o / l
        o_ref[:, sl] = acc.astype(BF16)


def _attention(q, k_pad, v_pad, tq):
    b, s, _ = q.shape
    lk = k_pad.shape[2]
    return pl.pallas_call(
        _attn_kernel,
        grid=(b, s // tq),
        in_specs=[
            pl.BlockSpec((None, tq, ATTN_WIDTH), lambda bi, qi: (bi, qi, 0)),
            pl.BlockSpec((None, 4, lk, LANES), lambda bi, qi: (bi, 0, 0, 0)),
            pl.BlockSpec((None, 4, lk, LANES), lambda bi, qi: (bi, 0, 0, 0)),
        ],
        out_specs=pl.BlockSpec((None, tq, ATTN_WIDTH), lambda bi, qi: (bi, qi, 0)),
        out_shape=jax.ShapeDtypeStruct((b, s, ATTN_WIDTH), BF16),
        compiler_params=_cparams(("arbitrary", "arbitrary")),
        name="attention",
    )(q, k_pad, v_pad)


def _lane_half_pad(t):
    h0, h1 = t[..., :HEAD_DIM], t[..., HEAD_DIM:]
    z = jnp.zeros_like(h0)
    cat = lambda a, c: jnp.concatenate([a, c], axis=-1)
    return jnp.stack([cat(h0, z), cat(z, h0), cat(h1, z), cat(z, h1)], axis=1)


def _post_tail(resid, y, gate, lng, lnb, shf, scf, wr_t):
    h = _layer_norm(DEEPNORM_ALPHA * resid + gate * y, lng, lnb)
    u = h * (1.0 + scf) + shf
    logits = lax.dot_general(wr_t, u, _NT, precision=HIGHEST, preferred_element_type=F32)
    ex = jnp.exp(logits - jnp.max(logits, axis=0, keepdims=True))
    aff = ex / jnp.sum(ex, axis=0, keepdims=True)
    return h, u, aff


def _fill_halo(ext_ref, prev_ref, main_ref, next_ref, ts):
    st = pl.program_id(1)
    last = pl.num_programs(1) - 1
    ext_ref[0:HALO, :] = jnp.where(st > 0, prev_ref[...], 0.0)
    ext_ref[HALO:HALO + ts, :] = main_ref[...]
    ext_ref[HALO + ts:HALO + ts + HALO, :] = jnp.where(st < last, next_ref[...], 0.0)


def _halo_specs(ts, width, seq):
    r = ts // HALO
    nblk = seq // HALO
    return [
        pl.BlockSpec((None, ts, width), lambda bi, si: (bi, si, 0)),
        pl.BlockSpec((None, HALO, width), lambda bi, si: (bi, jnp.maximum(si * r - 1, 0), 0)),
        pl.BlockSpec((None, HALO, width), lambda bi, si: (bi, jnp.minimum((si + 1) * r, nblk - 1), 0)),
    ]


def _tail_specs(ts):
    row = lambda bi, si: (bi, 0, 0)
    full = lambda bi, si: (0, 0)
    in_specs = [
        pl.BlockSpec((None, ts, D_MODEL), lambda bi, si: (bi, si, 0)),
        pl.BlockSpec((None, 1, D_MODEL), row),
        pl.BlockSpec((1, D_MODEL), full),
        pl.BlockSpec((1, D_MODEL), full),
        pl.BlockSpec((None, 1, D_MODEL), row),
        pl.BlockSpec((None, 1, D_MODEL), row),
        pl.BlockSpec((N_EXPERTS, D_MODEL), full),
    ]
    out_specs = [
        pl.BlockSpec((None, ts, D_MODEL), lambda bi, si: (bi, si, 0)),
        pl.BlockSpec((None, ts, D_MODEL), lambda bi, si: (bi, si, 0)),
        pl.BlockSpec((None, N_EXPERTS, ts), lambda bi, si: (bi, 0, si)),
    ]
    return in_specs, out_specs


def _tail_out_shape(b, s):
    return [jax.ShapeDtypeStruct((b, s, D_MODEL), F32),
            jax.ShapeDtypeStruct((b, s, D_MODEL), F32),
            jax.ShapeDtypeStruct((b, N_EXPERTS, s), F32)]


def _l0_out_kernel(seq, a_ref, p_ref, pprev_ref, pnext_ref, wpool_ref, pscale_ref, wout_ref,
                   x_ref, gate_ref, lng_ref, lnb_ref, shf_ref, scf_ref, wr_ref,
                   h_ref, u_ref, aff_ref, pext_ref):
    ts = p_ref.shape[0]
    _fill_halo(pext_ref, pprev_ref, p_ref, pnext_ref, ts)
    t = pl.program_id(1) * ts + lax.broadcasted_iota(I32, (ts, 1), 0)
    outs = []
    for g, w in enumerate(POOL_WINDOWS):
        sl = slice(g * POOL_GROUP_DIM, (g + 1) * POOL_GROUP_DIM)
        acc = pext_ref[HALO - w // 2:HALO - w // 2 + ts, sl]
        for j in range(-(w // 2) + 1, w // 2):
            acc = acc + pext_ref[HALO + j:HALO + j + ts, sl]
        cnt = jnp.minimum(t + w // 2, seq) - jnp.maximum(t - w // 2, 0)
        pooled = acc / cnt.astype(F32) - p_ref[:, sl]
        og = jnp.dot(pooled.astype(BF16), wpool_ref[g], preferred_element_type=F32)
        outs.append((og * pscale_ref[:, sl]).astype(BF16))
    pool = jnp.concatenate(outs, axis=-1)
    y = (jnp.dot(a_ref[...], wout_ref[:ATTN_WIDTH, :], preferred_element_type=F32)
         + jnp.dot(pool, wout_ref[ATTN_WIDTH:, :], preferred_element_type=F32))
    h, u, aff = _post_tail(x_ref[...], y, gate_ref[...], lng_ref[...], lnb_ref[...],
                           shf_ref[...], scf_ref[...], wr_ref[...])
    h_ref[...] = h
    u_ref[...] = u
    aff_ref[...] = aff


def _l0_out_proj(a, p, wpool_bf, pscale, wout_bf, x, gate, lng, lnb, shf, scf, wr_t, ts):
    b, s, _ = x.shape
    full2 = lambda bi, si: (0, 0)
    tail_in, tail_out = _tail_specs(ts)
    in_specs = (
        [pl.BlockSpec((None, ts, ATTN_WIDTH), lambda bi, si: (bi, si, 0))]
        + _halo_specs(ts, POOL_WIDTH, s)
        + [pl.BlockSpec((len(POOL_WINDOWS), POOL_GROUP_DIM, POOL_GROUP_DIM), lambda bi, si: (0, 0, 0)),
           pl.BlockSpec((1, POOL_WIDTH), full2),
           pl.BlockSpec((ATTN_WIDTH + POOL_WIDTH, D_MODEL), full2)]
        + tail_in)
    return pl.pallas_call(
        functools.partial(_l0_out_kernel, s),
        grid=(b, s // ts),
        in_specs=in_specs,
        out_specs=tail_out,
        out_shape=_tail_out_shape(b, s),
        scratch_shapes=[pltpu.VMEM((ts + 2 * HALO, POOL_WIDTH), F32)],
        compiler_params=_cparams(("arbitrary", "arbitrary")),
        name="l0_out_proj",
    )(a, p, p, p, wpool_bf, pscale, wout_bf, x, gate, lng, lnb, shf, scf, wr_t)


def _exclusive_cumsum_lanes(mask, tri):
    n = mask.shape[1]
    w = tri.shape[0]
    carry = jnp.zeros((mask.shape[0], 1), F32)
    parts = []
    for c in range(n // w):
        blk = mask[:, c * w:(c + 1) * w]
        inc = jnp.dot(blk.astype(BF16), tri, preferred_element_type=F32)
        parts.append(inc - blk + carry)
        carry = carry + inc[:, w - 1:w]
    return jnp.concatenate(parts, axis=1)


def _select_kernel(cap, aff_ref, dest_ref):
    aff = aff_ref[...]
    e = aff.shape[0]
    as_f32 = lambda word: pltpu.bitcast(word, F32)

    def refine(i, thr):
        cand = thr | jnp.left_shift(jnp.int32(1), 30 - i)
        cnt = jnp.sum(jnp.where(aff >= as_f32(cand), 1.0, 0.0), axis=1, keepdims=True)
        return jnp.where(cnt >= cap, cand, thr)

    thr = lax.fori_loop(0, 31, refine, jnp.zeros((e, 1), I32))
    w = 2 * LANES
    r = lax.broadcasted_iota(I32, (w, w), 0)
    c = lax.broadcasted_iota(I32, (w, w), 1)
    tri = jnp.where(r <= c, 1.0, 0.0).astype(BF16)
    gt = jnp.where(aff >= as_f32(thr + 1), 1.0, 0.0)
    eq = jnp.where(aff >= as_f32(thr), 1.0, 0.0) - gt
    need = cap - jnp.sum(gt, axis=1, keepdims=True)
    sel = gt + eq * jnp.where(_exclusive_cumsum_lanes(eq, tri) < need, 1.0, 0.0)
    pos = _exclusive_cumsum_lanes(sel, tri)
    dest_ref[...] = jnp.where(sel > 0.5, pos, float(cap)).astype(I32)


def _select(aff, cap):
    b, e, n = aff.shape
    return pl.pallas_call(
        functools.partial(_select_kernel, cap),
        grid=(b,),
        in_specs=[pl.BlockSpec((None, e, n), lambda bi: (bi, 0, 0))],
        out_specs=pl.BlockSpec((None, e, n), lambda bi: (bi, 0, 0)),
        out_shape=jax.ShapeDtypeStruct((b, e, n), I32),
        compiler_params=_cparams(("arbitrary",)),
        name="expert_select",
    )(aff)


def _expert_kernel(cap, seq, idx_ref, u_hbm, wg_ref, wu_ref, wd_ref, gate_ref, y_ref,
                   xe_ref, wg_bf, wu_bf, wd_bf, sem):
    e = pl.program_id(0)
    b = pl.program_id(1)

    @pl.when(b == 0)
    def _():
        wg_bf[...] = wg_ref[...].astype(BF16)
        wu_bf[...] = wu_ref[...].astype(BF16)
        wd_bf[...] = wd_ref[...].astype(BF16)

    base = (b * N_EXPERTS + e) * cap
    row0 = b * seq

    def row_copy(i):
        tok = row0 + idx_ref[base + i]
        return pltpu.make_async_copy(u_hbm.at[pl.ds(tok, 1), :], xe_ref.at[pl.ds(i, 1), :], sem)

    def issue(i, carry):
        row_copy(i).start()
        return carry

    lax.fori_loop(0, cap, issue, 0)
    pltpu.make_async_copy(u_hbm.at[pl.ds(0, cap), :], xe_ref, sem).wait()

    x = xe_ref[...].astype(BF16)
    hg = jnp.dot(x, wg_bf[...], preferred_element_type=F32)
    hu = jnp.dot(x, wu_bf[...], preferred_element_type=F32)
    act = (hg * jax.nn.sigmoid(hg) * hu).astype(BF16)
    y_ref[...] = jnp.dot(act, wd_bf[...], preferred_element_type=F32) * gate_ref[...]


def _expert_ffn(idx_flat, u_rows, w_g, w_u, w_d, gate_sel, batch, seq, cap):
    wspec = lambda: pl.BlockSpec((None, D_MODEL, EXPERT_FF), lambda e, b, idx: (e, 0, 0))
    grid_spec = pltpu.PrefetchScalarGridSpec(
        num_scalar_prefetch=1,
        grid=(N_EXPERTS, batch),
        in_specs=[
            pl.BlockSpec(memory_space=pl.ANY),
            wspec(), wspec(),
            pl.BlockSpec((None, EXPERT_FF, D_MODEL), lambda e, b, idx: (e, 0, 0)),
            pl.BlockSpec((None, None, cap, 1), lambda e, b, idx: (b, e, 0, 0)),
        ],
        out_specs=pl.BlockSpec((None, None, cap, D_MODEL), lambda e, b, idx: (b, e, 0, 0)),
        scratch_shapes=[
            pltpu.VMEM((cap, D_MODEL), F32),
            pltpu.VMEM((D_MODEL, EXPERT_FF), BF16),
            pltpu.VMEM((D_MODEL, EXPERT_FF), BF16),
            pltpu.VMEM((EXPERT_FF, D_MODEL), BF16),
            pltpu.SemaphoreType.DMA,
        ],
    )
    return pl.pallas_call(
        functools.partial(_expert_kernel, cap, seq),
        grid_spec=grid_spec,
        out_shape=jax.ShapeDtypeStruct((batch, N_EXPERTS, cap, D_MODEL), F32),
        compiler_params=_cparams(("arbitrary", "arbitrary")),
        name="expert_ffn",
    )(idx_flat, u_rows, w_g, w_u, w_d, gate_sel)


def _combine_kernel(cap, idx_ref, y_ref, f_ref):
    b = pl.program_id(0)
    e = pl.program_id(1)

    @pl.when(e == 0)
    def _():
        f_ref[...] = jnp.zeros_like(f_ref)

    base = (b * N_EXPERTS + e) * cap
    sub = lax.broadcasted_iota(I32, (SUBLANES, D_MODEL), 0)

    def chunk(ci, carry):
        r0 = pl.multiple_of(ci * SUBLANES, SUBLANES)
        rows = y_ref[pl.ds(r0, SUBLANES), :]
        for j in range(SUBLANES):
            tok = idx_ref[base + r0 + j]
            t0 = pl.multiple_of((tok >> 3) << 3, SUBLANES)
            add = jnp.where(sub == (tok & (SUBLANES - 1)),
                            jnp.broadcast_to(rows[j:j + 1, :], (SUBLANES, D_MODEL)), 0.0)
            f_ref[pl.ds(t0, SUBLANES), :] = f_ref[pl.ds(t0, SUBLANES), :] + add
        return carry

    lax.fori_loop(0, cap // SUBLANES, chunk, 0)


def _combine(idx_flat, y, seq, cap):
    batch = y.shape[0]
    grid_spec = pltpu.PrefetchScalarGridSpec(
        num_scalar_prefetch=1,
        grid=(batch, N_EXPERTS),
        in_specs=[pl.BlockSpec((None, None, cap, D_MODEL), lambda b, e, idx: (b, e, 0, 0))],
        out_specs=pl.BlockSpec((None, seq, D_MODEL), lambda b, e, idx: (b, 0, 0)),
    )
    return pl.pallas_call(
        functools.partial(_combine_kernel, cap),
        grid_spec=grid_spec,
        out_shape=jax.ShapeDtypeStruct((batch, seq, D_MODEL), F32),
        compiler_params=_cparams(("arbitrary", "arbitrary")),
        name="expert_combine",
    )(idx_flat, y)


def _moe(u, aff, w_g, w_u, w_d):
    b, n, _ = u.shape
    cap = CAPACITY_FACTOR * n // N_EXPERTS
    dest = _select(aff, cap)
    bi = jnp.arange(b, dtype=I32)[:, None, None]
    ei = jnp.arange(N_EXPERTS, dtype=I32)[None, :, None]
    tok = jnp.broadcast_to(jnp.arange(n, dtype=I32), dest.shape)
    idx = jnp.zeros((b, N_EXPERTS, cap + 1), I32).at[bi, ei, dest].set(tok)[..., :cap]
    gate_sel = jnp.take_along_axis(aff, idx, axis=-1)[..., None]
    idx_flat = idx.reshape(-1)
    y = _expert_ffn(idx_flat, u.reshape(b * n, D_MODEL), w_g, w_u, w_d, gate_sel, b, n, cap)
    return _combine(idx_flat, y, n, cap)


def _l1_in_kernel(h_ref, f_ref, gf_ref, lng_ref, lnb_ref, sh_ref, sc_ref, w_ref, h2_ref, bg_ref, cx_ref):
    h2 = _layer_norm(DEEPNORM_ALPHA * h_ref[...] + gf_ref[...] * f_ref[...], lng_ref[...], lnb_ref[...])
    h2_ref[...] = h2
    u = h2 * (1.0 + sc_ref[...]) + sh_ref[...]
    z = jnp.dot(u.astype(BF16), w_ref[...], preferred_element_type=F32)
    bg_ref[...] = z[:, :D_MODEL]
    cx_ref[...] = z[:, D_MODEL:2 * D_MODEL] * z[:, 2 * D_MODEL:]


def _l1_in_proj(h, f, gate_f, lng, lnb, shift, scale, w_bf, ts):
    b, s, _ = h.shape
    tok = lambda: pl.BlockSpec((None, ts, D_MODEL), lambda bi, si: (bi, si, 0))
    row = lambda: pl.BlockSpec((None, 1, D_MODEL), lambda bi, si: (bi, 0, 0))
    vec = lambda: pl.BlockSpec((1, D_MODEL), lambda bi, si: (0, 0))
    return pl.pallas_call(
        _l1_in_kernel,
        grid=(b, s // ts),
        in_specs=[tok(), tok(), row(), vec(), vec(), row(), row(),
                  pl.BlockSpec((D_MODEL, 3 * D_MODEL), lambda bi, si: (0, 0))],
        out_specs=[tok(), tok(), tok()],
        out_shape=[jax.ShapeDtypeStruct((b, s, D_MODEL), F32)] * 3,
        compiler_params=_cparams(("arbitrary", "arbitrary")),
        name="l1_in_proj",
    )(h, f, gate_f, lng, lnb, shift, scale, w_bf)


def _l1_out_kernel(cx_ref, cprev_ref, cnext_ref, bg_ref, cw_ref, wout_ref,
                   x_ref, gate_ref, lng_ref, lnb_ref, shf_ref, scf_ref, wr_ref,
                   h_ref, u_ref, aff_ref, cext_ref):
    ts = cx_ref.shape[0]
    _fill_halo(cext_ref, cprev_ref, cx_ref, cnext_ref, ts)
    z = (cext_ref[HALO - 1:HALO - 1 + ts, :] * cw_ref[0:1, :]
         + cx_ref[...] * cw_ref[1:2, :]
         + cext_ref[HALO + 1:HALO + 1 + ts, :] * cw_ref[2:3, :])
    y = jnp.dot((bg_ref[...] * z).astype(BF16), wout_ref[...], preferred_element_type=F32)
    h, u, aff = _post_tail(x_ref[...], y, gate_ref[...], lng_ref[...], lnb_ref[...],
                           shf_ref[...], scf_ref[...], wr_ref[...])
    h_ref[...] = h
    u_ref[...] = u
    aff_ref[...] = aff


def _l1_out_proj(cx, bg, conv_w, wout_bf, x, gate, lng, lnb, shf, scf, wr_t, ts):
    b, s, _ = x.shape
    tail_in, tail_out = _tail_specs(ts)
    in_specs = (
        _halo_specs(ts, D_MODEL, s)
        + [pl.BlockSpec((None, ts, D_MODEL), lambda bi, si: (bi, si, 0)),
           pl.BlockSpec(conv_w.shape, lambda bi, si: (0, 0)),
           pl.BlockSpec((D_MODEL, D_MODEL), lambda bi, si: (0, 0))]
        + tail_in)
    return pl.pallas_call(
        _l1_out_kernel,
        grid=(b, s // ts),
        in_specs=in_specs,
        out_specs=tail_out,
        out_shape=_tail_out_shape(b, s),
        scratch_shapes=[pltpu.VMEM((ts + 2 * HALO, D_MODEL), F32)],
        compiler_params=_cparams(("arbitrary", "arbitrary")),
        name="l1_out_proj",
    )(cx, cx, cx, bg, conv_w, wout_bf, x, gate, lng, lnb, shf, scf, wr_t)


def _final_kernel(h_ref, f_ref, gf_ref, lng_ref, lnb_ref, o_ref):
    o_ref[...] = _layer_norm(DEEPNORM_ALPHA * h_ref[...] + gf_ref[...] * f_ref[...], lng_ref[...], lnb_ref[...])


def _final_norm(h, f, gate_f, lng, lnb, ts):
    b, s, _ = h.shape
    tok = lambda: pl.BlockSpec((None, ts, D_MODEL), lambda bi, si: (bi, si, 0))
    return pl.pallas_call(
        _final_kernel,
        grid=(b, s // ts),
        in_specs=[tok(), tok(), pl.BlockSpec((None, 1, D_MODEL), lambda bi, si: (bi, 0, 0)),
                  pl.BlockSpec((1, D_MODEL), lambda bi, si: (0, 0)),
                  pl.BlockSpec((1, D_MODEL), lambda bi, si: (0, 0))],
        out_specs=tok(),
        out_shape=jax.ShapeDtypeStruct((b, s, D_MODEL), F32),
        compiler_params=_cparams(("arbitrary", "arbitrary")),
        name="final_norm",
    )(h, f, gate_f, lng, lnb)


def _rope_lane_tables(seq):
    rows = seq // GRID_W
    row = jnp.repeat(jnp.arange(rows, dtype=F32), GRID_W)
    col = jnp.tile(jnp.arange(GRID_W, dtype=F32), rows)
    axis_dim = HEAD_DIM // 2
    inv_freq = ROPE_THETA ** (-jnp.arange(0, axis_dim, 2, dtype=F32) / axis_dim)
    ang = jnp.concatenate([row[:, None] * inv_freq, col[:, None] * inv_freq], axis=-1)
    cos = jnp.repeat(jnp.cos(ang), 2, axis=-1)
    sin = jnp.repeat(jnp.sin(ang), 2, axis=-1)
    sign = jnp.tile(jnp.array([-1.0, 1.0], F32), HEAD_DIM // 2)
    return jnp.tile(cos, (1, N_Q_HEADS)), jnp.tile(sin * sign, (1, N_Q_HEADS))


def kernel(x, c, ctx, c_ctx, w_mod, b_mod, ln_mix_g, ln_mix_b, ln_ffn_g, ln_ffn_b, w_mix_in, q_norm_g, k_norm_g, w_pool_grp, pool_scale, w_mix_out, w_conv_in, conv_w, w_conv_out, w_router, w_exp_gate, w_exp_up, w_exp_down):
    b, s, d = x.shape
    lc = ctx.shape[1]
    ts = 512

    cond_rows = jnp.zeros((SUBLANES, d), F32).at[:b].set(c).at[b].set(c_ctx)
    mods = _modulation(cond_rows, w_mod, b_mod).reshape(DEPTH, SUBLANES, 6, d)
    m_lat = lambda layer, k: mods[layer, :b, k][:, None, :]
    m_ctx = lambda layer, k: mods[layer, b:b + 1, k][:, None, :]
    vec = lambda t: t.reshape(1, -1)

    w_in_bf = w_mix_in[0].astype(BF16)
    grp = lax.broadcasted_iota(I32, (ATTN_WIDTH, ATTN_WIDTH), 0) // HEAD_DIM
    bd = jnp.where(grp == grp.T, 1.0 / HEAD_DIM, 0.0).astype(BF16)
    qg = jnp.tile(q_norm_g[0], N_Q_HEADS).reshape(1, -1)
    kg = jnp.tile(k_norm_g[0], N_KV_HEADS).reshape(1, -1)
    cos_t, sin_t = _rope_lane_tables(s)
    q, k, v, p = _l0_in_proj(x, m_lat(0, 0), m_lat(0, 1), w_in_bf, bd, qg, kg, cos_t, sin_t, ts)
    ones = jnp.ones((lc, ATTN_WIDTH), F32)
    _, kc, vc, _ = _l0_in_proj(ctx, m_ctx(0, 0), m_ctx(0, 1), w_in_bf, bd, qg, kg, ones, 0.0 * ones, lc)
    k_pad = _lane_half_pad(jnp.concatenate([kc, k], axis=1))
    v_pad = _lane_half_pad(jnp.concatenate([vc, v], axis=1))
    a = _attention(q, k_pad, v_pad, 256)
    h, u, aff = _l0_out_proj(a, p, w_pool_grp[0].astype(BF16), vec(pool_scale[0]), w_mix_out[0].astype(BF16),
                             x, m_lat(0, 2), vec(ln_mix_g[0]), vec(ln_mix_b[0]), m_lat(0, 3), m_lat(0, 4),
                             w_router[0].T, ts)
    f = _moe(u, aff, w_exp_gate[0], w_exp_up[0], w_exp_down[0])

    h, bg, cx = _l1_in_proj(h, f, m_lat(0, 5), vec(ln_ffn_g[0]), vec(ln_ffn_b[0]), m_lat(1, 0), m_lat(1, 1),
                            w_conv_in[0].astype(BF16), ts)
    h, u, aff = _l1_out_proj(cx, bg, conv_w[0], w_conv_out[0].astype(BF16),
                             h, m_lat(1, 2), vec(ln_mix_g[1]), vec(ln_mix_b[1]), m_lat(1, 3), m_lat(1, 4),
                             w_router[1].T, ts)
    f = _moe(u, aff, w_exp_gate[1], w_exp_up[1], w_exp_down[1])
    return _final_norm(h, f, m_lat(1, 5), vec(ln_ffn_g[1]), vec(ln_ffn_b[1]), ts)
```

```python
import functools

import jax
import jax.numpy as jnp
from jax import lax
from jax.experimental import pallas as pl
from jax.experimental.pallas import tpu as pltpu

F32, BF16, I32 = jnp.float32, jnp.bfloat16, jnp.int32
HIGHEST = lax.Precision.HIGHEST

D_MODEL = 1024
DEPTH = 2
GRID_W = 64
N_Q_HEADS = 8
N_KV_HEADS = 2
HEAD_DIM = 64
ATTN_WIDTH = N_Q_HEADS * HEAD_DIM
KV_WIDTH = N_KV_HEADS * HEAD_DIM
ROPE_THETA = 10000.0
POOL_WINDOWS = (2, 4, 8, 16)
POOL_GROUP_DIM = 128
POOL_WIDTH = len(POOL_WINDOWS) * POOL_GROUP_DIM
MIX_IN_WIDTH = ATTN_WIDTH + 2 * KV_WIDTH + POOL_WIDTH
N_EXPERTS = 16
EXPERT_FF = 1024
CAPACITY_FACTOR = 2
NORM_EPS = 1e-6
DEEPNORM_ALPHA = (2 * DEPTH) ** 0.25

LANES = 128
SUBLANES = 8
HALO = SUBLANES
VMEM_LIMIT = 56 * 1024 * 1024

_NT = (((1,), (1,)), ((), ()))


def _cparams(sem):
    return pltpu.CompilerParams(dimension_semantics=sem, vmem_limit_bytes=VMEM_LIMIT)


def _layer_norm(z, g, b):
    mu = jnp.mean(z, axis=-1, keepdims=True)
    zc = z - mu
    var = jnp.mean(zc * zc, axis=-1, keepdims=True)
    return zc * lax.rsqrt(var + NORM_EPS) * g + b


def _mod_kernel(c_ref, w_ref, b_ref, o_ref):
    c = c_ref[...]
    cond = c * jax.nn.sigmoid(c)
    o_ref[...] = jnp.dot(cond, w_ref[...], precision=HIGHEST, preferred_element_type=F32) + b_ref[...]


def _modulation(cond_rows, w_mod, b_mod):
    n_out = w_mod.shape[-1]
    tn = D_MODEL
    return pl.pallas_call(
        _mod_kernel,
        grid=(DEPTH, n_out // tn),
        in_specs=[
            pl.BlockSpec((SUBLANES, D_MODEL), lambda l, j: (0, 0)),
            pl.BlockSpec((None, D_MODEL, tn), lambda l, j: (l, 0, j)),
            pl.BlockSpec((None, 1, tn), lambda l, j: (l, 0, j)),
        ],
        out_specs=pl.BlockSpec((None, SUBLANES, tn), lambda l, j: (l, 0, j)),
        out_shape=jax.ShapeDtypeStruct((DEPTH, SUBLANES, n_out), F32),
        compiler_params=_cparams(("arbitrary", "arbitrary")),
        name="modulation",
    )(cond_rows, w_mod, b_mod.reshape(DEPTH, 1, n_out))


def _group_mean_sq(t, bd):
    t2 = t * t
    hi = t2.astype(BF16)
    lo = (t2 - hi.astype(F32)).astype(BF16)
    return (jnp.dot(hi, bd, preferred_element_type=F32) + jnp.dot(lo, bd, preferred_element_type=F32))


def _rope_chunk(y, c, s_signed):
    n = y.shape[-1]
    lane = lax.broadcasted_iota(I32, y.shape, 1)
    nxt = pltpu.roll(y, n - 1, axis=1)
    prv = pltpu.roll(y, 1, axis=1)
    partner = jnp.where((lane & 1) == 0, nxt, prv)
    return y * c + partner * s_signed


def _l0_in_kernel(x_ref, sh_ref, sc_ref, w_ref, bd_ref, qg_ref, kg_ref, cos_ref, sin_ref,
                  q_ref, k_ref, v_ref, p_ref):
    u = x_ref[...] * (1.0 + sc_ref[...]) + sh_ref[...]
    h = jnp.dot(u.astype(BF16), w_ref[...], preferred_element_type=F32)
    q = h[:, :ATTN_WIDTH]
    k = h[:, ATTN_WIDTH:ATTN_WIDTH + KV_WIDTH]
    bd = bd_ref[...]
    qn = q * lax.rsqrt(_group_mean_sq(q, bd) + NORM_EPS) * qg_ref[...]
    kn = k * lax.rsqrt(_group_mean_sq(k, bd[:KV_WIDTH, :KV_WIDTH]) + NORM_EPS) * kg_ref[...]
    scale = HEAD_DIM ** -0.5
    for c in range(ATTN_WIDTH // LANES):
        sl = slice(c * LANES, (c + 1) * LANES)
        q_ref[:, sl] = (_rope_chunk(qn[:, sl], cos_ref[:, sl], sin_ref[:, sl]) * scale).astype(BF16)
    k_ref[...] = _rope_chunk(kn, cos_ref[:, :KV_WIDTH], sin_ref[:, :KV_WIDTH]).astype(BF16)
    v_ref[...] = h[:, ATTN_WIDTH + KV_WIDTH:ATTN_WIDTH + 2 * KV_WIDTH].astype(BF16)
    p_ref[...] = h[:, ATTN_WIDTH + 2 * KV_WIDTH:]


def _l0_in_proj(x, shift, scale, w_in_bf, bd, qg, kg, cos_t, sin_t, ts):
    b, l, _ = x.shape
    per_batch = shift.shape[0] == b
    mod_map = (lambda bi, si: (bi, 0, 0)) if per_batch else (lambda bi, si: (0, 0, 0))
    full = lambda bi, si: (0, 0)
    return pl.pallas_call(
        _l0_in_kernel,
        grid=(b, l // ts),
        in_specs=[
            pl.BlockSpec((None, ts, D_MODEL), lambda bi, si: (bi, si, 0)),
            pl.BlockSpec((None, 1, D_MODEL), mod_map),
            pl.BlockSpec((None, 1, D_MODEL), mod_map),
            pl.BlockSpec((D_MODEL, MIX_IN_WIDTH), full),
            pl.BlockSpec((ATTN_WIDTH, ATTN_WIDTH), full),
            pl.BlockSpec((1, ATTN_WIDTH), full),
            pl.BlockSpec((1, KV_WIDTH), full),
            pl.BlockSpec((ts, ATTN_WIDTH), lambda bi, si: (si, 0)),
            pl.BlockSpec((ts, ATTN_WIDTH), lambda bi, si: (si, 0)),
        ],
        out_specs=[
            pl.BlockSpec((None, ts, ATTN_WIDTH), lambda bi, si: (bi, si, 0)),
            pl.BlockSpec((None, ts, KV_WIDTH), lambda bi, si: (bi, si, 0)),
            pl.BlockSpec((None, ts, KV_WIDTH), lambda bi, si: (bi, si, 0)),
            pl.BlockSpec((None, ts, POOL_WIDTH), lambda bi, si: (bi, si, 0)),
        ],
        out_shape=[
            jax.ShapeDtypeStruct((b, l, ATTN_WIDTH), BF16),
            jax.ShapeDtypeStruct((b, l, KV_WIDTH), BF16),
            jax.ShapeDtypeStruct((b, l, KV_WIDTH), BF16),
            jax.ShapeDtypeStruct((b, l, POOL_WIDTH), F32),
        ],
        compiler_params=_cparams(("arbitrary", "arbitrary")),
        name="l0_in_proj",
    )(x, shift, scale, w_in_bf, bd, qg, kg, cos_t, sin_t)


def _attn_kernel(q_ref, k_ref, v_ref, o_ref):
    heads_per_kv = N_Q_HEADS // N_KV_HEADS
    for pr in range(ATTN_WIDTH // LANES):
        g = (2 * pr) // heads_per_kv
        sl = slice(pr * LANES, (pr + 1) * LANES)
        qp = q_ref[:, sl]
        acc = jnp.zeros(qp.shape, F32)
        for par in range(2):
            s = lax.dot_general(qp, k_ref[2 * g + par], _NT, preferred_element_type=F32)
            m = jnp.max(s, axis=-1, keepdims=True)
            p = jnp.exp(s - m)
            l = jnp.sum(p, axis=-1, keepdims=True)
            o = jnp.dot(p.astype(BF16), v_ref[2 * g + par], preferred_element_type=F32)
            acc = acc + o / l
        o_ref[:, sl] = acc.astype(BF16)


def _attention(q, k_pad, v_pad, tq):
    b, s, _ = q.shape
    lk = k_pad.shape[2]
    return pl.pallas_call(
        _attn_kernel,
        grid=(b, s // tq),
        in_specs=[
            pl.BlockSpec((None, tq, ATTN_WIDTH), lambda bi, qi: (bi, qi, 0)),
            pl.BlockSpec((None, 4, lk, LANES), lambda bi, qi: (bi, 0, 0, 0)),
            pl.BlockSpec((None, 4, lk, LANES), lambda bi, qi: (bi, 0, 0, 0)),
        ],
        out_specs=pl.BlockSpec((None, tq, ATTN_WIDTH), lambda bi, qi: (bi, qi, 0)),
        out_shape=jax.ShapeDtypeStruct((b, s, ATTN_WIDTH), BF16),
        compiler_params=_cparams(("arbitrary", "arbitrary")),
        name="attention",
    )(q, k_pad, v_pad)


def _lane_half_pad(t):
    h0, h1 = t[..., :HEAD_DIM], t[..., HEAD_DIM:]
    z = jnp.zeros_like(h0)
    cat = lambda a, c: jnp.concatenate([a, c], axis=-1)
    return jnp.stack([cat(h0, z), cat(z, h0), cat(h1, z), cat(z, h1)], axis=1)


def _post_tail(resid, y, gate, lng, lnb, shf, scf, wr_t):
    h = _layer_norm(DEEPNORM_ALPHA * resid + gate * y, lng, lnb)
    u = h * (1.0 + scf) + shf
    logits = lax.dot_general(wr_t, u, _NT, precision=HIGHEST, preferred_element_type=F32)
    ex = jnp.exp(logits - jnp.max(logits, axis=0, keepdims=True))
    aff = ex / jnp.sum(ex, axis=0, keepdims=True)
    return h, u, aff


def _fill_halo(ext_ref, prev_ref, main_ref, next_ref, ts):
    st = pl.program_id(1)
    last = pl.num_programs(1) - 1
    ext_ref[0:HALO, :] = jnp.where(st > 0, prev_ref[...], 0.0)
    ext_ref[HALO:HALO + ts, :] = main_ref[...]
    ext_ref[HALO + ts:HALO + ts + HALO, :] = jnp.where(st < last, next_ref[...], 0.0)


def _halo_specs(ts, width, seq):
    r = ts // HALO
    nblk = seq // HALO
    return [
        pl.BlockSpec((None, ts, width), lambda bi, si: (bi, si, 0)),
        pl.BlockSpec((None, HALO, width), lambda bi, si: (bi, jnp.maximum(si * r - 1, 0), 0)),
        pl.BlockSpec((None, HALO, width), lambda bi, si: (bi, jnp.minimum((si + 1) * r, nblk - 1), 0)),
    ]


def _tail_specs(ts):
    row = lambda bi, si: (bi, 0, 0)
    full = lambda bi, si: (0, 0)
    in_specs = [
        pl.BlockSpec((None, ts, D_MODEL), lambda bi, si: (bi, si, 0)),
        pl.BlockSpec((None, 1, D_MODEL), row),
        pl.BlockSpec((1, D_MODEL), full),
        pl.BlockSpec((1, D_MODEL), full),
        pl.BlockSpec((None, 1, D_MODEL), row),
        pl.BlockSpec((None, 1, D_MODEL), row),
        pl.BlockSpec((N_EXPERTS, D_MODEL), full),
    ]
    out_specs = [
        pl.BlockSpec((None, ts, D_MODEL), lambda bi, si: (bi, si, 0)),
        pl.BlockSpec((None, ts, D_MODEL), lambda bi, si: (bi, si, 0)),
        pl.BlockSpec((None, N_EXPERTS, ts), lambda bi, si: (bi, 0, si)),
    ]
    return in_specs, out_specs


def _tail_out_shape(b, s):
    return [jax.ShapeDtypeStruct((b, s, D_MODEL), F32),
            jax.ShapeDtypeStruct((b, s, D_MODEL), F32),
            jax.ShapeDtypeStruct((b, N_EXPERTS, s), F32)]


def _l0_out_kernel(seq, a_ref, p_ref, pprev_ref, pnext_ref, wpool_ref, pscale_ref, wout_ref,
                   x_ref, gate_ref, lng_ref, lnb_ref, shf_ref, scf_ref, wr_ref,
                   h_ref, u_ref, aff_ref, pext_ref):
    ts = p_ref.shape[0]
    _fill_halo(pext_ref, pprev_ref, p_ref, pnext_ref, ts)
    t = pl.program_id(1) * ts + lax.broadcasted_iota(I32, (ts, 1), 0)
    outs = []
    for g, w in enumerate(POOL_WINDOWS):
        sl = slice(g * POOL_GROUP_DIM, (g + 1) * POOL_GROUP_DIM)
        acc = pext_ref[HALO - w // 2:HALO - w // 2 + ts, sl]
        for j in range(-(w // 2) + 1, w // 2):
            acc = acc + pext_ref[HALO + j:HALO + j + ts, sl]
        cnt = jnp.minimum(t + w // 2, seq) - jnp.maximum(t - w // 2, 0)
        pooled = acc / cnt.astype(F32) - p_ref[:, sl]
        og = jnp.dot(pooled.astype(BF16), wpool_ref[g], preferred_element_type=F32)
        outs.append((og * pscale_ref[:, sl]).astype(BF16))
    pool = jnp.concatenate(outs, axis=-1)
    y = (jnp.dot(a_ref[...], wout_ref[:ATTN_WIDTH, :], preferred_element_type=F32)
         + jnp.dot(pool, wout_ref[ATTN_WIDTH:, :], preferred_element_type=F32))
    h, u, aff = _post_tail(x_ref[...], y, gate_ref[...], lng_ref[...], lnb_ref[...],
                           shf_ref[...], scf_ref[...], wr_ref[...])
    h_ref[...] = h
    u_ref[...] = u
    aff_ref[...] = aff


def _l0_out_proj(a, p, wpool_bf, pscale, wout_bf, x, gate, lng, lnb, shf, scf, wr_t, ts):
    b, s, _ = x.shape
    full2 = lambda bi, si: (0, 0)
    tail_in, tail_out = _tail_specs(ts)
    in_specs = (
        [pl.BlockSpec((None, ts, ATTN_WIDTH), lambda bi, si: (bi, si, 0))]
        + _halo_specs(ts, POOL_WIDTH, s)
        + [pl.BlockSpec((len(POOL_WINDOWS), POOL_GROUP_DIM, POOL_GROUP_DIM), lambda bi, si: (0, 0, 0)),
           pl.BlockSpec((1, POOL_WIDTH), full2),
           pl.BlockSpec((ATTN_WIDTH + POOL_WIDTH, D_MODEL), full2)]
        + tail_in)
    return pl.pallas_call(
        functools.partial(_l0_out_kernel, s),
        grid=(b, s // ts),
        in_specs=in_specs,
        out_specs=tail_out,
        out_shape=_tail_out_shape(b, s),
        scratch_shapes=[pltpu.VMEM((ts + 2 * HALO, POOL_WIDTH), F32)],
        compiler_params=_cparams(("arbitrary", "arbitrary")),
        name="l0_out_proj",
    )(a, p, p, p, wpool_bf, pscale, wout_bf, x, gate, lng, lnb, shf, scf, wr_t)


def _exclusive_cumsum_lanes(mask, tri):
    n = mask.shape[1]
    w = tri.shape[0]
    carry = jnp.zeros((mask.shape[0], 1), F32)
    parts = []
    for c in range(n // w):
        blk = mask[:, c * w:(c + 1) * w]
        inc = jnp.dot(blk.astype(BF16), tri, preferred_element_type=F32)
        parts.append(inc - blk + carry)
        carry = carry + inc[:, w - 1:w]
    return jnp.concatenate(parts, axis=1)


def _select_kernel(cap, aff_ref, dest_ref):
    aff = aff_ref[...]
    e = aff.shape[0]
    as_f32 = lambda word: pltpu.bitcast(word, F32)

    def refine(i, thr):
        cand = thr | jnp.left_shift(jnp.int32(1), 30 - i)
        cnt = jnp.sum(jnp.where(aff >= as_f32(cand), 1.0, 0.0), axis=1, keepdims=True)
        return jnp.where(cnt >= cap, cand, thr)

    thr = lax.fori_loop(0, 31, refine, jnp.zeros((e, 1), I32))
    w = 2 * LANES
    r = lax.broadcasted_iota(I32, (w, w), 0)
    c = lax.broadcasted_iota(I32, (w, w), 1)
    tri = jnp.where(r <= c, 1.0, 0.0).astype(BF16)
    gt = jnp.where(aff >= as_f32(thr + 1), 1.0, 0.0)
    eq = jnp.where(aff >= as_f32(thr), 1.0, 0.0) - gt
    need = cap - jnp.sum(gt, axis=1, keepdims=True)
    sel = gt + eq * jnp.where(_exclusive_cumsum_lanes(eq, tri) < need, 1.0, 0.0)
    pos = _exclusive_cumsum_lanes(sel, tri)
    dest_ref[...] = jnp.where(sel > 0.5, pos, float(cap)).astype(I32)


def _select(aff, cap):
    b, e, n = aff.shape
    return pl.pallas_call(
        functools.partial(_select_kernel, cap),
        grid=(b,),
        in_specs=[pl.BlockSpec((None, e, n), lambda bi: (bi, 0, 0))],
        out_specs=pl.BlockSpec((None, e, n), lambda bi: (bi, 0, 0)),
        out_shape=jax.ShapeDtypeStruct((b, e, n), I32),
        compiler_params=_cparams(("arbitrary",)),
        name="expert_select",
    )(aff)


def _expert_kernel(cap, seq, idx_ref, u_hbm, wg_ref, wu_ref, wd_ref, gate_ref, y_ref,
                   xe_ref, wg_bf, wu_bf, wd_bf, sem):
    e = pl.program_id(0)
    b = pl.program_id(1)

    @pl.when(b == 0)
    def _():
        wg_bf[...] = wg_ref[...].astype(BF16)
        wu_bf[...] = wu_ref[...].astype(BF16)
        wd_bf[...] = wd_ref[...].astype(BF16)

    base = (b * N_EXPERTS + e) * cap
    row0 = b * seq

    def row_copy(i):
        tok = row0 + idx_ref[base + i]
        return pltpu.make_async_copy(u_hbm.at[pl.ds(tok, 1), :], xe_ref.at[pl.ds(i, 1), :], sem)

    def issue(ci, carry):
        for j in range(SUBLANES):
            row_copy(ci * SUBLANES + j).start()
        return carry

    lax.fori_loop(0, cap // SUBLANES, issue, 0)
    pltpu.make_async_copy(u_hbm.at[pl.ds(0, cap), :], xe_ref, sem).wait()

    x = xe_ref[...].astype(BF16)
    hg = jnp.dot(x, wg_bf[...], preferred_element_type=F32)
    hu = jnp.dot(x, wu_bf[...], preferred_element_type=F32)
    act = (hg * jax.nn.sigmoid(hg) * hu).astype(BF16)
    y_ref[...] = jnp.dot(act, wd_bf[...], preferred_element_type=F32) * gate_ref[...]


def _expert_ffn(idx_flat, u_rows, layer, w_g, w_u, w_d, gate_sel, batch, seq, cap):
    wspec = lambda: pl.BlockSpec((None, None, D_MODEL, EXPERT_FF), lambda e, b, idx: (layer, e, 0, 0))
    grid_spec = pltpu.PrefetchScalarGridSpec(
        num_scalar_prefetch=1,
        grid=(N_EXPERTS, batch),
        in_specs=[
            pl.BlockSpec(memory_space=pl.ANY),
            wspec(), wspec(),
            pl.BlockSpec((None, None, EXPERT_FF, D_MODEL), lambda e, b, idx: (layer, e, 0, 0)),
            pl.BlockSpec((None, None, cap, 1), lambda e, b, idx: (b, e, 0, 0)),
        ],
        out_specs=pl.BlockSpec((None, None, cap, D_MODEL), lambda e, b, idx: (b, e, 0, 0)),
        scratch_shapes=[
            pltpu.VMEM((cap, D_MODEL), F32),
            pltpu.VMEM((D_MODEL, EXPERT_FF), BF16),
            pltpu.VMEM((D_MODEL, EXPERT_FF), BF16),
            pltpu.VMEM((EXPERT_FF, D_MODEL), BF16),
            pltpu.SemaphoreType.DMA,
        ],
    )
    return pl.pallas_call(
        functools.partial(_expert_kernel, cap, seq),
        grid_spec=grid_spec,
        out_shape=jax.ShapeDtypeStruct((batch, N_EXPERTS, cap, D_MODEL), F32),
        compiler_params=_cparams(("arbitrary", "arbitrary")),
        name="expert_ffn",
    )(idx_flat, u_rows, w_g, w_u, w_d, gate_sel)


def _combine_kernel(cap, idx_ref, y_ref, f_ref):
    b = pl.program_id(0)
    e = pl.program_id(1)

    @pl.when(e == 0)
    def _():
        f_ref[...] = jnp.zeros_like(f_ref)

    base = (b * N_EXPERTS + e) * cap
    sub = lax.broadcasted_iota(I32, (SUBLANES, D_MODEL), 0)

    def chunk(ci, carry):
        r0 = pl.multiple_of(ci * SUBLANES, SUBLANES)
        rows = y_ref[pl.ds(r0, SUBLANES), :]
        for j in range(SUBLANES):
            tok = idx_ref[base + r0 + j]
            t0 = pl.multiple_of((tok >> 3) << 3, SUBLANES)
            add = jnp.where(sub == (tok & (SUBLANES - 1)),
                            jnp.broadcast_to(rows[j:j + 1, :], (SUBLANES, D_MODEL)), 0.0)
            f_ref[pl.ds(t0, SUBLANES), :] = f_ref[pl.ds(t0, SUBLANES), :] + add
        return carry

    lax.fori_loop(0, cap // SUBLANES, chunk, 0)


def _combine(idx_flat, y, seq, cap):
    batch = y.shape[0]
    grid_spec = pltpu.PrefetchScalarGridSpec(
        num_scalar_prefetch=1,
        grid=(batch, N_EXPERTS),
        in_specs=[pl.BlockSpec((None, None, cap, D_MODEL), lambda b, e, idx: (b, e, 0, 0))],
        out_specs=pl.BlockSpec((None, seq, D_MODEL), lambda b, e, idx: (b, 0, 0)),
    )
    return pl.pallas_call(
        functools.partial(_combine_kernel, cap),
        grid_spec=grid_spec,
        out_shape=jax.ShapeDtypeStruct((batch, seq, D_MODEL), F32),
        compiler_params=_cparams(("arbitrary", "arbitrary")),
        name="expert_combine",
    )(idx_flat, y)


_TOKEN_SPLIT = 64
_COMPACT_COLS = 16
_COMPACT_CHUNK = 1024


def _compact_kernel(cap, dest_ref, aff_ref, out_ref):
    n = dest_ref.shape[1]
    acc = jnp.zeros((cap, _COMPACT_COLS), F32)
    for c0 in range(0, n, _COMPACT_CHUNK):
        dest = dest_ref[:, c0:c0 + _COMPACT_CHUNK]
        aff = aff_ref[:, c0:c0 + _COMPACT_CHUNK]
        slot = lax.broadcasted_iota(I32, (cap, _COMPACT_CHUNK), 0)
        onehot = jnp.where(dest == slot, 1.0, 0.0).astype(BF16)
        t = c0 + lax.broadcasted_iota(I32, (1, _COMPACT_CHUNK), 1)
        a1 = aff.astype(BF16).astype(F32)
        r1 = aff - a1
        a2 = r1.astype(BF16).astype(F32)
        a3 = r1 - a2
        row = lax.broadcasted_iota(I32, (_COMPACT_COLS, _COMPACT_CHUNK), 0)
        vals = jnp.where(row == 0, (t // _TOKEN_SPLIT).astype(F32),
               jnp.where(row == 1, (t % _TOKEN_SPLIT).astype(F32),
               jnp.where(row == 2, a1, jnp.where(row == 3, a2, jnp.where(row == 4, a3, 0.0)))))
        acc = acc + lax.dot_general(onehot, vals.astype(BF16), _NT, preferred_element_type=F32)
    out_ref[...] = acc


def _compact(dest, aff, cap):
    b, e, n = dest.shape
    row = lambda: pl.BlockSpec((None, None, 1, n), lambda bi, ei: (bi, ei, 0, 0))
    return pl.pallas_call(
        functools.partial(_compact_kernel, cap),
        grid=(b, e),
        in_specs=[row(), row()],
        out_specs=pl.BlockSpec((None, None, cap, _COMPACT_COLS), lambda bi, ei: (bi, ei, 0, 0)),
        out_shape=jax.ShapeDtypeStruct((b, e, cap, _COMPACT_COLS), F32),
        compiler_params=_cparams(("arbitrary", "arbitrary")),
        name="expert_compact",
    )(dest.reshape(b, e, 1, n), aff.reshape(b, e, 1, n))


def _moe(u, aff, layer, w_g, w_u, w_d):
    b, n, _ = u.shape
    cap = CAPACITY_FACTOR * n // N_EXPERTS
    dest = _select(aff, cap)
    packed = _compact(dest, aff, cap)
    idx_flat = (packed[..., 0] * _TOKEN_SPLIT + packed[..., 1]).astype(I32).reshape(-1)
    gate_sel = (packed[..., 2] + packed[..., 3] + packed[..., 4])[..., None]
    y = _expert_ffn(idx_flat, u.reshape(b * n, D_MODEL), layer, w_g, w_u, w_d, gate_sel, b, n, cap)
    return _combine(idx_flat, y, n, cap)


def _l1_in_kernel(h_ref, f_ref, gf_ref, lng_ref, lnb_ref, sh_ref, sc_ref, w_ref, h2_ref, bg_ref, cx_ref):
    h2 = _layer_norm(DEEPNORM_ALPHA * h_ref[...] + gf_ref[...] * f_ref[...], lng_ref[...], lnb_ref[...])
    h2_ref[...] = h2
    u = h2 * (1.0 + sc_ref[...]) + sh_ref[...]
    z = jnp.dot(u.astype(BF16), w_ref[...], preferred_element_type=F32)
    bg_ref[...] = z[:, :D_MODEL]
    cx_ref[...] = z[:, D_MODEL:2 * D_MODEL] * z[:, 2 * D_MODEL:]


def _l1_in_proj(h, f, gate_f, lng, lnb, shift, scale, w_bf, ts):
    b, s, _ = h.shape
    tok = lambda: pl.BlockSpec((None, ts, D_MODEL), lambda bi, si: (bi, si, 0))
    row = lambda: pl.BlockSpec((None, 1, D_MODEL), lambda bi, si: (bi, 0, 0))
    vec = lambda: pl.BlockSpec((1, D_MODEL), lambda bi, si: (0, 0))
    return pl.pallas_call(
        _l1_in_kernel,
        grid=(b, s // ts),
        in_specs=[tok(), tok(), row(), vec(), vec(), row(), row(),
                  pl.BlockSpec((D_MODEL, 3 * D_MODEL), lambda bi, si: (0, 0))],
        out_specs=[tok(), tok(), tok()],
        out_shape=[jax.ShapeDtypeStruct((b, s, D_MODEL), F32)] * 3,
        compiler_params=_cparams(("arbitrary", "arbitrary")),
        name="l1_in_proj",
    )(h, f, gate_f, lng, lnb, shift, scale, w_bf)


def _l1_out_kernel(cx_ref, cprev_ref, cnext_ref, bg_ref, cw_ref, wout_ref,
                   x_ref, gate_ref, lng_ref, lnb_ref, shf_ref, scf_ref, wr_ref,
                   h_ref, u_ref, aff_ref, cext_ref):
    ts = cx_ref.shape[0]
    _fill_halo(cext_ref, cprev_ref, cx_ref, cnext_ref, ts)
    z = (cext_ref[HALO - 1:HALO - 1 + ts, :] * cw_ref[0:1, :]
         + cx_ref[...] * cw_ref[1:2, :]
         + cext_ref[HALO + 1:HALO + 1 + ts, :] * cw_ref[2:3, :])
    y = jnp.dot((bg_ref[...] * z).astype(BF16), wout_ref[...], preferred_element_type=F32)
    h, u, aff = _post_tail(x_ref[...], y, gate_ref[...], lng_ref[...], lnb_ref[...],
                           shf_ref[...], scf_ref[...], wr_ref[...])
    h_ref[...] = h
    u_ref[...] = u
    aff_ref[...] = aff


def _l1_out_proj(cx, bg, conv_w, wout_bf, x, gate, lng, lnb, shf, scf, wr_t, ts):
    b, s, _ = x.shape
    tail_in, tail_out = _tail_specs(ts)
    in_specs = (
        _halo_specs(ts, D_MODEL, s)
        + [pl.BlockSpec((None, ts, D_MODEL), lambda bi, si: (bi, si, 0)),
           pl.BlockSpec(conv_w.shape, lambda bi, si: (0, 0)),
           pl.BlockSpec((D_MODEL, D_MODEL), lambda bi, si: (0, 0))]
        + tail_in)
    return pl.pallas_call(
        _l1_out_kernel,
        grid=(b, s // ts),
        in_specs=in_specs,
        out_specs=tail_out,
        out_shape=_tail_out_shape(b, s),
        scratch_shapes=[pltpu.VMEM((ts + 2 * HALO, D_MODEL), F32)],
        compiler_params=_cparams(("arbitrary", "arbitrary")),
        name="l1_out_proj",
    )(cx, cx, cx, bg, conv_w, wout_bf, x, gate, lng, lnb, shf, scf, wr_t)


def _final_kernel(h_ref, f_ref, gf_ref, lng_ref, lnb_ref, o_ref):
    o_ref[...] = _layer_norm(DEEPNORM_ALPHA * h_ref[...] + gf_ref[...] * f_ref[...], lng_ref[...], lnb_ref[...])


def _final_norm(h, f, gate_f, lng, lnb, ts):
    b, s, _ = h.shape
    tok = lambda: pl.BlockSpec((None, ts, D_MODEL), lambda bi, si: (bi, si, 0))
    return pl.pallas_call(
        _final_kernel,
        grid=(b, s // ts),
        in_specs=[tok(), tok(), pl.BlockSpec((None, 1, D_MODEL), lambda bi, si: (bi, 0, 0)),
                  pl.BlockSpec((1, D_MODEL), lambda bi, si: (0, 0)),
                  pl.BlockSpec((1, D_MODEL), lambda bi, si: (0, 0))],
        out_specs=tok(),
        out_shape=jax.ShapeDtypeStruct((b, s, D_MODEL), F32),
        compiler_params=_cparams(("arbitrary", "arbitrary")),
        name="final_norm",
    )(h, f, gate_f, lng, lnb)


def _rope_lane_tables(seq):
    rows = seq // GRID_W
    row = jnp.repeat(jnp.arange(rows, dtype=F32), GRID_W)
    col = jnp.tile(jnp.arange(GRID_W, dtype=F32), rows)
    axis_dim = HEAD_DIM // 2
    inv_freq = ROPE_THETA ** (-jnp.arange(0, axis_dim, 2, dtype=F32) / axis_dim)
    ang = jnp.concatenate([row[:, None] * inv_freq, col[:, None] * inv_freq], axis=-1)
    cos = jnp.repeat(jnp.cos(ang), 2, axis=-1)
    sin = jnp.repeat(jnp.sin(ang), 2, axis=-1)
    sign = jnp.tile(jnp.array([-1.0, 1.0], F32), HEAD_DIM // 2)
    return jnp.tile(cos, (1, N_Q_HEADS)), jnp.tile(sin * sign, (1, N_Q_HEADS))


def kernel(x, c, ctx, c_ctx, w_mod, b_mod, ln_mix_g, ln_mix_b, ln_ffn_g, ln_ffn_b, w_mix_in, q_norm_g, k_norm_g, w_pool_grp, pool_scale, w_mix_out, w_conv_in, conv_w, w_conv_out, w_router, w_exp_gate, w_exp_up, w_exp_down):
    b, s, d = x.shape
    lc = ctx.shape[1]
    ts = 512

    cond_rows = jnp.zeros((SUBLANES, d), F32).at[:b].set(c).at[b].set(c_ctx)
    mods = _modulation(cond_rows, w_mod, b_mod).reshape(DEPTH, SUBLANES, 6, d)
    m_lat = lambda layer, k: mods[layer, :b, k][:, None, :]
    m_ctx = lambda layer, k: mods[layer, b:b + 1, k][:, None, :]
    vec = lambda t: t.reshape(1, -1)

    w_in_bf = w_mix_in[0].astype(BF16)
    grp = lax.broadcasted_iota(I32, (ATTN_WIDTH, ATTN_WIDTH), 0) // HEAD_DIM
    bd = jnp.where(grp == grp.T, 1.0 / HEAD_DIM, 0.0).astype(BF16)
    qg = jnp.tile(q_norm_g[0], N_Q_HEADS).reshape(1, -1)
    kg = jnp.tile(k_norm_g[0], N_KV_HEADS).reshape(1, -1)
    cos_t, sin_t = _rope_lane_tables(s)
    q, k, v, p = _l0_in_proj(x, m_lat(0, 0), m_lat(0, 1), w_in_bf, bd, qg, kg, cos_t, sin_t, ts)
    ones = jnp.ones((lc, ATTN_WIDTH), F32)
    _, kc, vc, _ = _l0_in_proj(ctx, m_ctx(0, 0), m_ctx(0, 1), w_in_bf, bd, qg, kg, ones, 0.0 * ones, lc)
    k_pad = _lane_half_pad(jnp.concatenate([kc, k], axis=1))
    v_pad = _lane_half_pad(jnp.concatenate([vc, v], axis=1))
    a = _attention(q, k_pad, v_pad, 256)
    h, u, aff = _l0_out_proj(a, p, w_pool_grp[0].astype(BF16), vec(pool_scale[0]), w_mix_out[0].astype(BF16),
                             x, m_lat(0, 2), vec(ln_mix_g[0]), vec(ln_mix_b[0]), m_lat(0, 3), m_lat(0, 4),
                             w_router[0].T, ts)
    f = _moe(u, aff, 0, w_exp_gate, w_exp_up, w_exp_down)

    h, bg, cx = _l1_in_proj(h, f, m_lat(0, 5), vec(ln_ffn_g[0]), vec(ln_ffn_b[0]), m_lat(1, 0), m_lat(1, 1),
                            w_conv_in[0].astype(BF16), ts)
    h, u, aff = _l1_out_proj(cx, bg, conv_w[0], w_conv_out[0].astype(BF16),
                             h, m_lat(1, 2), vec(ln_mix_g[1]), vec(ln_mix_b[1]), m_lat(1, 3), m_lat(1, 4),
                             w_router[1].T, ts)
    f = _moe(u, aff, 1, w_exp_gate, w_exp_up, w_exp_down)
    return _final_norm(h, f, m_lat(1, 5), vec(ln_ffn_g[1]), vec(ln_ffn_b[1]), ts)
```

```python
import functools

import jax
import jax.numpy as jnp
from jax import lax
from jax.experimental import pallas as pl
from jax.experimental.pallas import tpu as pltpu

F32, BF16, I32 = jnp.float32, jnp.bfloat16, jnp.int32
HIGHEST = lax.Precision.HIGHEST

D_MODEL = 1024
DEPTH = 2
GRID_W = 64
N_Q_HEADS = 8
N_KV_HEADS = 2
HEAD_DIM = 64
ATTN_WIDTH = N_Q_HEADS * HEAD_DIM
KV_WIDTH = N_KV_HEADS * HEAD_DIM
ROPE_THETA = 10000.0
POOL_WINDOWS = (2, 4, 8, 16)
POOL_GROUP_DIM = 128
POOL_WIDTH = len(POOL_WINDOWS) * POOL_GROUP_DIM
MIX_IN_WIDTH = ATTN_WIDTH + 2 * KV_WIDTH + POOL_WIDTH
N_EXPERTS = 16
EXPERT_FF = 1024
CAPACITY_FACTOR = 2
NORM_EPS = 1e-6
DEEPNORM_ALPHA = (2 * DEPTH) ** 0.25
LOG2_E = 1.4426950408889634

LANES = 128
SUBLANES = 8
HALO = SUBLANES
VMEM_LIMIT = 56 * 1024 * 1024

_NT = (((1,), (1,)), ((), ()))


def _cparams(sem):
    return pltpu.CompilerParams(dimension_semantics=sem, vmem_limit_bytes=VMEM_LIMIT)


TOKEN_TILE_ROWS = D_MODEL // LANES


def _store_token_major(ref, rows):
    n = rows.shape[0]
    for j in range(TOKEN_TILE_ROWS):
        ref[pl.ds(j, n, stride=TOKEN_TILE_ROWS), :] = rows[:, j * LANES:(j + 1) * LANES]


def _load_token_major(ref, n):
    return jnp.concatenate(
        [ref[pl.ds(j, n, stride=TOKEN_TILE_ROWS), :] for j in range(TOKEN_TILE_ROWS)], axis=1)


def _layer_norm(z, g, b):
    mu = jnp.mean(z, axis=-1, keepdims=True)
    zc = z - mu
    var = jnp.mean(zc * zc, axis=-1, keepdims=True)
    return zc * lax.rsqrt(var + NORM_EPS) * g + b


def _mod_kernel(c_ref, w_ref, b_ref, o_ref):
    c = c_ref[...]
    cond = c * jax.nn.sigmoid(c)
    o_ref[...] = jnp.dot(cond, w_ref[...], precision=HIGHEST, preferred_element_type=F32) + b_ref[...]


def _modulation(cond_rows, w_mod, b_mod):
    n_out = w_mod.shape[-1]
    tn = D_MODEL
    return pl.pallas_call(
        _mod_kernel,
        grid=(DEPTH, n_out // tn),
        in_specs=[
            pl.BlockSpec((SUBLANES, D_MODEL), lambda l, j: (0, 0)),
            pl.BlockSpec((None, D_MODEL, tn), lambda l, j: (l, 0, j)),
            pl.BlockSpec((None, 1, tn), lambda l, j: (l, 0, j)),
        ],
        out_specs=pl.BlockSpec((None, SUBLANES, tn), lambda l, j: (l, 0, j)),
        out_shape=jax.ShapeDtypeStruct((DEPTH, SUBLANES, n_out), F32),
        compiler_params=_cparams(("arbitrary", "arbitrary")),
        name="modulation",
    )(cond_rows, w_mod, b_mod.reshape(DEPTH, 1, n_out))


def _group_mean_sq(t, bd):
    t2 = t * t
    hi = t2.astype(BF16)
    lo = (t2 - hi.astype(F32)).astype(BF16)
    return (jnp.dot(hi, bd, preferred_element_type=F32) + jnp.dot(lo, bd, preferred_element_type=F32))


def _rope_chunk(y, c, s_signed):
    n = y.shape[-1]
    lane = lax.broadcasted_iota(I32, y.shape, 1)
    nxt = pltpu.roll(y, n - 1, axis=1)
    prv = pltpu.roll(y, 1, axis=1)
    partner = jnp.where((lane & 1) == 0, nxt, prv)
    return y * c + partner * s_signed


def _l0_in_kernel(x_ref, sh_ref, sc_ref, w_ref, bd_ref, qg_ref, kg_ref, cos_ref, sin_ref,
                  q_ref, k_ref, v_ref, p_ref):
    u = x_ref[...] * (1.0 + sc_ref[...]) + sh_ref[...]
    h = jnp.dot(u.astype(BF16), w_ref[...], preferred_element_type=F32)
    q = h[:, :ATTN_WIDTH]
    k = h[:, ATTN_WIDTH:ATTN_WIDTH + KV_WIDTH]
    bd = bd_ref[...]
    qn = q * lax.rsqrt(_group_mean_sq(q, bd) + NORM_EPS) * qg_ref[...]
    kn = k * lax.rsqrt(_group_mean_sq(k, bd[:KV_WIDTH, :KV_WIDTH]) + NORM_EPS) * kg_ref[...]
    scale = HEAD_DIM ** -0.5 * LOG2_E
    heads_per_kv = N_Q_HEADS // N_KV_HEADS
    lane = lax.broadcasted_iota(I32, (x_ref.shape[0], LANES), 1)
    for c in range(ATTN_WIDTH // LANES):
        sl = slice(c * LANES, (c + 1) * LANES)
        pair = _rope_chunk(qn[:, sl], cos_ref[:, sl], sin_ref[:, sl]) * scale
        swapped = pltpu.roll(pair, HEAD_DIM, axis=1)
        g = (2 * c) // heads_per_kv
        keep = (lane < HEAD_DIM) if g == 0 else (lane >= HEAD_DIM)
        first, second = (pair, swapped) if g == 0 else (swapped, pair)
        q_ref[:, (2 * c) * LANES:(2 * c + 1) * LANES] = jnp.where(keep, first, 0.0).astype(BF16)
        q_ref[:, (2 * c + 1) * LANES:(2 * c + 2) * LANES] = jnp.where(keep, second, 0.0).astype(BF16)
    k_ref[...] = _rope_chunk(kn, cos_ref[:, :KV_WIDTH], sin_ref[:, :KV_WIDTH]).astype(BF16)
    v_ref[...] = h[:, ATTN_WIDTH + KV_WIDTH:ATTN_WIDTH + 2 * KV_WIDTH].astype(BF16)
    p_ref[...] = h[:, ATTN_WIDTH + 2 * KV_WIDTH:]


def _l0_in_proj(x, shift, scale, w_in_bf, bd, qg, kg, cos_t, sin_t, ts):
    b, l, _ = x.shape
    per_batch = shift.shape[0] == b
    mod_map = (lambda bi, si: (bi, 0, 0)) if per_batch else (lambda bi, si: (0, 0, 0))
    full = lambda bi, si: (0, 0)
    return pl.pallas_call(
        _l0_in_kernel,
        grid=(b, l // ts),
        in_specs=[
            pl.BlockSpec((None, ts, D_MODEL), lambda bi, si: (bi, si, 0)),
            pl.BlockSpec((None, 1, D_MODEL), mod_map),
            pl.BlockSpec((None, 1, D_MODEL), mod_map),
            pl.BlockSpec((D_MODEL, MIX_IN_WIDTH), full),
            pl.BlockSpec((ATTN_WIDTH, ATTN_WIDTH), full),
            pl.BlockSpec((1, ATTN_WIDTH), full),
            pl.BlockSpec((1, KV_WIDTH), full),
            pl.BlockSpec((ts, ATTN_WIDTH), lambda bi, si: (si, 0)),
            pl.BlockSpec((ts, ATTN_WIDTH), lambda bi, si: (si, 0)),
        ],
        out_specs=[
            pl.BlockSpec((None, ts, N_Q_HEADS * LANES), lambda bi, si: (bi, si, 0)),
            pl.BlockSpec((None, ts, KV_WIDTH), lambda bi, si: (bi, si, 0)),
            pl.BlockSpec((None, ts, KV_WIDTH), lambda bi, si: (bi, si, 0)),
            pl.BlockSpec((None, ts, POOL_WIDTH), lambda bi, si: (bi, si, 0)),
        ],
        out_shape=[
            jax.ShapeDtypeStruct((b, l, N_Q_HEADS * LANES), BF16),
            jax.ShapeDtypeStruct((b, l, KV_WIDTH), BF16),
            jax.ShapeDtypeStruct((b, l, KV_WIDTH), BF16),
            jax.ShapeDtypeStruct((b, l, POOL_WIDTH), F32),
        ],
        compiler_params=_cparams(("arbitrary", "arbitrary")),
        name="l0_in_proj",
    )(x, shift, scale, w_in_bf, bd, qg, kg, cos_t, sin_t)


_VT_ROWS = HEAD_DIM + 16


_KV_CHUNK = 2176
_MAX_ROWS = 64


def _attn_kernel(q_ref, k_ref, vt_ref, o_ref, ot_ref, *s_refs):
    heads_per_kv = N_Q_HEADS // N_KV_HEADS
    lk, tq = k_ref.shape[0], q_ref.shape[0]
    chunks = [(c0, min(_KV_CHUNK, lk - c0)) for c0 in range(0, lk, _KV_CHUNK)]

    def scores(h, c0, n, m):
        qh = q_ref[:, h * LANES:(h + 1) * LANES]
        s_c = lax.dot_general(k_ref[c0:c0 + n, :], qh, _NT, preferred_element_type=F32)
        s_refs[h % 2][c0:c0 + n, :] = s_c
        mc = jnp.max(s_c.reshape(n // _MAX_ROWS, _MAX_ROWS, tq), axis=0)
        return mc if m is None else jnp.maximum(m, mc)

    def weighted(h, c0, n, m, o_t):
        p_c = jnp.exp2(s_refs[h % 2][c0:c0 + n, :] - m).astype(BF16)
        o_c = jnp.dot(vt_ref[h // heads_per_kv, :, c0:c0 + n], p_c, preferred_element_type=F32)
        return o_c if o_t is None else o_t + o_c

    m_next = None
    for c0, n in chunks:
        m_next = scores(0, c0, n, m_next)
    for h in range(N_Q_HEADS):
        g = h // heads_per_kv
        m = jnp.max(m_next, axis=0, keepdims=True)
        m_next = None
        o_t = None
        for c0, n in chunks:
            o_t = weighted(h, c0, n, m, o_t)
            if h + 1 < N_Q_HEADS:
                m_next = scores(h + 1, c0, n, m_next)
        ot_ref[h * HEAD_DIM:(h + 1) * HEAD_DIM, :] = (
            o_t[:HEAD_DIM, :] / o_t[HEAD_DIM:HEAD_DIM + 1, :])
    o_ref[...] = ot_ref[...].T.astype(BF16)


_BOUNDED_KV_CHUNK = 1088


def _attn_bounded_kernel(bound_ref, q_ref, k_ref, vt_ref, o_ref, ot_ref):
    heads_per_kv = N_Q_HEADS // N_KV_HEADS
    lk = k_ref.shape[0]
    shift = bound_ref[0]
    for h in range(N_Q_HEADS):
        g = h // heads_per_kv
        qh = q_ref[:, h * LANES:(h + 1) * LANES]
        o_t = None
        for c0 in range(0, lk, _BOUNDED_KV_CHUNK):
            n = min(_BOUNDED_KV_CHUNK, lk - c0)
            s_c = lax.dot_general(k_ref[c0:c0 + n, :], qh, _NT, preferred_element_type=F32)
            p_c = jnp.exp2(s_c - shift).astype(BF16)
            o_c = jnp.dot(vt_ref[g, :, c0:c0 + n], p_c, preferred_element_type=F32)
            o_t = o_c if o_t is None else o_t + o_c
        ot_ref[h * HEAD_DIM:(h + 1) * HEAD_DIM, :] = (
            o_t[:HEAD_DIM, :] / o_t[HEAD_DIM:HEAD_DIM + 1, :])
    o_ref[...] = ot_ref[...].T.astype(BF16)


_MAX_SAFE_SCORE_BOUND = 40.0


def _attention(q, k_all, vt, score_bound, tq):
    b, s, qw = q.shape
    lk = k_all.shape[1]
    specs = dict(
        grid=(b, s // tq),
        out_specs=pl.BlockSpec((None, tq, ATTN_WIDTH), lambda bi, qi: (bi, qi, 0)),
        out_shape=jax.ShapeDtypeStruct((b, s, ATTN_WIDTH), BF16),
        compiler_params=_cparams(("arbitrary", "arbitrary")),
    )
    in_specs = [
        pl.BlockSpec((None, tq, qw), lambda bi, qi: (bi, qi, 0)),
        pl.BlockSpec((None, lk, KV_WIDTH), lambda bi, qi: (bi, 0, 0)),
        pl.BlockSpec((None, N_KV_HEADS, _VT_ROWS, lk), lambda bi, qi: (bi, 0, 0, 0)),
    ]

    def bounded(_):
        return pl.pallas_call(
            _attn_bounded_kernel,
            in_specs=[pl.BlockSpec(memory_space=pltpu.SMEM)] + in_specs,
            scratch_shapes=[pltpu.VMEM((ATTN_WIDTH, tq), F32)],
            name="attention_bounded", **specs,
        )(score_bound.reshape(1), q, k_all, vt)

    def exact(_):
        return pl.pallas_call(
            _attn_kernel,
            in_specs=in_specs,
            scratch_shapes=[pltpu.VMEM((ATTN_WIDTH, tq), F32), pltpu.VMEM((lk, tq), F32),
                            pltpu.VMEM((lk, tq), F32)],
            name="attention", **specs,
        )(q, k_all, vt)

    return lax.cond(score_bound < _MAX_SAFE_SCORE_BOUND, bounded, exact, None)


def _post_tail(resid, y, gate, lng, lnb, shf, scf, wr_t):
    h = _layer_norm(DEEPNORM_ALPHA * resid + gate * y, lng, lnb)
    u = h * (1.0 + scf) + shf
    logits = lax.dot_general(wr_t, u, _NT, precision=HIGHEST, preferred_element_type=F32)
    ex = jnp.exp(logits - jnp.max(logits, axis=0, keepdims=True))
    aff = ex / jnp.sum(ex, axis=0, keepdims=True)
    return h, u, aff


def _fill_halo(ext_ref, prev_ref, main_ref, next_ref, ts):
    st = pl.program_id(1)
    last = pl.num_programs(1) - 1
    ext_ref[0:HALO, :] = jnp.where(st > 0, prev_ref[...], 0.0)
    ext_ref[HALO:HALO + ts, :] = main_ref[...]
    ext_ref[HALO + ts:HALO + ts + HALO, :] = jnp.where(st < last, next_ref[...], 0.0)


def _halo_specs(ts, width, seq):
    r = ts // HALO
    nblk = seq // HALO
    return [
        pl.BlockSpec((None, ts, width), lambda bi, si: (bi, si, 0)),
        pl.BlockSpec((None, HALO, width), lambda bi, si: (bi, jnp.maximum(si * r - 1, 0), 0)),
        pl.BlockSpec((None, HALO, width), lambda bi, si: (bi, jnp.minimum((si + 1) * r, nblk - 1), 0)),
    ]


def _tail_specs(ts):
    row = lambda bi, si: (bi, 0, 0)
    full = lambda bi, si: (0, 0)
    in_specs = [
        pl.BlockSpec((None, ts, D_MODEL), lambda bi, si: (bi, si, 0)),
        pl.BlockSpec((None, 1, D_MODEL), row),
        pl.BlockSpec((1, D_MODEL), full),
        pl.BlockSpec((1, D_MODEL), full),
        pl.BlockSpec((None, 1, D_MODEL), row),
        pl.BlockSpec((None, 1, D_MODEL), row),
        pl.BlockSpec((N_EXPERTS, D_MODEL), full),
    ]
    out_specs = [
        pl.BlockSpec((None, ts, D_MODEL), lambda bi, si: (bi, si, 0)),
        pl.BlockSpec((None, ts * TOKEN_TILE_ROWS, LANES), lambda bi, si: (bi, si, 0)),
        pl.BlockSpec((None, N_EXPERTS, ts), lambda bi, si: (bi, 0, si)),
    ]
    return in_specs, out_specs


def _tail_out_shape(b, s):
    return [jax.ShapeDtypeStruct((b, s, D_MODEL), F32),
            jax.ShapeDtypeStruct((b, s * TOKEN_TILE_ROWS, LANES), F32),
            jax.ShapeDtypeStruct((b, N_EXPERTS, s), F32)]


def _l0_out_kernel(seq, a_ref, p_ref, pprev_ref, pnext_ref, wpool_ref, pscale_ref, wout_ref,
                   x_ref, gate_ref, lng_ref, lnb_ref, shf_ref, scf_ref, wr_ref,
                   h_ref, u_ref, aff_ref, pext_ref):
    ts = p_ref.shape[0]
    _fill_halo(pext_ref, pprev_ref, p_ref, pnext_ref, ts)
    t = pl.program_id(1) * ts + lax.broadcasted_iota(I32, (ts, 1), 0)
    outs = []
    for g, w in enumerate(POOL_WINDOWS):
        sl = slice(g * POOL_GROUP_DIM, (g + 1) * POOL_GROUP_DIM)
        acc = pext_ref[HALO - w // 2:HALO - w // 2 + ts, sl]
        for j in range(-(w // 2) + 1, w // 2):
            acc = acc + pext_ref[HALO + j:HALO + j + ts, sl]
        cnt = jnp.minimum(t + w // 2, seq) - jnp.maximum(t - w // 2, 0)
        pooled = acc / cnt.astype(F32) - p_ref[:, sl]
        og = jnp.dot(pooled.astype(BF16), wpool_ref[g], preferred_element_type=F32)
        outs.append((og * pscale_ref[:, sl]).astype(BF16))
    pool = jnp.concatenate(outs, axis=-1)
    y = (jnp.dot(a_ref[...], wout_ref[:ATTN_WIDTH, :], preferred_element_type=F32)
         + jnp.dot(pool, wout_ref[ATTN_WIDTH:, :], preferred_element_type=F32))
    h, u, aff = _post_tail(x_ref[...], y, gate_ref[...], lng_ref[...], lnb_ref[...],
                           shf_ref[...], scf_ref[...], wr_ref[...])
    h_ref[...] = h
    _store_token_major(u_ref, u)
    aff_ref[...] = aff


def _l0_out_proj(a, p, wpool_bf, pscale, wout_bf, x, gate, lng, lnb, shf, scf, wr_t, ts):
    b, s, _ = x.shape
    full2 = lambda bi, si: (0, 0)
    tail_in, tail_out = _tail_specs(ts)
    in_specs = (
        [pl.BlockSpec((None, ts, ATTN_WIDTH), lambda bi, si: (bi, si, 0))]
        + _halo_specs(ts, POOL_WIDTH, s)
        + [pl.BlockSpec((len(POOL_WINDOWS), POOL_GROUP_DIM, POOL_GROUP_DIM), lambda bi, si: (0, 0, 0)),
           pl.BlockSpec((1, POOL_WIDTH), full2),
           pl.BlockSpec((ATTN_WIDTH + POOL_WIDTH, D_MODEL), full2)]
        + tail_in)
    return pl.pallas_call(
        functools.partial(_l0_out_kernel, s),
        grid=(b, s // ts),
        in_specs=in_specs,
        out_specs=tail_out,
        out_shape=_tail_out_shape(b, s),
        scratch_shapes=[pltpu.VMEM((ts + 2 * HALO, POOL_WIDTH), F32)],
        compiler_params=_cparams(("arbitrary", "arbitrary")),
        name="l0_out_proj",
    )(a, p, p, p, wpool_bf, pscale, wout_bf, x, gate, lng, lnb, shf, scf, wr_t)


def _exclusive_cumsum_lanes(mask, tri):
    n = mask.shape[1]
    w = tri.shape[0]
    carry = jnp.zeros((mask.shape[0], 1), F32)
    parts = []
    for c in range(n // w):
        blk = mask[:, c * w:(c + 1) * w]
        inc = jnp.dot(blk.astype(BF16), tri, preferred_element_type=F32)
        parts.append(inc - blk + carry)
        carry = carry + inc[:, w - 1:w]
    return jnp.concatenate(parts, axis=1)


def _select_kernel(cap, aff_ref, dest_ref):
    aff = aff_ref[...]
    e = aff.shape[0]
    as_f32 = lambda word: pltpu.bitcast(word, F32)

    def refine(i, thr):
        cand = thr | jnp.left_shift(jnp.int32(1), 30 - i)
        cnt = jnp.sum(jnp.where(aff >= as_f32(cand), 1.0, 0.0), axis=1, keepdims=True)
        return jnp.where(cnt >= cap, cand, thr)

    thr = lax.fori_loop(0, 31, refine, jnp.zeros((e, 1), I32))
    w = 2 * LANES
    r = lax.broadcasted_iota(I32, (w, w), 0)
    c = lax.broadcasted_iota(I32, (w, w), 1)
    tri = jnp.where(r <= c, 1.0, 0.0).astype(BF16)
    gt = jnp.where(aff >= as_f32(thr + 1), 1.0, 0.0)
    eq = jnp.where(aff >= as_f32(thr), 1.0, 0.0) - gt
    need = cap - jnp.sum(gt, axis=1, keepdims=True)
    sel = gt + eq * jnp.where(_exclusive_cumsum_lanes(eq, tri) < need, 1.0, 0.0)
    pos = _exclusive_cumsum_lanes(sel, tri)
    dest_ref[...] = jnp.where(sel > 0.5, pos, float(cap)).astype(I32)


def _select(aff, cap):
    b, e, n = aff.shape
    return pl.pallas_call(
        functools.partial(_select_kernel, cap),
        grid=(b,),
        in_specs=[pl.BlockSpec((None, e, n), lambda bi: (bi, 0, 0))],
        out_specs=pl.BlockSpec((None, e, n), lambda bi: (bi, 0, 0)),
        out_shape=jax.ShapeDtypeStruct((b, e, n), I32),
        compiler_params=_cparams(("arbitrary",)),
        name="expert_select",
    )(aff)


def _token_tile(t):
    return pl.ds(pl.multiple_of(t * TOKEN_TILE_ROWS, TOKEN_TILE_ROWS), TOKEN_TILE_ROWS)


def _expert_kernel(cap, seq, idx_ref, u_hbm, wg_ref, wu_ref, wd_ref, gate_ref, y_ref,
                   xe_ref, wg_bf, wu_bf, wd_bf, sem):
    b = pl.program_id(1)
    nb = pl.num_programs(1)
    step = pl.program_id(0) * nb + b
    nsteps = pl.num_programs(0) * nb
    slot = step % 2

    def start_gather(k, into):
        ek = k // nb
        bk = k % nb
        base = (bk * N_EXPERTS + ek) * cap
        tok0 = bk * seq

        def issue(ci, carry):
            for j in range(SUBLANES):
                i = ci * SUBLANES + j
                tok = tok0 + idx_ref[base + i]
                pltpu.make_async_copy(u_hbm.at[_token_tile(tok), :], xe_ref.at[into, _token_tile(i), :],
                                      sem.at[into]).start()
            return carry

        lax.fori_loop(0, cap // SUBLANES, issue, 0)

    @pl.when(step == 0)
    def _():
        start_gather(step, slot)

    @pl.when(step + 1 < nsteps)
    def _():
        start_gather(step + 1, 1 - slot)

    @pl.when(b == 0)
    def _():
        wg_bf[...] = wg_ref[...].astype(BF16)
        wu_bf[...] = wu_ref[...].astype(BF16)
        wd_bf[...] = wd_ref[...].astype(BF16)

    pltpu.make_async_copy(u_hbm.at[pl.ds(0, cap * TOKEN_TILE_ROWS), :], xe_ref.at[slot], sem.at[slot]).wait()

    x = _load_token_major(xe_ref.at[slot], cap).astype(BF16)
    hg = jnp.dot(x, wg_bf[...], preferred_element_type=F32)
    hu = jnp.dot(x, wu_bf[...], preferred_element_type=F32)
    act = (hg * jax.nn.sigmoid(hg) * hu).astype(BF16)
    _store_token_major(y_ref, jnp.dot(act, wd_bf[...], preferred_element_type=F32) * gate_ref[...])


def _expert_ffn(idx_flat, u_rows, layer, w_g, w_u, w_d, gate_sel, batch, seq, cap):
    wspec = lambda: pl.BlockSpec((None, None, D_MODEL, EXPERT_FF), lambda e, b, idx: (layer, e, 0, 0))
    grid_spec = pltpu.PrefetchScalarGridSpec(
        num_scalar_prefetch=1,
        grid=(N_EXPERTS, batch),
        in_specs=[
            pl.BlockSpec(memory_space=pl.ANY),
            wspec(), wspec(),
            pl.BlockSpec((None, None, EXPERT_FF, D_MODEL), lambda e, b, idx: (layer, e, 0, 0)),
            pl.BlockSpec((None, None, cap, 1), lambda e, b, idx: (b, e, 0, 0)),
        ],
        out_specs=pl.BlockSpec((None, None, cap * TOKEN_TILE_ROWS, LANES), lambda e, b, idx: (b, e, 0, 0)),
        scratch_shapes=[
            pltpu.VMEM((2, cap * TOKEN_TILE_ROWS, LANES), F32),
            pltpu.VMEM((D_MODEL, EXPERT_FF), BF16),
            pltpu.VMEM((D_MODEL, EXPERT_FF), BF16),
            pltpu.VMEM((EXPERT_FF, D_MODEL), BF16),
            pltpu.SemaphoreType.DMA((2,)),
        ],
    )
    return pl.pallas_call(
        functools.partial(_expert_kernel, cap, seq),
        grid_spec=grid_spec,
        out_shape=jax.ShapeDtypeStruct((batch, N_EXPERTS, cap * TOKEN_TILE_ROWS, LANES), F32),
        compiler_params=_cparams(("arbitrary", "arbitrary")),
        name="expert_ffn",
    )(idx_flat, u_rows, w_g, w_u, w_d, gate_sel)


_COMBINE_UNROLL = 8


def _combine_kernel(cap, idx_ref, y_ref, f_ref):
    b = pl.program_id(0)
    e = pl.program_id(1)

    @pl.when(e == 0)
    def _():
        f_ref[...] = jnp.zeros_like(f_ref)

    base = (b * N_EXPERTS + e) * cap

    def chunk(ci, carry):
        i0 = ci * _COMBINE_UNROLL
        tiles = [_token_tile(idx_ref[base + i0 + j]) for j in range(_COMBINE_UNROLL)]
        sums = [f_ref[tiles[j], :] + y_ref[_token_tile(i0 + j), :] for j in range(_COMBINE_UNROLL)]
        for j in range(_COMBINE_UNROLL):
            f_ref[tiles[j], :] = sums[j]
        return carry

    lax.fori_loop(0, cap // _COMBINE_UNROLL, chunk, 0)


def _combine(idx_flat, y, seq, cap):
    batch = y.shape[0]
    grid_spec = pltpu.PrefetchScalarGridSpec(
        num_scalar_prefetch=1,
        grid=(batch, N_EXPERTS),
        in_specs=[pl.BlockSpec((None, None, cap * TOKEN_TILE_ROWS, LANES), lambda b, e, idx: (b, e, 0, 0))],
        out_specs=pl.BlockSpec((None, seq * TOKEN_TILE_ROWS, LANES), lambda b, e, idx: (b, 0, 0)),
    )
    return pl.pallas_call(
        functools.partial(_combine_kernel, cap),
        grid_spec=grid_spec,
        out_shape=jax.ShapeDtypeStruct((batch, seq * TOKEN_TILE_ROWS, LANES), F32),
        compiler_params=_cparams(("arbitrary", "arbitrary")),
        name="expert_combine",
    )(idx_flat, y)


_TOKEN_SPLIT = 64
_COMPACT_COLS = 16
_COMPACT_CHUNK = 1024


def _compact_kernel(cap, dest_ref, aff_ref, out_ref):
    n = dest_ref.shape[1]
    acc = jnp.zeros((cap, _COMPACT_COLS), F32)
    for c0 in range(0, n, _COMPACT_CHUNK):
        dest = dest_ref[:, c0:c0 + _COMPACT_CHUNK]
        aff = aff_ref[:, c0:c0 + _COMPACT_CHUNK]
        slot = lax.broadcasted_iota(I32, (cap, _COMPACT_CHUNK), 0)
        onehot = jnp.where(dest == slot, 1.0, 0.0).astype(BF16)
        t = c0 + lax.broadcasted_iota(I32, (1, _COMPACT_CHUNK), 1)
        a1 = aff.astype(BF16).astype(F32)
        r1 = aff - a1
        a2 = r1.astype(BF16).astype(F32)
        a3 = r1 - a2
        row = lax.broadcasted_iota(I32, (_COMPACT_COLS, _COMPACT_CHUNK), 0)
        vals = jnp.where(row == 0, (t // _TOKEN_SPLIT).astype(F32),
               jnp.where(row == 1, (t % _TOKEN_SPLIT).astype(F32),
               jnp.where(row == 2, a1, jnp.where(row == 3, a2, jnp.where(row == 4, a3, 0.0)))))
        acc = acc + lax.dot_general(onehot, vals.astype(BF16), _NT, preferred_element_type=F32)
    out_ref[...] = acc


def _compact(dest, aff, cap):
    b, e, n = dest.shape
    row = lambda: pl.BlockSpec((None, None, 1, n), lambda bi, ei: (bi, ei, 0, 0))
    return pl.pallas_call(
        functools.partial(_compact_kernel, cap),
        grid=(b, e),
        in_specs=[row(), row()],
        out_specs=pl.BlockSpec((None, None, cap, _COMPACT_COLS), lambda bi, ei: (bi, ei, 0, 0)),
        out_shape=jax.ShapeDtypeStruct((b, e, cap, _COMPACT_COLS), F32),
        compiler_params=_cparams(("arbitrary", "arbitrary")),
        name="expert_compact",
    )(dest.reshape(b, e, 1, n), aff.reshape(b, e, 1, n))


def _moe(u, aff, layer, w_g, w_u, w_d):
    b, _, n = aff.shape
    cap = CAPACITY_FACTOR * n // N_EXPERTS
    dest = _select(aff, cap)
    packed = _compact(dest, aff, cap)
    idx_flat = (packed[..., 0] * _TOKEN_SPLIT + packed[..., 1]).astype(I32).reshape(-1)
    gate_sel = (packed[..., 2] + packed[..., 3] + packed[..., 4])[..., None]
    y = _expert_ffn(idx_flat, u.reshape(b * n * TOKEN_TILE_ROWS, LANES), layer, w_g, w_u, w_d, gate_sel, b, n, cap)
    return _combine(idx_flat, y, n, cap)


def _l1_in_kernel(h_ref, f_ref, gf_ref, lng_ref, lnb_ref, sh_ref, sc_ref, w_ref, h2_ref, bg_ref, cx_ref):
    f = _load_token_major(f_ref, h_ref.shape[0])
    h2 = _layer_norm(DEEPNORM_ALPHA * h_ref[...] + gf_ref[...] * f, lng_ref[...], lnb_ref[...])
    h2_ref[...] = h2
    u = h2 * (1.0 + sc_ref[...]) + sh_ref[...]
    z = jnp.dot(u.astype(BF16), w_ref[...], preferred_element_type=F32)
    bg_ref[...] = z[:, :D_MODEL]
    cx_ref[...] = z[:, D_MODEL:2 * D_MODEL] * z[:, 2 * D_MODEL:]


def _l1_in_proj(h, f, gate_f, lng, lnb, shift, scale, w_bf, ts):
    b, s, _ = h.shape
    tok = lambda: pl.BlockSpec((None, ts, D_MODEL), lambda bi, si: (bi, si, 0))
    row = lambda: pl.BlockSpec((None, 1, D_MODEL), lambda bi, si: (bi, 0, 0))
    vec = lambda: pl.BlockSpec((1, D_MODEL), lambda bi, si: (0, 0))
    return pl.pallas_call(
        _l1_in_kernel,
        grid=(b, s // ts),
        in_specs=[tok(), pl.BlockSpec((None, ts * TOKEN_TILE_ROWS, LANES), lambda bi, si: (bi, si, 0)),
                  row(), vec(), vec(), row(), row(),
                  pl.BlockSpec((D_MODEL, 3 * D_MODEL), lambda bi, si: (0, 0))],
        out_specs=[tok(), tok(), tok()],
        out_shape=[jax.ShapeDtypeStruct((b, s, D_MODEL), F32)] * 3,
        compiler_params=_cparams(("arbitrary", "arbitrary")),
        name="l1_in_proj",
    )(h, f, gate_f, lng, lnb, shift, scale, w_bf)


def _l1_out_kernel(cx_ref, cprev_ref, cnext_ref, bg_ref, cw_ref, wout_ref,
                   x_ref, gate_ref, lng_ref, lnb_ref, shf_ref, scf_ref, wr_ref,
                   h_ref, u_ref, aff_ref, cext_ref):
    ts = cx_ref.shape[0]
    _fill_halo(cext_ref, cprev_ref, cx_ref, cnext_ref, ts)
    z = (cext_ref[HALO - 1:HALO - 1 + ts, :] * cw_ref[0:1, :]
         + cx_ref[...] * cw_ref[1:2, :]
         + cext_ref[HALO + 1:HALO + 1 + ts, :] * cw_ref[2:3, :])
    y = jnp.dot((bg_ref[...] * z).astype(BF16), wout_ref[...], preferred_element_type=F32)
    h, u, aff = _post_tail(x_ref[...], y, gate_ref[...], lng_ref[...], lnb_ref[...],
                           shf_ref[...], scf_ref[...], wr_ref[...])
    h_ref[...] = h
    _store_token_major(u_ref, u)
    aff_ref[...] = aff


def _l1_out_proj(cx, bg, conv_w, wout_bf, x, gate, lng, lnb, shf, scf, wr_t, ts):
    b, s, _ = x.shape
    tail_in, tail_out = _tail_specs(ts)
    in_specs = (
        _halo_specs(ts, D_MODEL, s)
        + [pl.BlockSpec((None, ts, D_MODEL), lambda bi, si: (bi, si, 0)),
           pl.BlockSpec(conv_w.shape, lambda bi, si: (0, 0)),
           pl.BlockSpec((D_MODEL, D_MODEL), lambda bi, si: (0, 0))]
        + tail_in)
    return pl.pallas_call(
        _l1_out_kernel,
        grid=(b, s // ts),
        in_specs=in_specs,
        out_specs=tail_out,
        out_shape=_tail_out_shape(b, s),
        scratch_shapes=[pltpu.VMEM((ts + 2 * HALO, D_MODEL), F32)],
        compiler_params=_cparams(("arbitrary", "arbitrary")),
        name="l1_out_proj",
    )(cx, cx, cx, bg, conv_w, wout_bf, x, gate, lng, lnb, shf, scf, wr_t)


def _final_kernel(h_ref, f_ref, gf_ref, lng_ref, lnb_ref, o_ref):
    f = _load_token_major(f_ref, h_ref.shape[0])
    o_ref[...] = _layer_norm(DEEPNORM_ALPHA * h_ref[...] + gf_ref[...] * f, lng_ref[...], lnb_ref[...])


def _final_norm(h, f, gate_f, lng, lnb, ts):
    b, s, _ = h.shape
    tok = lambda: pl.BlockSpec((None, ts, D_MODEL), lambda bi, si: (bi, si, 0))
    return pl.pallas_call(
        _final_kernel,
        grid=(b, s // ts),
        in_specs=[tok(), pl.BlockSpec((None, ts * TOKEN_TILE_ROWS, LANES), lambda bi, si: (bi, si, 0)),
                  pl.BlockSpec((None, 1, D_MODEL), lambda bi, si: (bi, 0, 0)),
                  pl.BlockSpec((1, D_MODEL), lambda bi, si: (0, 0)),
                  pl.BlockSpec((1, D_MODEL), lambda bi, si: (0, 0))],
        out_specs=tok(),
        out_shape=jax.ShapeDtypeStruct((b, s, D_MODEL), F32),
        compiler_params=_cparams(("arbitrary", "arbitrary")),
        name="final_norm",
    )(h, f, gate_f, lng, lnb)


def _rope_lane_tables(seq):
    rows = seq // GRID_W
    row = jnp.repeat(jnp.arange(rows, dtype=F32), GRID_W)
    col = jnp.tile(jnp.arange(GRID_W, dtype=F32), rows)
    axis_dim = HEAD_DIM // 2
    inv_freq = ROPE_THETA ** (-jnp.arange(0, axis_dim, 2, dtype=F32) / axis_dim)
    ang = jnp.concatenate([row[:, None] * inv_freq, col[:, None] * inv_freq], axis=-1)
    cos = jnp.repeat(jnp.cos(ang), 2, axis=-1)
    sin = jnp.repeat(jnp.sin(ang), 2, axis=-1)
    sign = jnp.tile(jnp.array([-1.0, 1.0], F32), HEAD_DIM // 2)
    return jnp.tile(cos, (1, N_Q_HEADS)), jnp.tile(sin * sign, (1, N_Q_HEADS))


def kernel(x, c, ctx, c_ctx, w_mod, b_mod, ln_mix_g, ln_mix_b, ln_ffn_g, ln_ffn_b, w_mix_in, q_norm_g, k_norm_g, w_pool_grp, pool_scale, w_mix_out, w_conv_in, conv_w, w_conv_out, w_router, w_exp_gate, w_exp_up, w_exp_down):
    b, s, d = x.shape
    lc = ctx.shape[1]
    ts = 512

    cond_rows = jnp.zeros((SUBLANES, d), F32).at[:b].set(c).at[b].set(c_ctx)
    mods = _modulation(cond_rows, w_mod, b_mod).reshape(DEPTH, SUBLANES, 6, d)
    m_lat = lambda layer, k: mods[layer, :b, k][:, None, :]
    m_ctx = lambda layer, k: mods[layer, b:b + 1, k][:, None, :]
    vec = lambda t: t.reshape(1, -1)

    w_in_bf = w_mix_in[0].astype(BF16)
    grp = lax.broadcasted_iota(I32, (ATTN_WIDTH, ATTN_WIDTH), 0) // HEAD_DIM
    bd = jnp.where(grp == grp.T, 1.0 / HEAD_DIM, 0.0).astype(BF16)
    qg = jnp.tile(q_norm_g[0], N_Q_HEADS).reshape(1, -1)
    kg = jnp.tile(k_norm_g[0], N_KV_HEADS).reshape(1, -1)
    cos_t, sin_t = _rope_lane_tables(s)
    q, k, v, p = _l0_in_proj(x, m_lat(0, 0), m_lat(0, 1), w_in_bf, bd, qg, kg, cos_t, sin_t, ts)
    ones = jnp.ones((lc, ATTN_WIDTH), F32)
    _, kc, vc, _ = _l0_in_proj(ctx, m_ctx(0, 0), m_ctx(0, 1), w_in_bf, bd, qg, kg, ones, 0.0 * ones, lc)
    k_all = jnp.concatenate([kc, k], axis=1)
    v_t = jnp.swapaxes(jnp.concatenate([vc, v], axis=1), 1, 2).reshape(b, N_KV_HEADS, HEAD_DIM, lc + s)
    v_t = jnp.concatenate([v_t, jnp.ones((b, N_KV_HEADS, _VT_ROWS - HEAD_DIM, lc + s), BF16)], axis=2)
    score_bound = (1.02 * HEAD_DIM ** 0.5 * LOG2_E) * jnp.max(jnp.abs(q_norm_g[0])) * jnp.max(jnp.abs(k_norm_g[0]))
    a = _attention(q, k_all, v_t, score_bound, 256)
    h, u, aff = _l0_out_proj(a, p, w_pool_grp[0].astype(BF16), vec(pool_scale[0]), w_mix_out[0].astype(BF16),
                             x, m_lat(0, 2), vec(ln_mix_g[0]), vec(ln_mix_b[0]), m_lat(0, 3), m_lat(0, 4),
                             w_router[0].T, ts)
    f = _moe(u, aff, 0, w_exp_gate, w_exp_up, w_exp_down)

    h, bg, cx = _l1_in_proj(h, f, m_lat(0, 5), vec(ln_ffn_g[0]), vec(ln_ffn_b[0]), m_lat(1, 0), m_lat(1, 1),
                            w_conv_in[0].astype(BF16), ts)
    h, u, aff = _l1_out_proj(cx, bg, conv_w[0], w_conv_out[0].astype(BF16),
                             h, m_lat(1, 2), vec(ln_mix_g[1]), vec(ln_mix_b[1]), m_lat(1, 3), m_lat(1, 4),
                             w_router[1].T, ts)
    f = _moe(u, aff, 1, w_exp_gate, w_exp_up, w_exp_down)
    return _final_norm(h, f, m_lat(1, 5), vec(ln_ffn_g[1]), vec(ln_ffn_b[1]), ts)
```

```python
import functools

import jax
import jax.numpy as jnp
from jax import lax
from jax.experimental import pallas as pl
from jax.experimental.pallas import tpu as pltpu

F32, BF16, I32 = jnp.float32, jnp.bfloat16, jnp.int32
HIGHEST = lax.Precision.HIGHEST

D_MODEL = 1024
DEPTH = 2
GRID_W = 64
N_Q_HEADS = 8
N_KV_HEADS = 2
HEAD_DIM = 64
ATTN_WIDTH = N_Q_HEADS * HEAD_DIM
KV_WIDTH = N_KV_HEADS * HEAD_DIM
ROPE_THETA = 10000.0
POOL_WINDOWS = (2, 4, 8, 16)
POOL_GROUP_DIM = 128
POOL_WIDTH = len(POOL_WINDOWS) * POOL_GROUP_DIM
MIX_IN_WIDTH = ATTN_WIDTH + 2 * KV_WIDTH + POOL_WIDTH
N_EXPERTS = 16
EXPERT_FF = 1024
CAPACITY_FACTOR = 2
NORM_EPS = 1e-6
DEEPNORM_ALPHA = (2 * DEPTH) ** 0.25
LOG2_E = 1.4426950408889634

LANES = 128
SUBLANES = 8
HALO = SUBLANES
VMEM_LIMIT = 56 * 1024 * 1024

_NT = (((1,), (1,)), ((), ()))


def _cparams(sem):
    return pltpu.CompilerParams(dimension_semantics=sem, vmem_limit_bytes=VMEM_LIMIT)


TOKEN_TILE_ROWS = D_MODEL // LANES


def _store_token_major(ref, rows):
    n = rows.shape[0]
    for j in range(TOKEN_TILE_ROWS):
        ref[pl.ds(j, n, stride=TOKEN_TILE_ROWS), :] = rows[:, j * LANES:(j + 1) * LANES]


def _load_token_major(ref, n):
    return jnp.concatenate(
        [ref[pl.ds(j, n, stride=TOKEN_TILE_ROWS), :] for j in range(TOKEN_TILE_ROWS)], axis=1)


def _layer_norm(z, g, b):
    mu = jnp.mean(z, axis=-1, keepdims=True)
    zc = z - mu
    var = jnp.mean(zc * zc, axis=-1, keepdims=True)
    return zc * lax.rsqrt(var + NORM_EPS) * g + b


def _mod_kernel(c_ref, w_ref, b_ref, o_ref):
    c = c_ref[...]
    cond = c * jax.nn.sigmoid(c)
    o_ref[...] = jnp.dot(cond, w_ref[...], precision=HIGHEST, preferred_element_type=F32) + b_ref[...]


def _modulation(cond_rows, w_mod, b_mod):
    n_out = w_mod.shape[-1]
    tn = D_MODEL
    return pl.pallas_call(
        _mod_kernel,
        grid=(DEPTH, n_out // tn),
        in_specs=[
            pl.BlockSpec((SUBLANES, D_MODEL), lambda l, j: (0, 0)),
            pl.BlockSpec((None, D_MODEL, tn), lambda l, j: (l, 0, j)),
            pl.BlockSpec((None, 1, tn), lambda l, j: (l, 0, j)),
        ],
        out_specs=pl.BlockSpec((None, SUBLANES, tn), lambda l, j: (l, 0, j)),
        out_shape=jax.ShapeDtypeStruct((DEPTH, SUBLANES, n_out), F32),
        compiler_params=_cparams(("arbitrary", "arbitrary")),
        name="modulation",
    )(cond_rows, w_mod, b_mod.reshape(DEPTH, 1, n_out))


def _group_mean_sq(t, bd):
    t2 = t * t
    hi = t2.astype(BF16)
    lo = (t2 - hi.astype(F32)).astype(BF16)
    return (jnp.dot(hi, bd, preferred_element_type=F32) + jnp.dot(lo, bd, preferred_element_type=F32))


def _rope_chunk(y, c, s_signed):
    n = y.shape[-1]
    lane = lax.broadcasted_iota(I32, y.shape, 1)
    nxt = pltpu.roll(y, n - 1, axis=1)
    prv = pltpu.roll(y, 1, axis=1)
    partner = jnp.where((lane & 1) == 0, nxt, prv)
    return y * c + partner * s_signed


def _l0_in_kernel(x_ref, sh_ref, sc_ref, w_ref, bd_ref, qg_ref, kg_ref, cos_ref, sin_ref,
                  q_ref, k_ref, v_ref, p_ref):
    u = x_ref[...] * (1.0 + sc_ref[...]) + sh_ref[...]
    h = jnp.dot(u.astype(BF16), w_ref[...], preferred_element_type=F32)
    q = h[:, :ATTN_WIDTH]
    k = h[:, ATTN_WIDTH:ATTN_WIDTH + KV_WIDTH]
    bd = bd_ref[...]
    qn = q * lax.rsqrt(_group_mean_sq(q, bd) + NORM_EPS) * qg_ref[...]
    kn = k * lax.rsqrt(_group_mean_sq(k, bd[:KV_WIDTH, :KV_WIDTH]) + NORM_EPS) * kg_ref[...]
    scale = HEAD_DIM ** -0.5 * LOG2_E
    heads_per_kv = N_Q_HEADS // N_KV_HEADS
    lane = lax.broadcasted_iota(I32, (x_ref.shape[0], LANES), 1)
    for c in range(ATTN_WIDTH // LANES):
        sl = slice(c * LANES, (c + 1) * LANES)
        pair = _rope_chunk(qn[:, sl], cos_ref[:, sl], sin_ref[:, sl]) * scale
        swapped = pltpu.roll(pair, HEAD_DIM, axis=1)
        g = (2 * c) // heads_per_kv
        keep = (lane < HEAD_DIM) if g == 0 else (lane >= HEAD_DIM)
        first, second = (pair, swapped) if g == 0 else (swapped, pair)
        q_ref[:, (2 * c) * LANES:(2 * c + 1) * LANES] = jnp.where(keep, first, 0.0).astype(BF16)
        q_ref[:, (2 * c + 1) * LANES:(2 * c + 2) * LANES] = jnp.where(keep, second, 0.0).astype(BF16)
    k_ref[...] = _rope_chunk(kn, cos_ref[:, :KV_WIDTH], sin_ref[:, :KV_WIDTH]).astype(BF16)
    v_ref[...] = h[:, ATTN_WIDTH + KV_WIDTH:ATTN_WIDTH + 2 * KV_WIDTH].astype(BF16)
    p_ref[...] = h[:, ATTN_WIDTH + 2 * KV_WIDTH:]


def _l0_in_proj(x, shift, scale, w_in_bf, bd, qg, kg, cos_t, sin_t, ts):
    b, l, _ = x.shape
    per_batch = shift.shape[0] == b
    mod_map = (lambda bi, si: (bi, 0, 0)) if per_batch else (lambda bi, si: (0, 0, 0))
    full = lambda bi, si: (0, 0)
    return pl.pallas_call(
        _l0_in_kernel,
        grid=(b, l // ts),
        in_specs=[
            pl.BlockSpec((None, ts, D_MODEL), lambda bi, si: (bi, si, 0)),
            pl.BlockSpec((None, 1, D_MODEL), mod_map),
            pl.BlockSpec((None, 1, D_MODEL), mod_map),
            pl.BlockSpec((D_MODEL, MIX_IN_WIDTH), full),
            pl.BlockSpec((ATTN_WIDTH, ATTN_WIDTH), full),
            pl.BlockSpec((1, ATTN_WIDTH), full),
            pl.BlockSpec((1, KV_WIDTH), full),
            pl.BlockSpec((ts, ATTN_WIDTH), lambda bi, si: (si, 0)),
            pl.BlockSpec((ts, ATTN_WIDTH), lambda bi, si: (si, 0)),
        ],
        out_specs=[
            pl.BlockSpec((None, ts, N_Q_HEADS * LANES), lambda bi, si: (bi, si, 0)),
            pl.BlockSpec((None, ts, KV_WIDTH), lambda bi, si: (bi, si, 0)),
            pl.BlockSpec((None, ts, KV_WIDTH), lambda bi, si: (bi, si, 0)),
            pl.BlockSpec((None, ts, POOL_WIDTH), lambda bi, si: (bi, si, 0)),
        ],
        out_shape=[
            jax.ShapeDtypeStruct((b, l, N_Q_HEADS * LANES), BF16),
            jax.ShapeDtypeStruct((b, l, KV_WIDTH), BF16),
            jax.ShapeDtypeStruct((b, l, KV_WIDTH), BF16),
            jax.ShapeDtypeStruct((b, l, POOL_WIDTH), F32),
        ],
        compiler_params=_cparams(("arbitrary", "arbitrary")),
        name="l0_in_proj",
    )(x, shift, scale, w_in_bf, bd, qg, kg, cos_t, sin_t)


_VT_ROWS = HEAD_DIM + 16


_KV_CHUNK = 2176
_MAX_ROWS = 64


def _attn_kernel(q_ref, k_ref, vt_ref, o_ref, ot_ref, *s_refs):
    heads_per_kv = N_Q_HEADS // N_KV_HEADS
    lk, tq = k_ref.shape[0], q_ref.shape[0]
    chunks = [(c0, min(_KV_CHUNK, lk - c0)) for c0 in range(0, lk, _KV_CHUNK)]

    def scores(h, c0, n, m):
        qh = q_ref[:, h * LANES:(h + 1) * LANES]
        s_c = lax.dot_general(k_ref[c0:c0 + n, :], qh, _NT, preferred_element_type=F32)
        s_refs[h % 2][c0:c0 + n, :] = s_c
        mc = jnp.max(s_c.reshape(n // _MAX_ROWS, _MAX_ROWS, tq), axis=0)
        return mc if m is None else jnp.maximum(m, mc)

    def weighted(h, c0, n, m, o_t):
        p_c = jnp.exp2(s_refs[h % 2][c0:c0 + n, :] - m).astype(BF16)
        o_c = jnp.dot(vt_ref[h // heads_per_kv, :, c0:c0 + n], p_c, preferred_element_type=F32)
        return o_c if o_t is None else o_t + o_c

    m_next = None
    for c0, n in chunks:
        m_next = scores(0, c0, n, m_next)
    for h in range(N_Q_HEADS):
        g = h // heads_per_kv
        m = jnp.max(m_next, axis=0, keepdims=True)
        m_next = None
        o_t = None
        for c0, n in chunks:
            o_t = weighted(h, c0, n, m, o_t)
            if h + 1 < N_Q_HEADS:
                m_next = scores(h + 1, c0, n, m_next)
        ot_ref[h * HEAD_DIM:(h + 1) * HEAD_DIM, :] = (
            o_t[:HEAD_DIM, :] / o_t[HEAD_DIM:HEAD_DIM + 1, :])
    o_ref[...] = ot_ref[...].T.astype(BF16)


_BOUNDED_KV_CHUNK = 1088


def _attn_bounded_kernel(bound_ref, q_ref, k_ref, vt_ref, o_ref, ot_ref):
    heads_per_kv = N_Q_HEADS // N_KV_HEADS
    lk = k_ref.shape[0]
    shift = bound_ref[0]
    for h in range(N_Q_HEADS):
        g = h // heads_per_kv
        qh = q_ref[:, h * LANES:(h + 1) * LANES]
        o_t = None
        for c0 in range(0, lk, _BOUNDED_KV_CHUNK):
            n = min(_BOUNDED_KV_CHUNK, lk - c0)
            s_c = lax.dot_general(k_ref[c0:c0 + n, :], qh, _NT, preferred_element_type=F32)
            p_c = jnp.exp2(s_c - shift).astype(BF16)
            o_c = jnp.dot(vt_ref[g, :, c0:c0 + n], p_c, preferred_element_type=F32)
            o_t = o_c if o_t is None else o_t + o_c
        ot_ref[h * HEAD_DIM:(h + 1) * HEAD_DIM, :] = (
            o_t[:HEAD_DIM, :] / o_t[HEAD_DIM:HEAD_DIM + 1, :])
    o_ref[...] = ot_ref[...].T.astype(BF16)


_MAX_SAFE_SCORE_BOUND = 40.0


def _attention(q, k_all, vt, score_bound, tq):
    b, s, qw = q.shape
    lk = k_all.shape[1]
    specs = dict(
        grid=(b, s // tq),
        out_specs=pl.BlockSpec((None, tq, ATTN_WIDTH), lambda bi, qi: (bi, qi, 0)),
        out_shape=jax.ShapeDtypeStruct((b, s, ATTN_WIDTH), BF16),
        compiler_params=_cparams(("arbitrary", "arbitrary")),
    )
    in_specs = [
        pl.BlockSpec((None, tq, qw), lambda bi, qi: (bi, qi, 0)),
        pl.BlockSpec((None, lk, KV_WIDTH), lambda bi, qi: (bi, 0, 0)),
        pl.BlockSpec((None, N_KV_HEADS, _VT_ROWS, lk), lambda bi, qi: (bi, 0, 0, 0)),
    ]

    def bounded(_):
        return pl.pallas_call(
            _attn_bounded_kernel,
            in_specs=[pl.BlockSpec(memory_space=pltpu.SMEM)] + in_specs,
            scratch_shapes=[pltpu.VMEM((ATTN_WIDTH, tq), F32)],
            name="attention_bounded", **specs,
        )(score_bound.reshape(1), q, k_all, vt)

    def exact(_):
        return pl.pallas_call(
            _attn_kernel,
            in_specs=in_specs,
            scratch_shapes=[pltpu.VMEM((ATTN_WIDTH, tq), F32), pltpu.VMEM((lk, tq), F32),
                            pltpu.VMEM((lk, tq), F32)],
            name="attention", **specs,
        )(q, k_all, vt)

    return lax.cond(score_bound < _MAX_SAFE_SCORE_BOUND, bounded, exact, None)


def _split_bf16(t):
    hi = t.astype(BF16)
    return hi, (t - hi.astype(F32)).astype(BF16)


def _router_weights(w_router_layer):
    hi, lo = _split_bf16(w_router_layer.T)
    return jnp.concatenate([hi, lo], axis=0)


def _post_tail(resid, y, gate, lng, lnb, shf, scf, wr2):
    h = _layer_norm(DEEPNORM_ALPHA * resid + gate * y, lng, lnb)
    u = h * (1.0 + scf) + shf
    u_hi, u_lo = _split_bf16(u)
    first = lax.dot_general(wr2, u_hi, _NT, preferred_element_type=F32)
    second = lax.dot_general(wr2[:N_EXPERTS], u_lo, _NT, preferred_element_type=F32)
    logits = first[:N_EXPERTS] + first[N_EXPERTS:] + second
    ex = jnp.exp(logits - jnp.max(logits, axis=0, keepdims=True))
    aff = ex / jnp.sum(ex, axis=0, keepdims=True)
    return h, u, aff


def _fill_halo(ext_ref, prev_ref, main_ref, next_ref, ts):
    st = pl.program_id(1)
    last = pl.num_programs(1) - 1
    ext_ref[0:HALO, :] = jnp.where(st > 0, prev_ref[...], 0.0)
    ext_ref[HALO:HALO + ts, :] = main_ref[...]
    ext_ref[HALO + ts:HALO + ts + HALO, :] = jnp.where(st < last, next_ref[...], 0.0)


def _halo_specs(ts, width, seq):
    r = ts // HALO
    nblk = seq // HALO
    return [
        pl.BlockSpec((None, ts, width), lambda bi, si: (bi, si, 0)),
        pl.BlockSpec((None, HALO, width), lambda bi, si: (bi, jnp.maximum(si * r - 1, 0), 0)),
        pl.BlockSpec((None, HALO, width), lambda bi, si: (bi, jnp.minimum((si + 1) * r, nblk - 1), 0)),
    ]


def _tail_specs(ts):
    row = lambda bi, si: (bi, 0, 0)
    full = lambda bi, si: (0, 0)
    in_specs = [
        pl.BlockSpec((None, ts, D_MODEL), lambda bi, si: (bi, si, 0)),
        pl.BlockSpec((None, 1, D_MODEL), row),
        pl.BlockSpec((1, D_MODEL), full),
        pl.BlockSpec((1, D_MODEL), full),
        pl.BlockSpec((None, 1, D_MODEL), row),
        pl.BlockSpec((None, 1, D_MODEL), row),
        pl.BlockSpec((2 * N_EXPERTS, D_MODEL), full),
    ]
    out_specs = [
        pl.BlockSpec((None, ts, D_MODEL), lambda bi, si: (bi, si, 0)),
        pl.BlockSpec((None, ts * TOKEN_TILE_ROWS, LANES), lambda bi, si: (bi, si, 0)),
        pl.BlockSpec((None, N_EXPERTS, ts), lambda bi, si: (bi, 0, si)),
    ]
    return in_specs, out_specs


def _tail_out_shape(b, s):
    return [jax.ShapeDtypeStruct((b, s, D_MODEL), F32),
            jax.ShapeDtypeStruct((b, s * TOKEN_TILE_ROWS, LANES), F32),
            jax.ShapeDtypeStruct((b, N_EXPERTS, s), F32)]


def _l0_out_kernel(seq, a_ref, p_ref, pprev_ref, pnext_ref, wpool_ref, pscale_ref, wout_ref,
                   x_ref, gate_ref, lng_ref, lnb_ref, shf_ref, scf_ref, wr_ref,
                   h_ref, u_ref, aff_ref, pext_ref):
    ts = p_ref.shape[0]
    _fill_halo(pext_ref, pprev_ref, p_ref, pnext_ref, ts)
    t = pl.program_id(1) * ts + lax.broadcasted_iota(I32, (ts, 1), 0)
    outs = []
    for g, w in enumerate(POOL_WINDOWS):
        sl = slice(g * POOL_GROUP_DIM, (g + 1) * POOL_GROUP_DIM)
        acc = pext_ref[HALO - w // 2:HALO - w // 2 + ts, sl]
        for j in range(-(w // 2) + 1, w // 2):
            acc = acc + pext_ref[HALO + j:HALO + j + ts, sl]
        cnt = jnp.minimum(t + w // 2, seq) - jnp.maximum(t - w // 2, 0)
        pooled = acc / cnt.astype(F32) - p_ref[:, sl]
        og = jnp.dot(pooled.astype(BF16), wpool_ref[g], preferred_element_type=F32)
        outs.append((og * pscale_ref[:, sl]).astype(BF16))
    pool = jnp.concatenate(outs, axis=-1)
    y = (jnp.dot(a_ref[...], wout_ref[:ATTN_WIDTH, :], preferred_element_type=F32)
         + jnp.dot(pool, wout_ref[ATTN_WIDTH:, :], preferred_element_type=F32))
    h, u, aff = _post_tail(x_ref[...], y, gate_ref[...], lng_ref[...], lnb_ref[...],
                           shf_ref[...], scf_ref[...], wr_ref[...])
    h_ref[...] = h
    _store_token_major(u_ref, u)
    aff_ref[...] = aff


def _l0_out_proj(a, p, wpool_bf, pscale, wout_bf, x, gate, lng, lnb, shf, scf, wr_t, ts):
    b, s, _ = x.shape
    full2 = lambda bi, si: (0, 0)
    tail_in, tail_out = _tail_specs(ts)
    in_specs = (
        [pl.BlockSpec((None, ts, ATTN_WIDTH), lambda bi, si: (bi, si, 0))]
        + _halo_specs(ts, POOL_WIDTH, s)
        + [pl.BlockSpec((len(POOL_WINDOWS), POOL_GROUP_DIM, POOL_GROUP_DIM), lambda bi, si: (0, 0, 0)),
           pl.BlockSpec((1, POOL_WIDTH), full2),
           pl.BlockSpec((ATTN_WIDTH + POOL_WIDTH, D_MODEL), full2)]
        + tail_in)
    return pl.pallas_call(
        functools.partial(_l0_out_kernel, s),
        grid=(b, s // ts),
        in_specs=in_specs,
        out_specs=tail_out,
        out_shape=_tail_out_shape(b, s),
        scratch_shapes=[pltpu.VMEM((ts + 2 * HALO, POOL_WIDTH), F32)],
        compiler_params=_cparams(("arbitrary", "arbitrary")),
        name="l0_out_proj",
    )(a, p, p, p, wpool_bf, pscale, wout_bf, x, gate, lng, lnb, shf, scf, wr_t)


def _exclusive_cumsum_lanes(mask, tri):
    n = mask.shape[1]
    w = tri.shape[0]
    carry = jnp.zeros((mask.shape[0], 1), F32)
    parts = []
    for c in range(n // w):
        blk = mask[:, c * w:(c + 1) * w]
        inc = jnp.dot(blk.astype(BF16), tri, preferred_element_type=F32)
        parts.append(inc - blk + carry)
        carry = carry + inc[:, w - 1:w]
    return jnp.concatenate(parts, axis=1)


_TOKEN_SPLIT = 64
_SLOT_LO = 32
_PACKED_VALUES = 5


def _select_kernel(cap, aff_ref, out_ref):
    aff = aff_ref[...]
    e = aff.shape[0]
    as_f32 = lambda word: pltpu.bitcast(word, F32)

    def refine(i, thr):
        cand = thr | jnp.left_shift(jnp.int32(1), 30 - i)
        cnt = jnp.sum(jnp.where(aff >= as_f32(cand), 1.0, 0.0), axis=1, keepdims=True)
        return jnp.where(cnt >= cap, cand, thr)

    thr = lax.fori_loop(0, 31, refine, jnp.zeros((e, 1), I32))
    w = 2 * LANES
    r = lax.broadcasted_iota(I32, (w, w), 0)
    c = lax.broadcasted_iota(I32, (w, w), 1)
    tri = jnp.where(r <= c, 1.0, 0.0).astype(BF16)
    gt = jnp.where(aff >= as_f32(thr + 1), 1.0, 0.0)
    eq = jnp.where(aff >= as_f32(thr), 1.0, 0.0) - gt
    need = cap - jnp.sum(gt, axis=1, keepdims=True)
    sel = gt + eq * jnp.where(_exclusive_cumsum_lanes(eq, tri) < need, 1.0, 0.0)
    pos = _exclusive_cumsum_lanes(sel, tri)
    dest = jnp.where(sel > 0.5, pos, float(cap)).astype(I32)

    n = aff.shape[1]
    n_hi = cap // _SLOT_LO
    d_hi = dest // _SLOT_LO
    d_lo = dest % _SLOT_LO
    t = lax.broadcasted_iota(I32, (1, n), 1)
    t_hi = (t // _TOKEN_SPLIT).astype(F32)
    t_lo = (t % _TOKEN_SPLIT).astype(F32)
    a1 = aff.astype(BF16).astype(F32)
    r1 = aff - a1
    a2 = r1.astype(BF16).astype(F32)
    a3 = r1 - a2
    row_hi = lax.broadcasted_iota(I32, (n_hi, n), 0)
    row_lo = lax.broadcasted_iota(I32, (_SLOT_LO, n), 0)
    for ei in range(e):
        rs = slice(ei, ei + 1)
        h = jnp.where(d_hi[rs] == row_hi, 1.0, 0.0)
        l = jnp.where(d_lo[rs] == row_lo, 1.0, 0.0).astype(BF16)
        vals = jnp.concatenate([h * t_hi, h * t_lo, h * a1[rs], h * a2[rs], h * a3[rs]], axis=0)
        out_ref[ei] = lax.dot_general(vals.astype(BF16), l, _NT, preferred_element_type=F32)


def _select(aff, cap):
    b, e, n = aff.shape
    rows = _PACKED_VALUES * (cap // _SLOT_LO)
    return pl.pallas_call(
        functools.partial(_select_kernel, cap),
        grid=(b,),
        in_specs=[pl.BlockSpec((None, e, n), lambda bi: (bi, 0, 0))],
        out_specs=pl.BlockSpec((None, e, rows, _SLOT_LO), lambda bi: (bi, 0, 0, 0)),
        out_shape=jax.ShapeDtypeStruct((b, e, rows, _SLOT_LO), F32),
        compiler_params=_cparams(("arbitrary",)),
        name="expert_select",
    )(aff)


def _token_tile(t):
    return pl.ds(pl.multiple_of(t * TOKEN_TILE_ROWS, TOKEN_TILE_ROWS), TOKEN_TILE_ROWS)


def _expert_kernel(cap, seq, idx_ref, u_hbm, wg_ref, wu_ref, wd_ref, gate_ref, y_ref,
                   xe_ref, wg_bf, wu_bf, wd_bf, sem):
    b = pl.program_id(1)
    nb = pl.num_programs(1)
    step = pl.program_id(0) * nb + b
    nsteps = pl.num_programs(0) * nb
    slot = step % 2

    def start_gather(k, into, rolled):
        ek = k // nb
        bk = k % nb
        base = (bk * N_EXPERTS + ek) * cap
        tok0 = bk * seq

        def start(i):
            tok = tok0 + idx_ref[base + i]
            pltpu.make_async_copy(u_hbm.at[_token_tile(tok), :], xe_ref.at[into, _token_tile(i), :],
                                  sem.at[into]).start()

        if rolled:
            def issue(ci, carry):
                for j in range(SUBLANES):
                    start(ci * SUBLANES + j)
                return carry

            lax.fori_loop(0, cap // SUBLANES, issue, 0)
        else:
            for i in range(cap):
                start(i)

    def wait_gather(into):
        pltpu.make_async_copy(u_hbm.at[pl.ds(0, cap * TOKEN_TILE_ROWS), :], xe_ref.at[into], sem.at[into]).wait()

    @pl.when(step == 0)
    def _():
        start_gather(step, slot, rolled=True)

    @pl.when(b == 0)
    def _():
        wg_bf[...] = wg_ref[...].astype(BF16)
        wu_bf[...] = wu_ref[...].astype(BF16)
        wd_bf[...] = wd_ref[...].astype(BF16)

    wait_gather(slot)
    x = _load_token_major(xe_ref.at[slot], cap).astype(BF16)
    start_gather(jnp.minimum(step + 1, nsteps - 1), 1 - slot, rolled=False)
    hg = jnp.dot(x, wg_bf[...], preferred_element_type=F32)
    hu = jnp.dot(x, wu_bf[...], preferred_element_type=F32)
    act = (hg * jax.nn.sigmoid(hg) * hu).astype(BF16)
    _store_token_major(y_ref, jnp.dot(act, wd_bf[...], preferred_element_type=F32) * gate_ref[...])

    @pl.when(step == nsteps - 1)
    def _():
        wait_gather(1 - slot)


def _expert_ffn(idx_flat, u_rows, layer, w_g, w_u, w_d, gate_sel, batch, seq, cap):
    wspec = lambda: pl.BlockSpec((None, None, D_MODEL, EXPERT_FF), lambda e, b, idx: (layer, e, 0, 0))
    grid_spec = pltpu.PrefetchScalarGridSpec(
        num_scalar_prefetch=1,
        grid=(N_EXPERTS, batch),
        in_specs=[
            pl.BlockSpec(memory_space=pl.ANY),
            wspec(), wspec(),
            pl.BlockSpec((None, None, EXPERT_FF, D_MODEL), lambda e, b, idx: (layer, e, 0, 0)),
            pl.BlockSpec((None, None, cap, 1), lambda e, b, idx: (b, e, 0, 0)),
        ],
        out_specs=pl.BlockSpec((None, None, cap * TOKEN_TILE_ROWS, LANES), lambda e, b, idx: (b, e, 0, 0)),
        scratch_shapes=[
            pltpu.VMEM((2, cap * TOKEN_TILE_ROWS, LANES), F32),
            pltpu.VMEM((D_MODEL, EXPERT_FF), BF16),
            pltpu.VMEM((D_MODEL, EXPERT_FF), BF16),
            pltpu.VMEM((EXPERT_FF, D_MODEL), BF16),
            pltpu.SemaphoreType.DMA((2,)),
        ],
    )
    return pl.pallas_call(
        functools.partial(_expert_kernel, cap, seq),
        grid_spec=grid_spec,
        out_shape=jax.ShapeDtypeStruct((batch, N_EXPERTS, cap * TOKEN_TILE_ROWS, LANES), F32),
        compiler_params=_cparams(("arbitrary", "arbitrary")),
        name="expert_ffn",
    )(idx_flat, u_rows, w_g, w_u, w_d, gate_sel)


_COMBINE_UNROLL = 8


def _combine_kernel(cap, idx_ref, y_ref, f_ref):
    b = pl.program_id(0)
    e = pl.program_id(1)

    @pl.when(e == 0)
    def _():
        f_ref[...] = jnp.zeros_like(f_ref)

    base = (b * N_EXPERTS + e) * cap

    def chunk(ci, carry):
        i0 = ci * _COMBINE_UNROLL
        tiles = [_token_tile(idx_ref[base + i0 + j]) for j in range(_COMBINE_UNROLL)]
        sums = [f_ref[tiles[j], :] + y_ref[_token_tile(i0 + j), :] for j in range(_COMBINE_UNROLL)]
        for j in range(_COMBINE_UNROLL):
            f_ref[tiles[j], :] = sums[j]
        return carry

    lax.fori_loop(0, cap // _COMBINE_UNROLL, chunk, 0)


def _combine(idx_flat, y, seq, cap):
    batch = y.shape[0]
    grid_spec = pltpu.PrefetchScalarGridSpec(
        num_scalar_prefetch=1,
        grid=(batch, N_EXPERTS),
        in_specs=[pl.BlockSpec((None, None, cap * TOKEN_TILE_ROWS, LANES), lambda b, e, idx: (b, e, 0, 0))],
        out_specs=pl.BlockSpec((None, seq * TOKEN_TILE_ROWS, LANES), lambda b, e, idx: (b, 0, 0)),
    )
    return pl.pallas_call(
        functools.partial(_combine_kernel, cap),
        grid_spec=grid_spec,
        out_shape=jax.ShapeDtypeStruct((batch, seq * TOKEN_TILE_ROWS, LANES), F32),
        compiler_params=_cparams(("arbitrary", "arbitrary")),
        name="expert_combine",
    )(idx_flat, y)


def _moe(u, aff, layer, w_g, w_u, w_d):
    b, _, n = aff.shape
    cap = CAPACITY_FACTOR * n // N_EXPERTS
    packed = _select(aff, cap).reshape(b, N_EXPERTS, _PACKED_VALUES, cap)
    idx_flat = (packed[:, :, 0] * _TOKEN_SPLIT + packed[:, :, 1]).astype(I32).reshape(-1)
    gate_sel = (packed[:, :, 2] + packed[:, :, 3] + packed[:, :, 4])[..., None]
    y = _expert_ffn(idx_flat, u.reshape(b * n * TOKEN_TILE_ROWS, LANES), layer, w_g, w_u, w_d, gate_sel, b, n, cap)
    return _combine(idx_flat, y, n, cap)


def _l1_in_kernel(h_ref, f_ref, gf_ref, lng_ref, lnb_ref, sh_ref, sc_ref, w_ref, h2_ref, bg_ref, cx_ref):
    f = _load_token_major(f_ref, h_ref.shape[0])
    h2 = _layer_norm(DEEPNORM_ALPHA * h_ref[...] + gf_ref[...] * f, lng_ref[...], lnb_ref[...])
    h2_ref[...] = h2
    u = h2 * (1.0 + sc_ref[...]) + sh_ref[...]
    z = jnp.dot(u.astype(BF16), w_ref[...], preferred_element_type=F32)
    bg_ref[...] = z[:, :D_MODEL]
    cx_ref[...] = z[:, D_MODEL:2 * D_MODEL] * z[:, 2 * D_MODEL:]


def _l1_in_proj(h, f, gate_f, lng, lnb, shift, scale, w_bf, ts):
    b, s, _ = h.shape
    tok = lambda: pl.BlockSpec((None, ts, D_MODEL), lambda bi, si: (bi, si, 0))
    row = lambda: pl.BlockSpec((None, 1, D_MODEL), lambda bi, si: (bi, 0, 0))
    vec = lambda: pl.BlockSpec((1, D_MODEL), lambda bi, si: (0, 0))
    return pl.pallas_call(
        _l1_in_kernel,
        grid=(b, s // ts),
        in_specs=[tok(), pl.BlockSpec((None, ts * TOKEN_TILE_ROWS, LANES), lambda bi, si: (bi, si, 0)),
                  row(), vec(), vec(), row(), row(),
                  pl.BlockSpec((D_MODEL, 3 * D_MODEL), lambda bi, si: (0, 0))],
        out_specs=[tok(), tok(), tok()],
        out_shape=[jax.ShapeDtypeStruct((b, s, D_MODEL), F32)] * 3,
        compiler_params=_cparams(("arbitrary", "arbitrary")),
        name="l1_in_proj",
    )(h, f, gate_f, lng, lnb, shift, scale, w_bf)


def _l1_out_kernel(cx_ref, cprev_ref, cnext_ref, bg_ref, cw_ref, wout_ref,
                   x_ref, gate_ref, lng_ref, lnb_ref, shf_ref, scf_ref, wr_ref,
                   h_ref, u_ref, aff_ref, cext_ref):
    ts = cx_ref.shape[0]
    _fill_halo(cext_ref, cprev_ref, cx_ref, cnext_ref, ts)
    z = (cext_ref[HALO - 1:HALO - 1 + ts, :] * cw_ref[0:1, :]
         + cx_ref[...] * cw_ref[1:2, :]
         + cext_ref[HALO + 1:HALO + 1 + ts, :] * cw_ref[2:3, :])
    y = jnp.dot((bg_ref[...] * z).astype(BF16), wout_ref[...], preferred_element_type=F32)
    h, u, aff = _post_tail(x_ref[...], y, gate_ref[...], lng_ref[...], lnb_ref[...],
                           shf_ref[...], scf_ref[...], wr_ref[...])
    h_ref[...] = h
    _store_token_major(u_ref, u)
    aff_ref[...] = aff


def _l1_out_proj(cx, bg, conv_w, wout_bf, x, gate, lng, lnb, shf, scf, wr_t, ts):
    b, s, _ = x.shape
    tail_in, tail_out = _tail_specs(ts)
    in_specs = (
        _halo_specs(ts, D_MODEL, s)
        + [pl.BlockSpec((None, ts, D_MODEL), lambda bi, si: (bi, si, 0)),
           pl.BlockSpec(conv_w.shape, lambda bi, si: (0, 0)),
           pl.BlockSpec((D_MODEL, D_MODEL), lambda bi, si: (0, 0))]
        + tail_in)
    return pl.pallas_call(
        _l1_out_kernel,
        grid=(b, s // ts),
        in_specs=in_specs,
        out_specs=tail_out,
        out_shape=_tail_out_shape(b, s),
        scratch_shapes=[pltpu.VMEM((ts + 2 * HALO, D_MODEL), F32)],
        compiler_params=_cparams(("arbitrary", "arbitrary")),
        name="l1_out_proj",
    )(cx, cx, cx, bg, conv_w, wout_bf, x, gate, lng, lnb, shf, scf, wr_t)


def _final_kernel(h_ref, f_ref, gf_ref, lng_ref, lnb_ref, o_ref):
    f = _load_token_major(f_ref, h_ref.shape[0])
    o_ref[...] = _layer_norm(DEEPNORM_ALPHA * h_ref[...] + gf_ref[...] * f, lng_ref[...], lnb_ref[...])


def _final_norm(h, f, gate_f, lng, lnb, ts):
    b, s, _ = h.shape
    tok = lambda: pl.BlockSpec((None, ts, D_MODEL), lambda bi, si: (bi, si, 0))
    return pl.pallas_call(
        _final_kernel,
        grid=(b, s // ts),
        in_specs=[tok(), pl.BlockSpec((None, ts * TOKEN_TILE_ROWS, LANES), lambda bi, si: (bi, si, 0)),
                  pl.BlockSpec((None, 1, D_MODEL), lambda bi, si: (bi, 0, 0)),
                  pl.BlockSpec((1, D_MODEL), lambda bi, si: (0, 0)),
                  pl.BlockSpec((1, D_MODEL), lambda bi, si: (0, 0))],
        out_specs=tok(),
        out_shape=jax.ShapeDtypeStruct((b, s, D_MODEL), F32),
        compiler_params=_cparams(("arbitrary", "arbitrary")),
        name="final_norm",
    )(h, f, gate_f, lng, lnb)


def _rope_lane_tables(seq):
    rows = seq // GRID_W
    row = jnp.repeat(jnp.arange(rows, dtype=F32), GRID_W)
    col = jnp.tile(jnp.arange(GRID_W, dtype=F32), rows)
    axis_dim = HEAD_DIM // 2
    inv_freq = ROPE_THETA ** (-jnp.arange(0, axis_dim, 2, dtype=F32) / axis_dim)
    ang = jnp.concatenate([row[:, None] * inv_freq, col[:, None] * inv_freq], axis=-1)
    cos = jnp.repeat(jnp.cos(ang), 2, axis=-1)
    sin = jnp.repeat(jnp.sin(ang), 2, axis=-1)
    sign = jnp.tile(jnp.array([-1.0, 1.0], F32), HEAD_DIM // 2)
    return jnp.tile(cos, (1, N_Q_HEADS)), jnp.tile(sin * sign, (1, N_Q_HEADS))


def kernel(x, c, ctx, c_ctx, w_mod, b_mod, ln_mix_g, ln_mix_b, ln_ffn_g, ln_ffn_b, w_mix_in, q_norm_g, k_norm_g, w_pool_grp, pool_scale, w_mix_out, w_conv_in, conv_w, w_conv_out, w_router, w_exp_gate, w_exp_up, w_exp_down):
    b, s, d = x.shape
    lc = ctx.shape[1]
    ts = 512

    cond_rows = jnp.zeros((SUBLANES, d), F32).at[:b].set(c).at[b].set(c_ctx)
    mods = _modulation(cond_rows, w_mod, b_mod).reshape(DEPTH, SUBLANES, 6, d)
    m_lat = lambda layer, k: mods[layer, :b, k][:, None, :]
    m_ctx = lambda layer, k: mods[layer, b:b + 1, k][:, None, :]
    vec = lambda t: t.reshape(1, -1)

    w_in_bf = w_mix_in[0].astype(BF16)
    grp = lax.broadcasted_iota(I32, (ATTN_WIDTH, ATTN_WIDTH), 0) // HEAD_DIM
    bd = jnp.where(grp == grp.T, 1.0 / HEAD_DIM, 0.0).astype(BF16)
    qg = jnp.tile(q_norm_g[0], N_Q_HEADS).reshape(1, -1)
    kg = jnp.tile(k_norm_g[0], N_KV_HEADS).reshape(1, -1)
    cos_t, sin_t = _rope_lane_tables(s)
    q, k, v, p = _l0_in_proj(x, m_lat(0, 0), m_lat(0, 1), w_in_bf, bd, qg, kg, cos_t, sin_t, ts)
    ones = jnp.ones((lc, ATTN_WIDTH), F32)
    _, kc, vc, _ = _l0_in_proj(ctx, m_ctx(0, 0), m_ctx(0, 1), w_in_bf, bd, qg, kg, ones, 0.0 * ones, lc)
    k_all = jnp.concatenate([kc, k], axis=1)
    v_t = jnp.swapaxes(jnp.concatenate([vc, v], axis=1), 1, 2).reshape(b, N_KV_HEADS, HEAD_DIM, lc + s)
    v_t = jnp.concatenate([v_t, jnp.ones((b, N_KV_HEADS, _VT_ROWS - HEAD_DIM, lc + s), BF16)], axis=2)
    score_bound = (1.02 * HEAD_DIM ** 0.5 * LOG2_E) * jnp.max(jnp.abs(q_norm_g[0])) * jnp.max(jnp.abs(k_norm_g[0]))
    a = _attention(q, k_all, v_t, score_bound, 256)
    h, u, aff = _l0_out_proj(a, p, w_pool_grp[0].astype(BF16), vec(pool_scale[0]), w_mix_out[0].astype(BF16),
                             x, m_lat(0, 2), vec(ln_mix_g[0]), vec(ln_mix_b[0]), m_lat(0, 3), m_lat(0, 4),
                             _router_weights(w_router[0]), ts)
    f = _moe(u, aff, 0, w_exp_gate, w_exp_up, w_exp_down)

    h, bg, cx = _l1_in_proj(h, f, m_lat(0, 5), vec(ln_ffn_g[0]), vec(ln_ffn_b[0]), m_lat(1, 0), m_lat(1, 1),
                            w_conv_in[0].astype(BF16), ts)
    h, u, aff = _l1_out_proj(cx, bg, conv_w[0], w_conv_out[0].astype(BF16),
                             h, m_lat(1, 2), vec(ln_mix_g[1]), vec(ln_mix_b[1]), m_lat(1, 3), m_lat(1, 4),
                             _router_weights(w_router[1]), ts)
    f = _moe(u, aff, 1, w_exp_gate, w_exp_up, w_exp_down)
    return _final_norm(h, f, m_lat(1, 5), vec(ln_ffn_g[1]), vec(ln_ffn_b[1]), ts)
```

```python
import functools

import jax
import jax.numpy as jnp
from jax import lax
from jax.experimental import pallas as pl
from jax.experimental.pallas import tpu as pltpu

F32, BF16, I32 = jnp.float32, jnp.bfloat16, jnp.int32
HIGHEST = lax.Precision.HIGHEST

D_MODEL = 1024
DEPTH = 2
GRID_W = 64
N_Q_HEADS = 8
N_KV_HEADS = 2
HEAD_DIM = 64
ATTN_WIDTH = N_Q_HEADS * HEAD_DIM
KV_WIDTH = N_KV_HEADS * HEAD_DIM
ROPE_THETA = 10000.0
POOL_WINDOWS = (2, 4, 8, 16)
POOL_GROUP_DIM = 128
POOL_WIDTH = len(POOL_WINDOWS) * POOL_GROUP_DIM
MIX_IN_WIDTH = ATTN_WIDTH + 2 * KV_WIDTH + POOL_WIDTH
N_EXPERTS = 16
EXPERT_FF = 1024
CAPACITY_FACTOR = 2
NORM_EPS = 1e-6
DEEPNORM_ALPHA = (2 * DEPTH) ** 0.25
LOG2_E = 1.4426950408889634

LANES = 128
SUBLANES = 8
HALO = SUBLANES
VMEM_LIMIT = 56 * 1024 * 1024

_NT = (((1,), (1,)), ((), ()))


def _cparams(sem):
    return pltpu.CompilerParams(dimension_semantics=sem, vmem_limit_bytes=VMEM_LIMIT)


TOKEN_TILE_ROWS = D_MODEL // LANES


def _store_token_major(ref, rows):
    n = rows.shape[0]
    for j in range(TOKEN_TILE_ROWS):
        ref[pl.ds(j, n, stride=TOKEN_TILE_ROWS), :] = rows[:, j * LANES:(j + 1) * LANES]


def _load_token_major(ref, n):
    return jnp.concatenate(
        [ref[pl.ds(j, n, stride=TOKEN_TILE_ROWS), :] for j in range(TOKEN_TILE_ROWS)], axis=1)


def _layer_norm(z, g, b):
    mu = jnp.mean(z, axis=-1, keepdims=True)
    zc = z - mu
    var = jnp.mean(zc * zc, axis=-1, keepdims=True)
    return zc * lax.rsqrt(var + NORM_EPS) * g + b


def _mod_kernel(c_ref, w_ref, b_ref, o_ref):
    c = c_ref[...]
    cond = c * jax.nn.sigmoid(c)
    o_ref[...] = jnp.dot(cond, w_ref[...], precision=HIGHEST, preferred_element_type=F32) + b_ref[...]


def _modulation(cond_rows, w_mod, b_mod):
    n_out = w_mod.shape[-1]
    tn = D_MODEL
    return pl.pallas_call(
        _mod_kernel,
        grid=(DEPTH, n_out // tn),
        in_specs=[
            pl.BlockSpec((SUBLANES, D_MODEL), lambda l, j: (0, 0)),
            pl.BlockSpec((None, D_MODEL, tn), lambda l, j: (l, 0, j)),
            pl.BlockSpec((None, 1, tn), lambda l, j: (l, 0, j)),
        ],
        out_specs=pl.BlockSpec((None, SUBLANES, tn), lambda l, j: (l, 0, j)),
        out_shape=jax.ShapeDtypeStruct((DEPTH, SUBLANES, n_out), F32),
        compiler_params=_cparams(("arbitrary", "arbitrary")),
        name="modulation",
    )(cond_rows, w_mod, b_mod.reshape(DEPTH, 1, n_out))


def _group_mean_sq(t, bd):
    t2 = t * t
    hi = t2.astype(BF16)
    lo = (t2 - hi.astype(F32)).astype(BF16)
    return (jnp.dot(hi, bd, preferred_element_type=F32) + jnp.dot(lo, bd, preferred_element_type=F32))


def _rope_chunk(y, c, s_signed):
    n = y.shape[-1]
    lane = lax.broadcasted_iota(I32, y.shape, 1)
    nxt = pltpu.roll(y, n - 1, axis=1)
    prv = pltpu.roll(y, 1, axis=1)
    partner = jnp.where((lane & 1) == 0, nxt, prv)
    return y * c + partner * s_signed


def _l0_in_kernel(x_ref, sh_ref, sc_ref, w_ref, bd_ref, qg_ref, kg_ref, cos_ref, sin_ref,
                  q_ref, k_ref, v_ref, p_ref):
    u = x_ref[...] * (1.0 + sc_ref[...]) + sh_ref[...]
    h = jnp.dot(u.astype(BF16), w_ref[...], preferred_element_type=F32)
    q = h[:, :ATTN_WIDTH]
    k = h[:, ATTN_WIDTH:ATTN_WIDTH + KV_WIDTH]
    bd = bd_ref[...]
    qn = q * lax.rsqrt(_group_mean_sq(q, bd) + NORM_EPS) * qg_ref[...]
    kn = k * lax.rsqrt(_group_mean_sq(k, bd[:KV_WIDTH, :KV_WIDTH]) + NORM_EPS) * kg_ref[...]
    scale = HEAD_DIM ** -0.5 * LOG2_E
    heads_per_kv = N_Q_HEADS // N_KV_HEADS
    lane = lax.broadcasted_iota(I32, (x_ref.shape[0], LANES), 1)
    for c in range(ATTN_WIDTH // LANES):
        sl = slice(c * LANES, (c + 1) * LANES)
        pair = _rope_chunk(qn[:, sl], cos_ref[:, sl], sin_ref[:, sl]) * scale
        swapped = pltpu.roll(pair, HEAD_DIM, axis=1)
        g = (2 * c) // heads_per_kv
        keep = (lane < HEAD_DIM) if g == 0 else (lane >= HEAD_DIM)
        first, second = (pair, swapped) if g == 0 else (swapped, pair)
        q_ref[:, (2 * c) * LANES:(2 * c + 1) * LANES] = jnp.where(keep, first, 0.0).astype(BF16)
        q_ref[:, (2 * c + 1) * LANES:(2 * c + 2) * LANES] = jnp.where(keep, second, 0.0).astype(BF16)
    k_ref[...] = _rope_chunk(kn, cos_ref[:, :KV_WIDTH], sin_ref[:, :KV_WIDTH]).astype(BF16)
    v_ref[...] = h[:, ATTN_WIDTH + KV_WIDTH:ATTN_WIDTH + 2 * KV_WIDTH].astype(BF16)
    p_ref[...] = h[:, ATTN_WIDTH + 2 * KV_WIDTH:]


def _l0_in_proj(x, shift, scale, w_in_bf, bd, qg, kg, cos_t, sin_t, ts):
    b, l, _ = x.shape
    per_batch = shift.shape[0] == b
    mod_map = (lambda bi, si: (bi, 0, 0)) if per_batch else (lambda bi, si: (0, 0, 0))
    full = lambda bi, si: (0, 0)
    return pl.pallas_call(
        _l0_in_kernel,
        grid=(b, l // ts),
        in_specs=[
            pl.BlockSpec((None, ts, D_MODEL), lambda bi, si: (bi, si, 0)),
            pl.BlockSpec((None, 1, D_MODEL), mod_map),
            pl.BlockSpec((None, 1, D_MODEL), mod_map),
            pl.BlockSpec((D_MODEL, MIX_IN_WIDTH), full),
            pl.BlockSpec((ATTN_WIDTH, ATTN_WIDTH), full),
            pl.BlockSpec((1, ATTN_WIDTH), full),
            pl.BlockSpec((1, KV_WIDTH), full),
            pl.BlockSpec((ts, ATTN_WIDTH), lambda bi, si: (si, 0)),
            pl.BlockSpec((ts, ATTN_WIDTH), lambda bi, si: (si, 0)),
        ],
        out_specs=[
            pl.BlockSpec((None, ts, N_Q_HEADS * LANES), lambda bi, si: (bi, si, 0)),
            pl.BlockSpec((None, ts, KV_WIDTH), lambda bi, si: (bi, si, 0)),
            pl.BlockSpec((None, ts, KV_WIDTH), lambda bi, si: (bi, si, 0)),
            pl.BlockSpec((None, ts, POOL_WIDTH), lambda bi, si: (bi, si, 0)),
        ],
        out_shape=[
            jax.ShapeDtypeStruct((b, l, N_Q_HEADS * LANES), BF16),
            jax.ShapeDtypeStruct((b, l, KV_WIDTH), BF16),
            jax.ShapeDtypeStruct((b, l, KV_WIDTH), BF16),
            jax.ShapeDtypeStruct((b, l, POOL_WIDTH), F32),
        ],
        compiler_params=_cparams(("arbitrary", "arbitrary")),
        name="l0_in_proj",
    )(x, shift, scale, w_in_bf, bd, qg, kg, cos_t, sin_t)


_VT_ROWS = HEAD_DIM + 16


_KV_CHUNK = 2176
_MAX_ROWS = 64


def _attn_kernel(q_ref, k_ref, vt_ref, o_ref, ot_ref, *s_refs):
    heads_per_kv = N_Q_HEADS // N_KV_HEADS
    lk, tq = k_ref.shape[0], q_ref.shape[0]
    chunks = [(c0, min(_KV_CHUNK, lk - c0)) for c0 in range(0, lk, _KV_CHUNK)]

    def scores(h, c0, n, m):
        qh = q_ref[:, h * LANES:(h + 1) * LANES]
        s_c = lax.dot_general(k_ref[c0:c0 + n, :], qh, _NT, preferred_element_type=F32)
        s_refs[h % 2][c0:c0 + n, :] = s_c
        mc = jnp.max(s_c.reshape(n // _MAX_ROWS, _MAX_ROWS, tq), axis=0)
        return mc if m is None else jnp.maximum(m, mc)

    def weighted(h, c0, n, m, o_t):
        p_c = jnp.exp2(s_refs[h % 2][c0:c0 + n, :] - m).astype(BF16)
        o_c = jnp.dot(vt_ref[h // heads_per_kv, :, c0:c0 + n], p_c, preferred_element_type=F32)
        return o_c if o_t is None else o_t + o_c

    m_next = None
    for c0, n in chunks:
        m_next = scores(0, c0, n, m_next)
    for h in range(N_Q_HEADS):
        g = h // heads_per_kv
        m = jnp.max(m_next, axis=0, keepdims=True)
        m_next = None
        o_t = None
        for c0, n in chunks:
            o_t = weighted(h, c0, n, m, o_t)
            if h + 1 < N_Q_HEADS:
                m_next = scores(h + 1, c0, n, m_next)
        ot_ref[h * HEAD_DIM:(h + 1) * HEAD_DIM, :] = (
            o_t[:HEAD_DIM, :] / o_t[HEAD_DIM:HEAD_DIM + 1, :])
    o_ref[...] = ot_ref[...].T.astype(BF16)


_BOUNDED_KV_CHUNK = 1088


def _attn_bounded_kernel(bound_ref, q_ref, k_ref, vt_ref, o_ref, ot_ref):
    heads_per_kv = N_Q_HEADS // N_KV_HEADS
    lk = k_ref.shape[0]
    shift = bound_ref[0]
    for h in range(N_Q_HEADS):
        g = h // heads_per_kv
        qh = q_ref[:, h * LANES:(h + 1) * LANES]
        o_t = None
        for c0 in range(0, lk, _BOUNDED_KV_CHUNK):
            n = min(_BOUNDED_KV_CHUNK, lk - c0)
            s_c = lax.dot_general(k_ref[c0:c0 + n, :], qh, _NT, preferred_element_type=F32)
            p_c = jnp.exp2(s_c - shift).astype(BF16)
            o_c = jnp.dot(vt_ref[g, :, c0:c0 + n], p_c, preferred_element_type=F32)
            o_t = o_c if o_t is None else o_t + o_c
        ot_ref[h * HEAD_DIM:(h + 1) * HEAD_DIM, :] = (
            o_t[:HEAD_DIM, :] / o_t[HEAD_DIM:HEAD_DIM + 1, :])
    o_ref[...] = ot_ref[...].T.astype(BF16)


_MAX_SAFE_SCORE_BOUND = 40.0


def _attention(q, k_all, vt, score_bound, tq):
    b, s, qw = q.shape
    lk = k_all.shape[1]
    specs = dict(
        grid=(b, s // tq),
        out_specs=pl.BlockSpec((None, tq, ATTN_WIDTH), lambda bi, qi: (bi, qi, 0)),
        out_shape=jax.ShapeDtypeStruct((b, s, ATTN_WIDTH), BF16),
        compiler_params=_cparams(("arbitrary", "arbitrary")),
    )
    in_specs = [
        pl.BlockSpec((None, tq, qw), lambda bi, qi: (bi, qi, 0)),
        pl.BlockSpec((None, lk, KV_WIDTH), lambda bi, qi: (bi, 0, 0)),
        pl.BlockSpec((None, N_KV_HEADS, _VT_ROWS, lk), lambda bi, qi: (bi, 0, 0, 0)),
    ]

    def bounded(_):
        return pl.pallas_call(
            _attn_bounded_kernel,
            in_specs=[pl.BlockSpec(memory_space=pltpu.SMEM)] + in_specs,
            scratch_shapes=[pltpu.VMEM((ATTN_WIDTH, tq), F32)],
            name="attention_bounded", **specs,
        )(score_bound.reshape(1), q, k_all, vt)

    def exact(_):
        return pl.pallas_call(
            _attn_kernel,
            in_specs=in_specs,
            scratch_shapes=[pltpu.VMEM((ATTN_WIDTH, tq), F32), pltpu.VMEM((lk, tq), F32),
                            pltpu.VMEM((lk, tq), F32)],
            name="attention", **specs,
        )(q, k_all, vt)

    return lax.cond(score_bound < _MAX_SAFE_SCORE_BOUND, bounded, exact, None)


def _split_bf16(t):
    hi = t.astype(BF16)
    return hi, (t - hi.astype(F32)).astype(BF16)


def _router_weights(w_router_layer):
    hi, lo = _split_bf16(w_router_layer.T)
    return jnp.concatenate([hi, lo], axis=0)


def _post_tail(resid, y, gate, lng, lnb, shf, scf, wr2):
    h = _layer_norm(DEEPNORM_ALPHA * resid + gate * y, lng, lnb)
    u = h * (1.0 + scf) + shf
    u_hi, u_lo = _split_bf16(u)
    first = lax.dot_general(wr2, u_hi, _NT, preferred_element_type=F32)
    second = lax.dot_general(wr2[:N_EXPERTS], u_lo, _NT, preferred_element_type=F32)
    logits = first[:N_EXPERTS] + first[N_EXPERTS:] + second
    ex = jnp.exp(logits - jnp.max(logits, axis=0, keepdims=True))
    aff = ex / jnp.sum(ex, axis=0, keepdims=True)
    return h, u, aff


def _fill_halo(ext_ref, prev_ref, main_ref, next_ref, ts):
    st = pl.program_id(1)
    last = pl.num_programs(1) - 1
    ext_ref[0:HALO, :] = jnp.where(st > 0, prev_ref[...], 0.0)
    ext_ref[HALO:HALO + ts, :] = main_ref[...]
    ext_ref[HALO + ts:HALO + ts + HALO, :] = jnp.where(st < last, next_ref[...], 0.0)


def _halo_specs(ts, width, seq):
    r = ts // HALO
    nblk = seq // HALO
    return [
        pl.BlockSpec((None, ts, width), lambda bi, si: (bi, si, 0)),
        pl.BlockSpec((None, HALO, width), lambda bi, si: (bi, jnp.maximum(si * r - 1, 0), 0)),
        pl.BlockSpec((None, HALO, width), lambda bi, si: (bi, jnp.minimum((si + 1) * r, nblk - 1), 0)),
    ]


def _tail_specs(ts):
    row = lambda bi, si: (bi, 0, 0)
    full = lambda bi, si: (0, 0)
    in_specs = [
        pl.BlockSpec((None, ts, D_MODEL), lambda bi, si: (bi, si, 0)),
        pl.BlockSpec((None, 1, D_MODEL), row),
        pl.BlockSpec((1, D_MODEL), full),
        pl.BlockSpec((1, D_MODEL), full),
        pl.BlockSpec((None, 1, D_MODEL), row),
        pl.BlockSpec((None, 1, D_MODEL), row),
        pl.BlockSpec((2 * N_EXPERTS, D_MODEL), full),
    ]
    out_specs = [
        pl.BlockSpec((None, ts, D_MODEL), lambda bi, si: (bi, si, 0)),
        pl.BlockSpec((None, ts * TOKEN_TILE_ROWS, LANES), lambda bi, si: (bi, si, 0)),
        pl.BlockSpec((None, N_EXPERTS, ts), lambda bi, si: (bi, 0, si)),
    ]
    return in_specs, out_specs


def _tail_out_shape(b, s):
    return [jax.ShapeDtypeStruct((b, s, D_MODEL), F32),
            jax.ShapeDtypeStruct((b, s * TOKEN_TILE_ROWS, LANES), F32),
            jax.ShapeDtypeStruct((b, N_EXPERTS, s), F32)]


def _l0_out_kernel(seq, a_ref, p_ref, pprev_ref, pnext_ref, wpool_ref, pscale_ref, wout_ref,
                   x_ref, gate_ref, lng_ref, lnb_ref, shf_ref, scf_ref, wr_ref,
                   h_ref, u_ref, aff_ref, pext_ref):
    ts = p_ref.shape[0]
    _fill_halo(pext_ref, pprev_ref, p_ref, pnext_ref, ts)
    t = pl.program_id(1) * ts + lax.broadcasted_iota(I32, (ts, 1), 0)
    outs = []
    for g, w in enumerate(POOL_WINDOWS):
        sl = slice(g * POOL_GROUP_DIM, (g + 1) * POOL_GROUP_DIM)
        acc = pext_ref[HALO - w // 2:HALO - w // 2 + ts, sl]
        for j in range(-(w // 2) + 1, w // 2):
            acc = acc + pext_ref[HALO + j:HALO + j + ts, sl]
        cnt = jnp.minimum(t + w // 2, seq) - jnp.maximum(t - w // 2, 0)
        pooled = acc / cnt.astype(F32) - p_ref[:, sl]
        og = jnp.dot(pooled.astype(BF16), wpool_ref[g], preferred_element_type=F32)
        outs.append((og * pscale_ref[:, sl]).astype(BF16))
    pool = jnp.concatenate(outs, axis=-1)
    y = (jnp.dot(a_ref[...], wout_ref[:ATTN_WIDTH, :], preferred_element_type=F32)
         + jnp.dot(pool, wout_ref[ATTN_WIDTH:, :], preferred_element_type=F32))
    h, u, aff = _post_tail(x_ref[...], y, gate_ref[...], lng_ref[...], lnb_ref[...],
                           shf_ref[...], scf_ref[...], wr_ref[...])
    h_ref[...] = h
    _store_token_major(u_ref, u)
    aff_ref[...] = aff


def _l0_out_proj(a, p, wpool_bf, pscale, wout_bf, x, gate, lng, lnb, shf, scf, wr_t, ts):
    b, s, _ = x.shape
    full2 = lambda bi, si: (0, 0)
    tail_in, tail_out = _tail_specs(ts)
    in_specs = (
        [pl.BlockSpec((None, ts, ATTN_WIDTH), lambda bi, si: (bi, si, 0))]
        + _halo_specs(ts, POOL_WIDTH, s)
        + [pl.BlockSpec((len(POOL_WINDOWS), POOL_GROUP_DIM, POOL_GROUP_DIM), lambda bi, si: (0, 0, 0)),
           pl.BlockSpec((1, POOL_WIDTH), full2),
           pl.BlockSpec((ATTN_WIDTH + POOL_WIDTH, D_MODEL), full2)]
        + tail_in)
    return pl.pallas_call(
        functools.partial(_l0_out_kernel, s),
        grid=(b, s // ts),
        in_specs=in_specs,
        out_specs=tail_out,
        out_shape=_tail_out_shape(b, s),
        scratch_shapes=[pltpu.VMEM((ts + 2 * HALO, POOL_WIDTH), F32)],
        compiler_params=_cparams(("arbitrary", "arbitrary")),
        name="l0_out_proj",
    )(a, p, p, p, wpool_bf, pscale, wout_bf, x, gate, lng, lnb, shf, scf, wr_t)


def _exclusive_cumsum_lanes(mask, tri):
    n = mask.shape[1]
    w = tri.shape[0]
    carry = jnp.zeros((mask.shape[0], 1), F32)
    parts = []
    for c in range(n // w):
        blk = mask[:, c * w:(c + 1) * w]
        inc = jnp.dot(blk.astype(BF16), tri, preferred_element_type=F32)
        parts.append(inc - blk + carry)
        carry = carry + inc[:, w - 1:w]
    return jnp.concatenate(parts, axis=1)


_TOKEN_SPLIT = 64
_SLOT_LO = 32
_PACKED_VALUES = 5


def _select_kernel(cap, aff_ref, out_ref):
    aff = aff_ref[...]
    e = aff.shape[0]
    as_f32 = lambda word: pltpu.bitcast(word, F32)

    def refine(i, thr):
        cand = thr | jnp.left_shift(jnp.int32(1), 30 - i)
        cnt = jnp.sum(jnp.where(aff >= as_f32(cand), 1.0, 0.0), axis=1, keepdims=True)
        return jnp.where(cnt >= cap, cand, thr)

    thr = lax.fori_loop(0, 31, refine, jnp.zeros((e, 1), I32))
    w = 2 * LANES
    r = lax.broadcasted_iota(I32, (w, w), 0)
    c = lax.broadcasted_iota(I32, (w, w), 1)
    tri = jnp.where(r <= c, 1.0, 0.0).astype(BF16)
    gt = jnp.where(aff >= as_f32(thr + 1), 1.0, 0.0)
    eq = jnp.where(aff >= as_f32(thr), 1.0, 0.0) - gt
    need = cap - jnp.sum(gt, axis=1, keepdims=True)
    sel = gt + eq * jnp.where(_exclusive_cumsum_lanes(eq, tri) < need, 1.0, 0.0)
    pos = _exclusive_cumsum_lanes(sel, tri)
    dest = jnp.where(sel > 0.5, pos, float(cap)).astype(I32)

    n = aff.shape[1]
    n_hi = cap // _SLOT_LO
    d_hi = dest // _SLOT_LO
    d_lo = dest % _SLOT_LO
    t = lax.broadcasted_iota(I32, (1, n), 1)
    t_hi = (t // _TOKEN_SPLIT).astype(F32)
    t_lo = (t % _TOKEN_SPLIT).astype(F32)
    a1 = aff.astype(BF16).astype(F32)
    r1 = aff - a1
    a2 = r1.astype(BF16).astype(F32)
    a3 = r1 - a2
    row_hi = lax.broadcasted_iota(I32, (n_hi, n), 0)
    row_lo = lax.broadcasted_iota(I32, (_SLOT_LO, n), 0)
    for ei in range(e):
        rs = slice(ei, ei + 1)
        h = jnp.where(d_hi[rs] == row_hi, 1.0, 0.0)
        l = jnp.where(d_lo[rs] == row_lo, 1.0, 0.0).astype(BF16)
        vals = jnp.concatenate([h * t_hi, h * t_lo, h * a1[rs], h * a2[rs], h * a3[rs]], axis=0)
        out_ref[ei] = lax.dot_general(vals.astype(BF16), l, _NT, preferred_element_type=F32)


def _select(aff, cap):
    b, e, n = aff.shape
    rows = _PACKED_VALUES * (cap // _SLOT_LO)
    return pl.pallas_call(
        functools.partial(_select_kernel, cap),
        grid=(b,),
        in_specs=[pl.BlockSpec((None, e, n), lambda bi: (bi, 0, 0))],
        out_specs=pl.BlockSpec((None, e, rows, _SLOT_LO), lambda bi: (bi, 0, 0, 0)),
        out_shape=jax.ShapeDtypeStruct((b, e, rows, _SLOT_LO), F32),
        compiler_params=_cparams(("arbitrary",)),
        name="expert_select",
    )(aff)


def _token_tile(t):
    return pl.ds(pl.multiple_of(t * TOKEN_TILE_ROWS, TOKEN_TILE_ROWS), TOKEN_TILE_ROWS)


_GATHER_AHEAD = 2


def _expert_kernel(cap, seq, idx_ref, u_hbm, wg_ref, wu_ref, wd_ref, gate_ref, y_ref,
                   xe_ref, wg_bf, wu_bf, wd_bf, sem):
    b = pl.program_id(1)
    nb = pl.num_programs(1)
    step = pl.program_id(0) * nb + b
    nsteps = pl.num_programs(0) * nb
    nbuf = _GATHER_AHEAD + 1
    slot = step % nbuf

    def start_gather(k, into, rolled):
        ek = k // nb
        bk = k % nb
        base = (bk * N_EXPERTS + ek) * cap
        tok0 = bk * seq

        def start(i):
            tok = tok0 + idx_ref[base + i]
            pltpu.make_async_copy(u_hbm.at[_token_tile(tok), :], xe_ref.at[into, _token_tile(i), :],
                                  sem.at[into]).start()

        if rolled:
            def issue(ci, carry):
                for j in range(SUBLANES):
                    start(ci * SUBLANES + j)
                return carry

            lax.fori_loop(0, cap // SUBLANES, issue, 0)
        else:
            for i in range(cap):
                start(i)

    def wait_gather(into):
        pltpu.make_async_copy(u_hbm.at[pl.ds(0, cap * TOKEN_TILE_ROWS), :], xe_ref.at[into], sem.at[into]).wait()

    @pl.when(step == 0)
    def _():
        for k in range(_GATHER_AHEAD):
            start_gather(k, k, rolled=True)

    @pl.when(b == 0)
    def _():
        wg_bf[...] = wg_ref[...].astype(BF16)
        wu_bf[...] = wu_ref[...].astype(BF16)
        wd_bf[...] = wd_ref[...].astype(BF16)

    wait_gather(slot)
    x = _load_token_major(xe_ref.at[slot], cap).astype(BF16)
    start_gather(jnp.minimum(step + _GATHER_AHEAD, nsteps - 1), (step + _GATHER_AHEAD) % nbuf, rolled=False)
    hg = jnp.dot(x, wg_bf[...], preferred_element_type=F32)
    hu = jnp.dot(x, wu_bf[...], preferred_element_type=F32)
    act = (hg * jax.nn.sigmoid(hg) * hu).astype(BF16)
    _store_token_major(y_ref, jnp.dot(act, wd_bf[...], preferred_element_type=F32) * gate_ref[...])

    @pl.when(step == nsteps - 1)
    def _():
        for k in range(1, nbuf):
            wait_gather((step + k) % nbuf)


def _expert_ffn(idx_flat, u_rows, layer, w_g, w_u, w_d, gate_sel, batch, seq, cap):
    wspec = lambda: pl.BlockSpec((None, None, D_MODEL, EXPERT_FF), lambda e, b, idx: (layer, e, 0, 0))
    grid_spec = pltpu.PrefetchScalarGridSpec(
        num_scalar_prefetch=1,
        grid=(N_EXPERTS, batch),
        in_specs=[
            pl.BlockSpec(memory_space=pl.ANY),
            wspec(), wspec(),
            pl.BlockSpec((None, None, EXPERT_FF, D_MODEL), lambda e, b, idx: (layer, e, 0, 0)),
            pl.BlockSpec((None, None, cap, 1), lambda e, b, idx: (b, e, 0, 0)),
        ],
        out_specs=pl.BlockSpec((None, None, cap * TOKEN_TILE_ROWS, LANES), lambda e, b, idx: (b, e, 0, 0)),
        scratch_shapes=[
            pltpu.VMEM((_GATHER_AHEAD + 1, cap * TOKEN_TILE_ROWS, LANES), F32),
            pltpu.VMEM((D_MODEL, EXPERT_FF), BF16),
            pltpu.VMEM((D_MODEL, EXPERT_FF), BF16),
            pltpu.VMEM((EXPERT_FF, D_MODEL), BF16),
            pltpu.SemaphoreType.DMA((_GATHER_AHEAD + 1,)),
        ],
    )
    return pl.pallas_call(
        functools.partial(_expert_kernel, cap, seq),
        grid_spec=grid_spec,
        out_shape=jax.ShapeDtypeStruct((batch, N_EXPERTS, cap * TOKEN_TILE_ROWS, LANES), F32),
        compiler_params=_cparams(("arbitrary", "arbitrary")),
        name="expert_ffn",
    )(idx_flat, u_rows, w_g, w_u, w_d, gate_sel)


_COMBINE_UNROLL = 8


def _combine_kernel(cap, idx_ref, y_ref, f_ref):
    b = pl.program_id(0)
    e = pl.program_id(1)

    @pl.when(e == 0)
    def _():
        f_ref[...] = jnp.zeros_like(f_ref)

    base = (b * N_EXPERTS + e) * cap

    def chunk(ci, carry):
        i0 = ci * _COMBINE_UNROLL
        tiles = [_token_tile(idx_ref[base + i0 + j]) for j in range(_COMBINE_UNROLL)]
        sums = [f_ref[tiles[j], :] + y_ref[_token_tile(i0 + j), :] for j in range(_COMBINE_UNROLL)]
        for j in range(_COMBINE_UNROLL):
            f_ref[tiles[j], :] = sums[j]
        return carry

    lax.fori_loop(0, cap // _COMBINE_UNROLL, chunk, 0)


def _combine(idx_flat, y, seq, cap):
    batch = y.shape[0]
    grid_spec = pltpu.PrefetchScalarGridSpec(
        num_scalar_prefetch=1,
        grid=(batch, N_EXPERTS),
        in_specs=[pl.BlockSpec((None, None, cap * TOKEN_TILE_ROWS, LANES), lambda b, e, idx: (b, e, 0, 0))],
        out_specs=pl.BlockSpec((None, seq * TOKEN_TILE_ROWS, LANES), lambda b, e, idx: (b, 0, 0)),
    )
    return pl.pallas_call(
        functools.partial(_combine_kernel, cap),
        grid_spec=grid_spec,
        out_shape=jax.ShapeDtypeStruct((batch, seq * TOKEN_TILE_ROWS, LANES), F32),
        compiler_params=_cparams(("arbitrary", "arbitrary")),
        name="expert_combine",
    )(idx_flat, y)


def _moe(u, aff, layer, w_g, w_u, w_d):
    b, _, n = aff.shape
    cap = CAPACITY_FACTOR * n // N_EXPERTS
    packed = _select(aff, cap).reshape(b, N_EXPERTS, _PACKED_VALUES, cap)
    idx_flat = (packed[:, :, 0] * _TOKEN_SPLIT + packed[:, :, 1]).astype(I32).reshape(-1)
    gate_sel = (packed[:, :, 2] + packed[:, :, 3] + packed[:, :, 4])[..., None]
    y = _expert_ffn(idx_flat, u.reshape(b * n * TOKEN_TILE_ROWS, LANES), layer, w_g, w_u, w_d, gate_sel, b, n, cap)
    return _combine(idx_flat, y, n, cap)


def _l1_in_kernel(h_ref, f_ref, gf_ref, lng_ref, lnb_ref, sh_ref, sc_ref, w_ref, h2_ref, bg_ref, cx_ref):
    f = _load_token_major(f_ref, h_ref.shape[0])
    h2 = _layer_norm(DEEPNORM_ALPHA * h_ref[...] + gf_ref[...] * f, lng_ref[...], lnb_ref[...])
    h2_ref[...] = h2
    u = h2 * (1.0 + sc_ref[...]) + sh_ref[...]
    z = jnp.dot(u.astype(BF16), w_ref[...], preferred_element_type=F32)
    bg_ref[...] = z[:, :D_MODEL]
    cx_ref[...] = z[:, D_MODEL:2 * D_MODEL] * z[:, 2 * D_MODEL:]


def _l1_in_proj(h, f, gate_f, lng, lnb, shift, scale, w_bf, ts):
    b, s, _ = h.shape
    tok = lambda: pl.BlockSpec((None, ts, D_MODEL), lambda bi, si: (bi, si, 0))
    row = lambda: pl.BlockSpec((None, 1, D_MODEL), lambda bi, si: (bi, 0, 0))
    vec = lambda: pl.BlockSpec((1, D_MODEL), lambda bi, si: (0, 0))
    return pl.pallas_call(
        _l1_in_kernel,
        grid=(b, s // ts),
        in_specs=[tok(), pl.BlockSpec((None, ts * TOKEN_TILE_ROWS, LANES), lambda bi, si: (bi, si, 0)),
                  row(), vec(), vec(), row(), row(),
                  pl.BlockSpec((D_MODEL, 3 * D_MODEL), lambda bi, si: (0, 0))],
        out_specs=[tok(), tok(), tok()],
        out_shape=[jax.ShapeDtypeStruct((b, s, D_MODEL), F32)] * 3,
        compiler_params=_cparams(("arbitrary", "arbitrary")),
        name="l1_in_proj",
    )(h, f, gate_f, lng, lnb, shift, scale, w_bf)


def _l1_out_kernel(cx_ref, cprev_ref, cnext_ref, bg_ref, cw_ref, wout_ref,
                   x_ref, gate_ref, lng_ref, lnb_ref, shf_ref, scf_ref, wr_ref,
                   h_ref, u_ref, aff_ref, cext_ref):
    ts = cx_ref.shape[0]
    _fill_halo(cext_ref, cprev_ref, cx_ref, cnext_ref, ts)
    z = (cext_ref[HALO - 1:HALO - 1 + ts, :] * cw_ref[0:1, :]
         + cx_ref[...] * cw_ref[1:2, :]
         + cext_ref[HALO + 1:HALO + 1 + ts, :] * cw_ref[2:3, :])
    y = jnp.dot((bg_ref[...] * z).astype(BF16), wout_ref[...], preferred_element_type=F32)
    h, u, aff = _post_tail(x_ref[...], y, gate_ref[...], lng_ref[...], lnb_ref[...],
                           shf_ref[...], scf_ref[...], wr_ref[...])
    h_ref[...] = h
    _store_token_major(u_ref, u)
    aff_ref[...] = aff


def _l1_out_proj(cx, bg, conv_w, wout_bf, x, gate, lng, lnb, shf, scf, wr_t, ts):
    b, s, _ = x.shape
    tail_in, tail_out = _tail_specs(ts)
    in_specs = (
        _halo_specs(ts, D_MODEL, s)
        + [pl.BlockSpec((None, ts, D_MODEL), lambda bi, si: (bi, si, 0)),
           pl.BlockSpec(conv_w.shape, lambda bi, si: (0, 0)),
           pl.BlockSpec((D_MODEL, D_MODEL), lambda bi, si: (0, 0))]
        + tail_in)
    return pl.pallas_call(
        _l1_out_kernel,
        grid=(b, s // ts),
        in_specs=in_specs,
        out_specs=tail_out,
        out_shape=_tail_out_shape(b, s),
        scratch_shapes=[pltpu.VMEM((ts + 2 * HALO, D_MODEL), F32)],
        compiler_params=_cparams(("arbitrary", "arbitrary")),
        name="l1_out_proj",
    )(cx, cx, cx, bg, conv_w, wout_bf, x, gate, lng, lnb, shf, scf, wr_t)


def _final_kernel(h_ref, f_ref, gf_ref, lng_ref, lnb_ref, o_ref):
    f = _load_token_major(f_ref, h_ref.shape[0])
    o_ref[...] = _layer_norm(DEEPNORM_ALPHA * h_ref[...] + gf_ref[...] * f, lng_ref[...], lnb_ref[...])


def _final_norm(h, f, gate_f, lng, lnb, ts):
    b, s, _ = h.shape
    tok = lambda: pl.BlockSpec((None, ts, D_MODEL), lambda bi, si: (bi, si, 0))
    return pl.pallas_call(
        _final_kernel,
        grid=(b, s // ts),
        in_specs=[tok(), pl.BlockSpec((None, ts * TOKEN_TILE_ROWS, LANES), lambda bi, si: (bi, si, 0)),
                  pl.BlockSpec((None, 1, D_MODEL), lambda bi, si: (bi, 0, 0)),
                  pl.BlockSpec((1, D_MODEL), lambda bi, si: (0, 0)),
                  pl.BlockSpec((1, D_MODEL), lambda bi, si: (0, 0))],
        out_specs=tok(),
        out_shape=jax.ShapeDtypeStruct((b, s, D_MODEL), F32),
        compiler_params=_cparams(("arbitrary", "arbitrary")),
        name="final_norm",
    )(h, f, gate_f, lng, lnb)


def _rope_lane_tables(seq):
    rows = seq // GRID_W
    row = jnp.repeat(jnp.arange(rows, dtype=F32), GRID_W)
    col = jnp.tile(jnp.arange(GRID_W, dtype=F32), rows)
    axis_dim = HEAD_DIM // 2
    inv_freq = ROPE_THETA ** (-jnp.arange(0, axis_dim, 2, dtype=F32) / axis_dim)
    ang = jnp.concatenate([row[:, None] * inv_freq, col[:, None] * inv_freq], axis=-1)
    cos = jnp.repeat(jnp.cos(ang), 2, axis=-1)
    sin = jnp.repeat(jnp.sin(ang), 2, axis=-1)
    sign = jnp.tile(jnp.array([-1.0, 1.0], F32), HEAD_DIM // 2)
    return jnp.tile(cos, (1, N_Q_HEADS)), jnp.tile(sin * sign, (1, N_Q_HEADS))


def kernel(x, c, ctx, c_ctx, w_mod, b_mod, ln_mix_g, ln_mix_b, ln_ffn_g, ln_ffn_b, w_mix_in, q_norm_g, k_norm_g, w_pool_grp, pool_scale, w_mix_out, w_conv_in, conv_w, w_conv_out, w_router, w_exp_gate, w_exp_up, w_exp_down):
    b, s, d = x.shape
    lc = ctx.shape[1]
    ts = 512

    cond_rows = jnp.zeros((SUBLANES, d), F32).at[:b].set(c).at[b].set(c_ctx)
    mods = _modulation(cond_rows, w_mod, b_mod).reshape(DEPTH, SUBLANES, 6, d)
    m_lat = lambda layer, k: mods[layer, :b, k][:, None, :]
    m_ctx = lambda layer, k: mods[layer, b:b + 1, k][:, None, :]
    vec = lambda t: t.reshape(1, -1)

    w_in_bf = w_mix_in[0].astype(BF16)
    grp = lax.broadcasted_iota(I32, (ATTN_WIDTH, ATTN_WIDTH), 0) // HEAD_DIM
    bd = jnp.where(grp == grp.T, 1.0 / HEAD_DIM, 0.0).astype(BF16)
    qg = jnp.tile(q_norm_g[0], N_Q_HEADS).reshape(1, -1)
    kg = jnp.tile(k_norm_g[0], N_KV_HEADS).reshape(1, -1)
    cos_t, sin_t = _rope_lane_tables(s)
    q, k, v, p = _l0_in_proj(x, m_lat(0, 0), m_lat(0, 1), w_in_bf, bd, qg, kg, cos_t, sin_t, ts)
    ones = jnp.ones((lc, ATTN_WIDTH), F32)
    _, kc, vc, _ = _l0_in_proj(ctx, m_ctx(0, 0), m_ctx(0, 1), w_in_bf, bd, qg, kg, ones, 0.0 * ones, lc)
    k_all = jnp.concatenate([kc, k], axis=1)
    v_t = jnp.swapaxes(jnp.concatenate([vc, v], axis=1), 1, 2).reshape(b, N_KV_HEADS, HEAD_DIM, lc + s)
    v_t = jnp.concatenate([v_t, jnp.ones((b, N_KV_HEADS, _VT_ROWS - HEAD_DIM, lc + s), BF16)], axis=2)
    score_bound = (1.02 * HEAD_DIM ** 0.5 * LOG2_E) * jnp.max(jnp.abs(q_norm_g[0])) * jnp.max(jnp.abs(k_norm_g[0]))
    a = _attention(q, k_all, v_t, score_bound, 256)
    h, u, aff = _l0_out_proj(a, p, w_pool_grp[0].astype(BF16), vec(pool_scale[0]), w_mix_out[0].astype(BF16),
                             x, m_lat(0, 2), vec(ln_mix_g[0]), vec(ln_mix_b[0]), m_lat(0, 3), m_lat(0, 4),
                             _router_weights(w_router[0]), ts)
    f = _moe(u, aff, 0, w_exp_gate, w_exp_up, w_exp_down)

    h, bg, cx = _l1_in_proj(h, f, m_lat(0, 5), vec(ln_ffn_g[0]), vec(ln_ffn_b[0]), m_lat(1, 0), m_lat(1, 1),
                            w_conv_in[0].astype(BF16), ts)
    h, u, aff = _l1_out_proj(cx, bg, conv_w[0], w_conv_out[0].astype(BF16),
                             h, m_lat(1, 2), vec(ln_mix_g[1]), vec(ln_mix_b[1]), m_lat(1, 3), m_lat(1, 4),
                             _router_weights(w_router[1]), ts)
    f = _moe(u, aff, 1, w_exp_gate, w_exp_up, w_exp_down)
    return _final_norm(h, f, m_lat(1, 5), vec(ln_ffn_g[1]), vec(ln_ffn_b[1]), ts)
```

```python
import functools

import jax
import jax.numpy as jnp
from jax import lax
from jax.experimental import pallas as pl
from jax.experimental.pallas import tpu as pltpu

F32, BF16, I32 = jnp.float32, jnp.bfloat16, jnp.int32

D_MODEL = 1024
DEPTH = 2
GRID_W = 64
N_Q_HEADS = 8
N_KV_HEADS = 2
HEAD_DIM = 64
ATTN_WIDTH = N_Q_HEADS * HEAD_DIM
KV_WIDTH = N_KV_HEADS * HEAD_DIM
ROPE_THETA = 10000.0
POOL_WINDOWS = (2, 4, 8, 16)
POOL_GROUP_DIM = 128
POOL_WIDTH = len(POOL_WINDOWS) * POOL_GROUP_DIM
MIX_IN_WIDTH = ATTN_WIDTH + 2 * KV_WIDTH + POOL_WIDTH
N_EXPERTS = 16
EXPERT_FF = 1024
CAPACITY_FACTOR = 2
NORM_EPS = 1e-6
DEEPNORM_ALPHA = (2 * DEPTH) ** 0.25
LOG2_E = 1.4426950408889634

LANES = 128
SUBLANES = 8
HALO = SUBLANES
VMEM_LIMIT = 56 * 1024 * 1024

_NT = (((1,), (1,)), ((), ()))


def _cparams(sem):
    return pltpu.CompilerParams(dimension_semantics=sem, vmem_limit_bytes=VMEM_LIMIT)


TOKEN_TILE_ROWS = D_MODEL // LANES


def _store_token_major(ref, rows):
    n = rows.shape[0]
    for j in range(TOKEN_TILE_ROWS):
        ref[pl.ds(j, n, stride=TOKEN_TILE_ROWS), :] = rows[:, j * LANES:(j + 1) * LANES]


def _load_token_major(ref, n):
    return jnp.concatenate(
        [ref[pl.ds(j, n, stride=TOKEN_TILE_ROWS), :] for j in range(TOKEN_TILE_ROWS)], axis=1)


def _layer_norm(z, g, b):
    mu = jnp.mean(z, axis=-1, keepdims=True)
    zc = z - mu
    var = jnp.mean(zc * zc, axis=-1, keepdims=True)
    return zc * lax.rsqrt(var + NORM_EPS) * g + b


def _split_bf16(t):
    hi = t.astype(BF16)
    return hi, (t - hi.astype(F32)).astype(BF16)


def _mod_kernel(c_ref, w_ref, b_ref, o_ref):
    c = c_ref[...]
    c_hi, c_lo = _split_bf16(c * jax.nn.sigmoid(c))
    w_hi, w_lo = _split_bf16(w_ref[...])
    dot = functools.partial(jnp.dot, preferred_element_type=F32)
    o_ref[...] = dot(c_hi, w_hi) + dot(c_lo, w_hi) + dot(c_hi, w_lo) + b_ref[...]


def _modulation(cond_rows, w_mod, b_mod):
    n_out = w_mod.shape[-1]
    tn = D_MODEL
    return pl.pallas_call(
        _mod_kernel,
        grid=(DEPTH, n_out // tn),
        in_specs=[
            pl.BlockSpec((SUBLANES, D_MODEL), lambda l, j: (0, 0)),
            pl.BlockSpec((None, D_MODEL, tn), lambda l, j: (l, 0, j)),
            pl.BlockSpec((None, 1, tn), lambda l, j: (l, 0, j)),
        ],
        out_specs=pl.BlockSpec((None, SUBLANES, tn), lambda l, j: (l, 0, j)),
        out_shape=jax.ShapeDtypeStruct((DEPTH, SUBLANES, n_out), F32),
        compiler_params=_cparams(("arbitrary", "arbitrary")),
        name="modulation",
    )(cond_rows, w_mod, b_mod.reshape(DEPTH, 1, n_out))


def _group_mean_sq(t, bd):
    hi, lo = _split_bf16(t * t)
    return (jnp.dot(hi, bd, preferred_element_type=F32) + jnp.dot(lo, bd, preferred_element_type=F32))


def _rope_chunk(y, c, s_signed):
    n = y.shape[-1]
    lane = lax.broadcasted_iota(I32, y.shape, 1)
    nxt = pltpu.roll(y, n - 1, axis=1)
    prv = pltpu.roll(y, 1, axis=1)
    partner = jnp.where((lane & 1) == 0, nxt, prv)
    return y * c + partner * s_signed


def _l0_in_kernel(x_ref, sh_ref, sc_ref, w_ref, bd_ref, qg_ref, kg_ref, cos_ref, sin_ref,
                  q_ref, k_ref, v_ref, p_ref):
    u = x_ref[...] * (1.0 + sc_ref[...]) + sh_ref[...]
    h = jnp.dot(u.astype(BF16), w_ref[...], preferred_element_type=F32)
    q = h[:, :ATTN_WIDTH]
    k = h[:, ATTN_WIDTH:ATTN_WIDTH + KV_WIDTH]
    bd = bd_ref[...]
    qn = q * lax.rsqrt(_group_mean_sq(q, bd) + NORM_EPS) * qg_ref[...]
    kn = k * lax.rsqrt(_group_mean_sq(k, bd[:KV_WIDTH, :KV_WIDTH]) + NORM_EPS) * kg_ref[...]
    scale = HEAD_DIM ** -0.5 * LOG2_E
    heads_per_kv = N_Q_HEADS // N_KV_HEADS
    lane = lax.broadcasted_iota(I32, (x_ref.shape[0], LANES), 1)
    for c in range(ATTN_WIDTH // LANES):
        sl = slice(c * LANES, (c + 1) * LANES)
        pair = _rope_chunk(qn[:, sl], cos_ref[:, sl], sin_ref[:, sl]) * scale
        swapped = pltpu.roll(pair, HEAD_DIM, axis=1)
        g = (2 * c) // heads_per_kv
        keep = (lane < HEAD_DIM) if g == 0 else (lane >= HEAD_DIM)
        first, second = (pair, swapped) if g == 0 else (swapped, pair)
        q_ref[:, (2 * c) * LANES:(2 * c + 1) * LANES] = jnp.where(keep, first, 0.0).astype(BF16)
        q_ref[:, (2 * c + 1) * LANES:(2 * c + 2) * LANES] = jnp.where(keep, second, 0.0).astype(BF16)
    k_ref[...] = _rope_chunk(kn, cos_ref[:, :KV_WIDTH], sin_ref[:, :KV_WIDTH]).astype(BF16)
    v_ref[...] = h[:, ATTN_WIDTH + KV_WIDTH:ATTN_WIDTH + 2 * KV_WIDTH].astype(BF16)
    p_ref[...] = h[:, ATTN_WIDTH + 2 * KV_WIDTH:]


def _l0_in_proj(x, shift, scale, w_in_bf, bd, qg, kg, cos_t, sin_t, ts):
    b, l, _ = x.shape
    per_batch = shift.shape[0] == b
    mod_map = (lambda bi, si: (bi, 0, 0)) if per_batch else (lambda bi, si: (0, 0, 0))
    full = lambda bi, si: (0, 0)
    return pl.pallas_call(
        _l0_in_kernel,
        grid=(b, l // ts),
        in_specs=[
            pl.BlockSpec((None, ts, D_MODEL), lambda bi, si: (bi, si, 0)),
            pl.BlockSpec((None, 1, D_MODEL), mod_map),
            pl.BlockSpec((None, 1, D_MODEL), mod_map),
            pl.BlockSpec((D_MODEL, MIX_IN_WIDTH), full),
            pl.BlockSpec((ATTN_WIDTH, ATTN_WIDTH), full),
            pl.BlockSpec((1, ATTN_WIDTH), full),
            pl.BlockSpec((1, KV_WIDTH), full),
            pl.BlockSpec((ts, ATTN_WIDTH), lambda bi, si: (si, 0)),
            pl.BlockSpec((ts, ATTN_WIDTH), lambda bi, si: (si, 0)),
        ],
        out_specs=[
            pl.BlockSpec((None, ts, N_Q_HEADS * LANES), lambda bi, si: (bi, si, 0)),
            pl.BlockSpec((None, ts, KV_WIDTH), lambda bi, si: (bi, si, 0)),
            pl.BlockSpec((None, ts, KV_WIDTH), lambda bi, si: (bi, si, 0)),
            pl.BlockSpec((None, ts, POOL_WIDTH), lambda bi, si: (bi, si, 0)),
        ],
        out_shape=[
            jax.ShapeDtypeStruct((b, l, N_Q_HEADS * LANES), BF16),
            jax.ShapeDtypeStruct((b, l, KV_WIDTH), BF16),
            jax.ShapeDtypeStruct((b, l, KV_WIDTH), BF16),
            jax.ShapeDtypeStruct((b, l, POOL_WIDTH), F32),
        ],
        compiler_params=_cparams(("arbitrary", "arbitrary")),
        name="l0_in_proj",
    )(x, shift, scale, w_in_bf, bd, qg, kg, cos_t, sin_t)


_VT_ROWS = HEAD_DIM + 16


_KV_CHUNK = 2176
_MAX_ROWS = 64


def _attn_kernel(q_ref, k_ref, vt_ref, o_ref, ot_ref, *s_refs):
    heads_per_kv = N_Q_HEADS // N_KV_HEADS
    lk, tq = k_ref.shape[0], q_ref.shape[0]
    chunks = [(c0, min(_KV_CHUNK, lk - c0)) for c0 in range(0, lk, _KV_CHUNK)]

    def scores(h, c0, n, m):
        qh = q_ref[:, h * LANES:(h + 1) * LANES]
        s_c = lax.dot_general(k_ref[c0:c0 + n, :], qh, _NT, preferred_element_type=F32)
        s_refs[h % 2][c0:c0 + n, :] = s_c
        mc = jnp.max(s_c.reshape(n // _MAX_ROWS, _MAX_ROWS, tq), axis=0)
        return mc if m is None else jnp.maximum(m, mc)

    def weighted(h, c0, n, m, o_t):
        p_c = jnp.exp2(s_refs[h % 2][c0:c0 + n, :] - m).astype(BF16)
        o_c = jnp.dot(vt_ref[h // heads_per_kv, :, c0:c0 + n], p_c, preferred_element_type=F32)
        return o_c if o_t is None else o_t + o_c

    m_next = None
    for c0, n in chunks:
        m_next = scores(0, c0, n, m_next)
    for h in range(N_Q_HEADS):
        g = h // heads_per_kv
        m = jnp.max(m_next, axis=0, keepdims=True)
        m_next = None
        o_t = None
        for c0, n in chunks:
            o_t = weighted(h, c0, n, m, o_t)
            if h + 1 < N_Q_HEADS:
                m_next = scores(h + 1, c0, n, m_next)
        ot_ref[h * HEAD_DIM:(h + 1) * HEAD_DIM, :] = (
            o_t[:HEAD_DIM, :] / o_t[HEAD_DIM:HEAD_DIM + 1, :])
    o_ref[...] = ot_ref[...].T.astype(BF16)


_BOUNDED_KV_CHUNK = 1088


def _attn_bounded_kernel(bound_ref, q_ref, k_ref, vt_ref, o_ref, ot_ref):
    heads_per_kv = N_Q_HEADS // N_KV_HEADS
    lk = k_ref.shape[0]
    shift = bound_ref[0]
    for h in range(N_Q_HEADS):
        g = h // heads_per_kv
        qh = q_ref[:, h * LANES:(h + 1) * LANES]
        o_t = None
        for c0 in range(0, lk, _BOUNDED_KV_CHUNK):
            n = min(_BOUNDED_KV_CHUNK, lk - c0)
            s_c = lax.dot_general(k_ref[c0:c0 + n, :], qh, _NT, preferred_element_type=F32)
            p_c = jnp.exp2(s_c - shift).astype(BF16)
            o_c = jnp.dot(vt_ref[g, :, c0:c0 + n], p_c, preferred_element_type=F32)
            o_t = o_c if o_t is None else o_t + o_c
        ot_ref[h * HEAD_DIM:(h + 1) * HEAD_DIM, :] = (
            o_t[:HEAD_DIM, :] / o_t[HEAD_DIM:HEAD_DIM + 1, :])
    o_ref[...] = ot_ref[...].T.astype(BF16)


_MAX_SAFE_SCORE_BOUND = 40.0


def _attention(q, k_all, vt, score_bound, tq):
    b, s, qw = q.shape
    lk = k_all.shape[1]
    specs = dict(
        grid=(b, s // tq),
        out_specs=pl.BlockSpec((None, tq, ATTN_WIDTH), lambda bi, qi: (bi, qi, 0)),
        out_shape=jax.ShapeDtypeStruct((b, s, ATTN_WIDTH), BF16),
        compiler_params=_cparams(("arbitrary", "arbitrary")),
    )
    in_specs = [
        pl.BlockSpec((None, tq, qw), lambda bi, qi: (bi, qi, 0)),
        pl.BlockSpec((None, lk, KV_WIDTH), lambda bi, qi: (bi, 0, 0)),
        pl.BlockSpec((None, N_KV_HEADS, _VT_ROWS, lk), lambda bi, qi: (bi, 0, 0, 0)),
    ]

    def bounded(_):
        return pl.pallas_call(
            _attn_bounded_kernel,
            in_specs=[pl.BlockSpec(memory_space=pltpu.SMEM)] + in_specs,
            scratch_shapes=[pltpu.VMEM((ATTN_WIDTH, tq), F32)],
            name="attention_bounded", **specs,
        )(score_bound.reshape(1), q, k_all, vt)

    def exact(_):
        return pl.pallas_call(
            _attn_kernel,
            in_specs=in_specs,
            scratch_shapes=[pltpu.VMEM((ATTN_WIDTH, tq), F32), pltpu.VMEM((lk, tq), F32),
                            pltpu.VMEM((lk, tq), F32)],
            name="attention", **specs,
        )(q, k_all, vt)

    return lax.cond(score_bound < _MAX_SAFE_SCORE_BOUND, bounded, exact, None)


def _router_weights(w_router_layer):
    hi, lo = _split_bf16(w_router_layer.T)
    return jnp.concatenate([hi, lo], axis=0)


def _post_tail(resid, y, gate, lng, lnb, shf, scf, wr2):
    h = _layer_norm(DEEPNORM_ALPHA * resid + gate * y, lng, lnb)
    u = h * (1.0 + scf) + shf
    u_hi, u_lo = _split_bf16(u)
    first = lax.dot_general(wr2, u_hi, _NT, preferred_element_type=F32)
    second = lax.dot_general(wr2[:N_EXPERTS], u_lo, _NT, preferred_element_type=F32)
    logits = first[:N_EXPERTS] + first[N_EXPERTS:] + second
    ex = jnp.exp(logits - jnp.max(logits, axis=0, keepdims=True))
    aff = ex / jnp.sum(ex, axis=0, keepdims=True)
    return h, u, aff


def _fill_halo(ext_ref, prev_ref, main_ref, next_ref, ts):
    st = pl.program_id(1)
    last = pl.num_programs(1) - 1
    ext_ref[0:HALO, :] = jnp.where(st > 0, prev_ref[...], 0.0)
    ext_ref[HALO:HALO + ts, :] = main_ref[...]
    ext_ref[HALO + ts:HALO + ts + HALO, :] = jnp.where(st < last, next_ref[...], 0.0)


def _halo_specs(ts, width, seq):
    r = ts // HALO
    nblk = seq // HALO
    return [
        pl.BlockSpec((None, ts, width), lambda bi, si: (bi, si, 0)),
        pl.BlockSpec((None, HALO, width), lambda bi, si: (bi, jnp.maximum(si * r - 1, 0), 0)),
        pl.BlockSpec((None, HALO, width), lambda bi, si: (bi, jnp.minimum((si + 1) * r, nblk - 1), 0)),
    ]


def _tail_specs(ts):
    row = lambda bi, si: (bi, 0, 0)
    full = lambda bi, si: (0, 0)
    in_specs = [
        pl.BlockSpec((None, ts, D_MODEL), lambda bi, si: (bi, si, 0)),
        pl.BlockSpec((None, 1, D_MODEL), row),
        pl.BlockSpec((1, D_MODEL), full),
        pl.BlockSpec((1, D_MODEL), full),
        pl.BlockSpec((None, 1, D_MODEL), row),
        pl.BlockSpec((None, 1, D_MODEL), row),
        pl.BlockSpec((2 * N_EXPERTS, D_MODEL), full),
    ]
    out_specs = [
        pl.BlockSpec((None, ts, D_MODEL), lambda bi, si: (bi, si, 0)),
        pl.BlockSpec((None, ts * TOKEN_TILE_ROWS, LANES), lambda bi, si: (bi, si, 0)),
        pl.BlockSpec((None, N_EXPERTS, ts), lambda bi, si: (bi, 0, si)),
    ]
    return in_specs, out_specs


def _tail_out_shape(b, s):
    return [jax.ShapeDtypeStruct((b, s, D_MODEL), F32),
            jax.ShapeDtypeStruct((b, s * TOKEN_TILE_ROWS, LANES), F32),
            jax.ShapeDtypeStruct((b, N_EXPERTS, s), F32)]


def _l0_out_kernel(seq, a_ref, p_ref, pprev_ref, pnext_ref, wpool_ref, pscale_ref, wout_ref,
                   x_ref, gate_ref, lng_ref, lnb_ref, shf_ref, scf_ref, wr_ref,
                   h_ref, u_ref, aff_ref, pext_ref):
    ts = p_ref.shape[0]
    _fill_halo(pext_ref, pprev_ref, p_ref, pnext_ref, ts)
    t = pl.program_id(1) * ts + lax.broadcasted_iota(I32, (ts, 1), 0)
    outs = []
    for g, w in enumerate(POOL_WINDOWS):
        sl = slice(g * POOL_GROUP_DIM, (g + 1) * POOL_GROUP_DIM)
        fwd = pext_ref[:, sl]
        n_ext = fwd.shape[0]
        span = 1
        while span < w // 2:
            fwd = fwd + pltpu.roll(fwd, n_ext - span, axis=0)
            span *= 2
        acc = (fwd + pltpu.roll(fwd, w // 2, axis=0))[HALO:HALO + ts, :]
        cnt = jnp.minimum(t + w // 2, seq) - jnp.maximum(t - w // 2, 0)
        pooled = acc / cnt.astype(F32) - p_ref[:, sl]
        og = jnp.dot(pooled.astype(BF16), wpool_ref[g], preferred_element_type=F32)
        outs.append((og * pscale_ref[:, sl]).astype(BF16))
    pool = jnp.concatenate(outs, axis=-1)
    y = (jnp.dot(a_ref[...], wout_ref[:ATTN_WIDTH, :], preferred_element_type=F32)
         + jnp.dot(pool, wout_ref[ATTN_WIDTH:, :], preferred_element_type=F32))
    h, u, aff = _post_tail(x_ref[...], y, gate_ref[...], lng_ref[...], lnb_ref[...],
                           shf_ref[...], scf_ref[...], wr_ref[...])
    h_ref[...] = h
    _store_token_major(u_ref, u)
    aff_ref[...] = aff


def _l0_out_proj(a, p, wpool_bf, pscale, wout_bf, x, gate, lng, lnb, shf, scf, wr_t, ts):
    b, s, _ = x.shape
    full2 = lambda bi, si: (0, 0)
    tail_in, tail_out = _tail_specs(ts)
    in_specs = (
        [pl.BlockSpec((None, ts, ATTN_WIDTH), lambda bi, si: (bi, si, 0))]
        + _halo_specs(ts, POOL_WIDTH, s)
        + [pl.BlockSpec((len(POOL_WINDOWS), POOL_GROUP_DIM, POOL_GROUP_DIM), lambda bi, si: (0, 0, 0)),
           pl.BlockSpec((1, POOL_WIDTH), full2),
           pl.BlockSpec((ATTN_WIDTH + POOL_WIDTH, D_MODEL), full2)]
        + tail_in)
    return pl.pallas_call(
        functools.partial(_l0_out_kernel, s),
        grid=(b, s // ts),
        in_specs=in_specs,
        out_specs=tail_out,
        out_shape=_tail_out_shape(b, s),
        scratch_shapes=[pltpu.VMEM((ts + 2 * HALO, POOL_WIDTH), F32)],
        compiler_params=_cparams(("arbitrary", "arbitrary")),
        name="l0_out_proj",
    )(a, p, p, p, wpool_bf, pscale, wout_bf, x, gate, lng, lnb, shf, scf, wr_t)


def _exclusive_cumsum_lanes(mask, tri):
    n = mask.shape[1]
    w = tri.shape[0]
    carry = jnp.zeros((mask.shape[0], 1), F32)
    parts = []
    for c in range(n // w):
        blk = mask[:, c * w:(c + 1) * w]
        inc = jnp.dot(blk.astype(BF16), tri, preferred_element_type=F32)
        parts.append(inc - blk + carry)
        carry = carry + inc[:, w - 1:w]
    return jnp.concatenate(parts, axis=1)


_TOKEN_SPLIT = 64
_SLOT_LO = 32
_PACKED_VALUES = 5


def _select_kernel(cap, aff_ref, out_ref):
    aff = aff_ref[...]
    e = aff.shape[0]
    as_f32 = lambda word: pltpu.bitcast(word, F32)

    def refine(i, thr):
        cand = thr | jnp.left_shift(jnp.int32(1), 30 - i)
        cnt = jnp.sum(jnp.where(aff >= as_f32(cand), 1.0, 0.0), axis=1, keepdims=True)
        return jnp.where(cnt >= cap, cand, thr)

    thr = lax.fori_loop(0, 31, refine, jnp.zeros((e, 1), I32))
    w = 2 * LANES
    r = lax.broadcasted_iota(I32, (w, w), 0)
    c = lax.broadcasted_iota(I32, (w, w), 1)
    tri = jnp.where(r <= c, 1.0, 0.0).astype(BF16)
    gt = jnp.where(aff >= as_f32(thr + 1), 1.0, 0.0)
    eq = jnp.where(aff >= as_f32(thr), 1.0, 0.0) - gt
    need = cap - jnp.sum(gt, axis=1, keepdims=True)
    sel = gt + eq * jnp.where(_exclusive_cumsum_lanes(eq, tri) < need, 1.0, 0.0)
    pos = _exclusive_cumsum_lanes(sel, tri)
    dest = jnp.where(sel > 0.5, pos, float(cap)).astype(I32)

    n = aff.shape[1]
    n_hi = cap // _SLOT_LO
    d_hi = dest // _SLOT_LO
    d_lo = dest % _SLOT_LO
    t = lax.broadcasted_iota(I32, (1, n), 1)
    t_hi = (t // _TOKEN_SPLIT).astype(F32)
    t_lo = (t % _TOKEN_SPLIT).astype(F32)
    a1 = aff.astype(BF16).astype(F32)
    r1 = aff - a1
    a2 = r1.astype(BF16).astype(F32)
    a3 = r1 - a2
    row_hi = lax.broadcasted_iota(I32, (n_hi, n), 0)
    row_lo = lax.broadcasted_iota(I32, (_SLOT_LO, n), 0)
    for ei in range(e):
        rs = slice(ei, ei + 1)
        h = jnp.where(d_hi[rs] == row_hi, 1.0, 0.0)
        l = jnp.where(d_lo[rs] == row_lo, 1.0, 0.0).astype(BF16)
        vals = jnp.concatenate([h * t_hi, h * t_lo, h * a1[rs], h * a2[rs], h * a3[rs]], axis=0)
        out_ref[ei] = lax.dot_general(vals.astype(BF16), l, _NT, preferred_element_type=F32)


def _select(aff, cap):
    b, e, n = aff.shape
    rows = _PACKED_VALUES * (cap // _SLOT_LO)
    return pl.pallas_call(
        functools.partial(_select_kernel, cap),
        grid=(b,),
        in_specs=[pl.BlockSpec((None, e, n), lambda bi: (bi, 0, 0))],
        out_specs=pl.BlockSpec((None, e, rows, _SLOT_LO), lambda bi: (bi, 0, 0, 0)),
        out_shape=jax.ShapeDtypeStruct((b, e, rows, _SLOT_LO), F32),
        compiler_params=_cparams(("arbitrary",)),
        name="expert_select",
    )(aff)


def _token_tile(t):
    return pl.ds(pl.multiple_of(t * TOKEN_TILE_ROWS, TOKEN_TILE_ROWS), TOKEN_TILE_ROWS)


_GATHER_AHEAD = 2


def _expert_kernel(cap, seq, idx_ref, u_hbm, wg_ref, wu_ref, wd_ref, gate_ref, y_ref,
                   xe_ref, wg_bf, wu_bf, wd_bf, sem):
    b = pl.program_id(1)
    nb = pl.num_programs(1)
    step = pl.program_id(0) * nb + b
    nsteps = pl.num_programs(0) * nb
    nbuf = _GATHER_AHEAD + 1
    slot = step % nbuf

    def start_gather(k, into, rolled):
        ek = k // nb
        bk = k % nb
        base = (bk * N_EXPERTS + ek) * cap
        tok0 = bk * seq

        def start(i):
            tok = tok0 + idx_ref[base + i]
            pltpu.make_async_copy(u_hbm.at[_token_tile(tok), :], xe_ref.at[into, _token_tile(i), :],
                                  sem.at[into]).start()

        if rolled:
            def issue(ci, carry):
                for j in range(SUBLANES):
                    start(ci * SUBLANES + j)
                return carry

            lax.fori_loop(0, cap // SUBLANES, issue, 0)
        else:
            for i in range(cap):
                start(i)

    def wait_gather(into):
        pltpu.make_async_copy(u_hbm.at[pl.ds(0, cap * TOKEN_TILE_ROWS), :], xe_ref.at[into], sem.at[into]).wait()

    @pl.when(step == 0)
    def _():
        for k in range(_GATHER_AHEAD):
            start_gather(k, k, rolled=True)

    @pl.when(b == 0)
    def _():
        wg_bf[...] = wg_ref[...].astype(BF16)
        wu_bf[...] = wu_ref[...].astype(BF16)
        wd_bf[...] = wd_ref[...].astype(BF16)

    wait_gather(slot)
    x = _load_token_major(xe_ref.at[slot], cap).astype(BF16)
    start_gather(jnp.minimum(step + _GATHER_AHEAD, nsteps - 1), (step + _GATHER_AHEAD) % nbuf, rolled=False)
    hg = jnp.dot(x, wg_bf[...], preferred_element_type=F32)
    hu = jnp.dot(x, wu_bf[...], preferred_element_type=F32)
    act = (hg * jax.nn.sigmoid(hg) * hu).astype(BF16)
    _store_token_major(y_ref, jnp.dot(act, wd_bf[...], preferred_element_type=F32) * gate_ref[...])

    @pl.when(step == nsteps - 1)
    def _():
        for k in range(1, nbuf):
            wait_gather((step + k) % nbuf)


def _expert_ffn(idx_flat, u_rows, layer, w_g, w_u, w_d, gate_sel, batch, seq, cap):
    wspec = lambda: pl.BlockSpec((None, None, D_MODEL, EXPERT_FF), lambda e, b, idx: (layer, e, 0, 0))
    grid_spec = pltpu.PrefetchScalarGridSpec(
        num_scalar_prefetch=1,
        grid=(N_EXPERTS, batch),
        in_specs=[
            pl.BlockSpec(memory_space=pl.ANY),
            wspec(), wspec(),
            pl.BlockSpec((None, None, EXPERT_FF, D_MODEL), lambda e, b, idx: (layer, e, 0, 0)),
            pl.BlockSpec((None, None, cap, 1), lambda e, b, idx: (b, e, 0, 0)),
        ],
        out_specs=pl.BlockSpec((None, None, cap * TOKEN_TILE_ROWS, LANES), lambda e, b, idx: (b, e, 0, 0)),
        scratch_shapes=[
            pltpu.VMEM((_GATHER_AHEAD + 1, cap * TOKEN_TILE_ROWS, LANES), F32),
            pltpu.VMEM((D_MODEL, EXPERT_FF), BF16),
            pltpu.VMEM((D_MODEL, EXPERT_FF), BF16),
            pltpu.VMEM((EXPERT_FF, D_MODEL), BF16),
            pltpu.SemaphoreType.DMA((_GATHER_AHEAD + 1,)),
        ],
    )
    return pl.pallas_call(
        functools.partial(_expert_kernel, cap, seq),
        grid_spec=grid_spec,
        out_shape=jax.ShapeDtypeStruct((batch, N_EXPERTS, cap * TOKEN_TILE_ROWS, LANES), F32),
        compiler_params=_cparams(("arbitrary", "arbitrary")),
        name="expert_ffn",
    )(idx_flat, u_rows, w_g, w_u, w_d, gate_sel)


_COMBINE_UNROLL = 8


def _combine_kernel(cap, idx_ref, y_ref, f_ref):
    b = pl.program_id(0)
    e = pl.program_id(1)

    @pl.when(e == 0)
    def _():
        f_ref[...] = jnp.zeros_like(f_ref)

    base = (b * N_EXPERTS + e) * cap

    def chunk(ci, carry):
        i0 = ci * _COMBINE_UNROLL
        tiles = [_token_tile(idx_ref[base + i0 + j]) for j in range(_COMBINE_UNROLL)]
        sums = [f_ref[tiles[j], :] + y_ref[_token_tile(i0 + j), :] for j in range(_COMBINE_UNROLL)]
        for j in range(_COMBINE_UNROLL):
            f_ref[tiles[j], :] = sums[j]
        return carry

    lax.fori_loop(0, cap // _COMBINE_UNROLL, chunk, 0)


def _combine(idx_flat, y, seq, cap):
    batch = y.shape[0]
    grid_spec = pltpu.PrefetchScalarGridSpec(
        num_scalar_prefetch=1,
        grid=(batch, N_EXPERTS),
        in_specs=[pl.BlockSpec((None, None, cap * TOKEN_TILE_ROWS, LANES), lambda b, e, idx: (b, e, 0, 0))],
        out_specs=pl.BlockSpec((None, seq * TOKEN_TILE_ROWS, LANES), lambda b, e, idx: (b, 0, 0)),
    )
    return pl.pallas_call(
        functools.partial(_combine_kernel, cap),
        grid_spec=grid_spec,
        out_shape=jax.ShapeDtypeStruct((batch, seq * TOKEN_TILE_ROWS, LANES), F32),
        compiler_params=_cparams(("arbitrary", "arbitrary")),
        name="expert_combine",
    )(idx_flat, y)


def _moe(u, aff, layer, w_g, w_u, w_d):
    b, _, n = aff.shape
    cap = CAPACITY_FACTOR * n // N_EXPERTS
    packed = _select(aff, cap).reshape(b, N_EXPERTS, _PACKED_VALUES, cap)
    idx_flat = (packed[:, :, 0] * _TOKEN_SPLIT + packed[:, :, 1]).astype(I32).reshape(-1)
    gate_sel = (packed[:, :, 2] + packed[:, :, 3] + packed[:, :, 4])[..., None]
    y = _expert_ffn(idx_flat, u.reshape(b * n * TOKEN_TILE_ROWS, LANES), layer, w_g, w_u, w_d, gate_sel, b, n, cap)
    return _combine(idx_flat, y, n, cap)


def _l1_in_kernel(h_ref, f_ref, gf_ref, lng_ref, lnb_ref, sh_ref, sc_ref, w_ref, h2_ref, bg_ref, cx_ref):
    f = _load_token_major(f_ref, h_ref.shape[0])
    h2 = _layer_norm(DEEPNORM_ALPHA * h_ref[...] + gf_ref[...] * f, lng_ref[...], lnb_ref[...])
    h2_ref[...] = h2
    u = h2 * (1.0 + sc_ref[...]) + sh_ref[...]
    z = jnp.dot(u.astype(BF16), w_ref[...], preferred_element_type=F32)
    bg_ref[...] = z[:, :D_MODEL]
    cx_ref[...] = z[:, D_MODEL:2 * D_MODEL] * z[:, 2 * D_MODEL:]


def _l1_in_proj(h, f, gate_f, lng, lnb, shift, scale, w_bf, ts):
    b, s, _ = h.shape
    tok = lambda: pl.BlockSpec((None, ts, D_MODEL), lambda bi, si: (bi, si, 0))
    row = lambda: pl.BlockSpec((None, 1, D_MODEL), lambda bi, si: (bi, 0, 0))
    vec = lambda: pl.BlockSpec((1, D_MODEL), lambda bi, si: (0, 0))
    return pl.pallas_call(
        _l1_in_kernel,
        grid=(b, s // ts),
        in_specs=[tok(), pl.BlockSpec((None, ts * TOKEN_TILE_ROWS, LANES), lambda bi, si: (bi, si, 0)),
                  row(), vec(), vec(), row(), row(),
                  pl.BlockSpec((D_MODEL, 3 * D_MODEL), lambda bi, si: (0, 0))],
        out_specs=[tok(), tok(), tok()],
        out_shape=[jax.ShapeDtypeStruct((b, s, D_MODEL), F32)] * 3,
        compiler_params=_cparams(("arbitrary", "arbitrary")),
        name="l1_in_proj",
    )(h, f, gate_f, lng, lnb, shift, scale, w_bf)


def _l1_out_kernel(cx_ref, cprev_ref, cnext_ref, bg_ref, cw_ref, wout_ref,
                   x_ref, gate_ref, lng_ref, lnb_ref, shf_ref, scf_ref, wr_ref,
                   h_ref, u_ref, aff_ref, cext_ref):
    ts = cx_ref.shape[0]
    _fill_halo(cext_ref, cprev_ref, cx_ref, cnext_ref, ts)
    z = (cext_ref[HALO - 1:HALO - 1 + ts, :] * cw_ref[0:1, :]
         + cx_ref[...] * cw_ref[1:2, :]
         + cext_ref[HALO + 1:HALO + 1 + ts, :] * cw_ref[2:3, :])
    y = jnp.dot((bg_ref[...] * z).astype(BF16), wout_ref[...], preferred_element_type=F32)
    h, u, aff = _post_tail(x_ref[...], y, gate_ref[...], lng_ref[...], lnb_ref[...],
                           shf_ref[...], scf_ref[...], wr_ref[...])
    h_ref[...] = h
    _store_token_major(u_ref, u)
    aff_ref[...] = aff


def _l1_out_proj(cx, bg, conv_w, wout_bf, x, gate, lng, lnb, shf, scf, wr_t, ts):
    b, s, _ = x.shape
    tail_in, tail_out = _tail_specs(ts)
    in_specs = (
        _halo_specs(ts, D_MODEL, s)
        + [pl.BlockSpec((None, ts, D_MODEL), lambda bi, si: (bi, si, 0)),
           pl.BlockSpec(conv_w.shape, lambda bi, si: (0, 0)),
           pl.BlockSpec((D_MODEL, D_MODEL), lambda bi, si: (0, 0))]
        + tail_in)
    return pl.pallas_call(
        _l1_out_kernel,
        grid=(b, s // ts),
        in_specs=in_specs,
        out_specs=tail_out,
        out_shape=_tail_out_shape(b, s),
        scratch_shapes=[pltpu.VMEM((ts + 2 * HALO, D_MODEL), F32)],
        compiler_params=_cparams(("arbitrary", "arbitrary")),
        name="l1_out_proj",
    )(cx, cx, cx, bg, conv_w, wout_bf, x, gate, lng, lnb, shf, scf, wr_t)


def _final_kernel(h_ref, f_ref, gf_ref, lng_ref, lnb_ref, o_ref):
    f = _load_token_major(f_ref, h_ref.shape[0])
    o_ref[...] = _layer_norm(DEEPNORM_ALPHA * h_ref[...] + gf_ref[...] * f, lng_ref[...], lnb_ref[...])


def _final_norm(h, f, gate_f, lng, lnb, ts):
    b, s, _ = h.shape
    tok = lambda: pl.BlockSpec((None, ts, D_MODEL), lambda bi, si: (bi, si, 0))
    return pl.pallas_call(
        _final_kernel,
        grid=(b, s // ts),
        in_specs=[tok(), pl.BlockSpec((None, ts * TOKEN_TILE_ROWS, LANES), lambda bi, si: (bi, si, 0)),
                  pl.BlockSpec((None, 1, D_MODEL), lambda bi, si: (bi, 0, 0)),
                  pl.BlockSpec((1, D_MODEL), lambda bi, si: (0, 0)),
                  pl.BlockSpec((1, D_MODEL), lambda bi, si: (0, 0))],
        out_specs=tok(),
        out_shape=jax.ShapeDtypeStruct((b, s, D_MODEL), F32),
        compiler_params=_cparams(("arbitrary", "arbitrary")),
        name="final_norm",
    )(h, f, gate_f, lng, lnb)


def _rope_lane_tables(seq):
    rows = seq // GRID_W
    row = jnp.repeat(jnp.arange(rows, dtype=F32), GRID_W)
    col = jnp.tile(jnp.arange(GRID_W, dtype=F32), rows)
    axis_dim = HEAD_DIM // 2
    inv_freq = ROPE_THETA ** (-jnp.arange(0, axis_dim, 2, dtype=F32) / axis_dim)
    ang = jnp.concatenate([row[:, None] * inv_freq, col[:, None] * inv_freq], axis=-1)
    cos = jnp.repeat(jnp.cos(ang), 2, axis=-1)
    sin = jnp.repeat(jnp.sin(ang), 2, axis=-1)
    sign = jnp.tile(jnp.array([-1.0, 1.0], F32), HEAD_DIM // 2)
    return jnp.tile(cos, (1, N_Q_HEADS)), jnp.tile(sin * sign, (1, N_Q_HEADS))


def kernel(x, c, ctx, c_ctx, w_mod, b_mod, ln_mix_g, ln_mix_b, ln_ffn_g, ln_ffn_b, w_mix_in, q_norm_g, k_norm_g, w_pool_grp, pool_scale, w_mix_out, w_conv_in, conv_w, w_conv_out, w_router, w_exp_gate, w_exp_up, w_exp_down):
    b, s, d = x.shape
    lc = ctx.shape[1]
    ts = 512

    cond_rows = jnp.zeros((SUBLANES, d), F32).at[:b].set(c).at[b].set(c_ctx)
    mods = _modulation(cond_rows, w_mod, b_mod).reshape(DEPTH, SUBLANES, 6, d)
    m_lat = lambda layer, k: mods[layer, :b, k][:, None, :]
    m_ctx = lambda layer, k: mods[layer, b:b + 1, k][:, None, :]
    vec = lambda t: t.reshape(1, -1)

    w_in_bf = w_mix_in[0].astype(BF16)
    grp = lax.broadcasted_iota(I32, (ATTN_WIDTH, ATTN_WIDTH), 0) // HEAD_DIM
    bd = jnp.where(grp == grp.T, 1.0 / HEAD_DIM, 0.0).astype(BF16)
    qg = jnp.tile(q_norm_g[0], N_Q_HEADS).reshape(1, -1)
    kg = jnp.tile(k_norm_g[0], N_KV_HEADS).reshape(1, -1)
    cos_t, sin_t = _rope_lane_tables(s)
    q, k, v, p = _l0_in_proj(x, m_lat(0, 0), m_lat(0, 1), w_in_bf, bd, qg, kg, cos_t, sin_t, ts)
    ones = jnp.ones((lc, ATTN_WIDTH), F32)
    _, kc, vc, _ = _l0_in_proj(ctx, m_ctx(0, 0), m_ctx(0, 1), w_in_bf, bd, qg, kg, ones, 0.0 * ones, lc)
    k_all = jnp.concatenate([kc, k], axis=1)
    v_t = jnp.swapaxes(jnp.concatenate([vc, v], axis=1), 1, 2).reshape(b, N_KV_HEADS, HEAD_DIM, lc + s)
    v_t = jnp.concatenate([v_t, jnp.ones((b, N_KV_HEADS, _VT_ROWS - HEAD_DIM, lc + s), BF16)], axis=2)
    score_bound = (1.02 * HEAD_DIM ** 0.5 * LOG2_E) * jnp.max(jnp.abs(q_norm_g[0])) * jnp.max(jnp.abs(k_norm_g[0]))
    a = _attention(q, k_all, v_t, score_bound, 512)
    h, u, aff = _l0_out_proj(a, p, w_pool_grp[0].astype(BF16), vec(pool_scale[0]), w_mix_out[0].astype(BF16),
                             x, m_lat(0, 2), vec(ln_mix_g[0]), vec(ln_mix_b[0]), m_lat(0, 3), m_lat(0, 4),
                             _router_weights(w_router[0]), ts)
    f = _moe(u, aff, 0, w_exp_gate, w_exp_up, w_exp_down)

    h, bg, cx = _l1_in_proj(h, f, m_lat(0, 5), vec(ln_ffn_g[0]), vec(ln_ffn_b[0]), m_lat(1, 0), m_lat(1, 1),
                            w_conv_in[0].astype(BF16), ts)
    h, u, aff = _l1_out_proj(cx, bg, conv_w[0], w_conv_out[0].astype(BF16),
                             h, m_lat(1, 2), vec(ln_mix_g[1]), vec(ln_mix_b[1]), m_lat(1, 3), m_lat(1, 4),
                             _router_weights(w_router[1]), ts)
    f = _moe(u, aff, 1, w_exp_gate, w_exp_up, w_exp_down)
    return _final_norm(h, f, m_lat(1, 5), vec(ln_ffn_g[1]), vec(ln_ffn_b[1]), ts)
```

```python
import functools

import jax
import jax.numpy as jnp
from jax import lax
from jax.experimental import pallas as pl
from jax.experimental.pallas import tpu as pltpu

F32, BF16, I32 = jnp.float32, jnp.bfloat16, jnp.int32

D_MODEL = 1024
DEPTH = 2
GRID_W = 64
N_Q_HEADS = 8
N_KV_HEADS = 2
HEAD_DIM = 64
ATTN_WIDTH = N_Q_HEADS * HEAD_DIM
KV_WIDTH = N_KV_HEADS * HEAD_DIM
ROPE_THETA = 10000.0
POOL_WINDOWS = (2, 4, 8, 16)
POOL_GROUP_DIM = 128
POOL_WIDTH = len(POOL_WINDOWS) * POOL_GROUP_DIM
MIX_IN_WIDTH = ATTN_WIDTH + 2 * KV_WIDTH + POOL_WIDTH
N_EXPERTS = 16
EXPERT_FF = 1024
CAPACITY_FACTOR = 2
NORM_EPS = 1e-6
DEEPNORM_ALPHA = (2 * DEPTH) ** 0.25
LOG2_E = 1.4426950408889634

LANES = 128
SUBLANES = 8
HALO = SUBLANES
VMEM_LIMIT = 56 * 1024 * 1024

_NT = (((1,), (1,)), ((), ()))


def _cparams(sem):
    return pltpu.CompilerParams(dimension_semantics=sem, vmem_limit_bytes=VMEM_LIMIT)


TOKEN_TILE_ROWS = D_MODEL // LANES


def _store_token_major(ref, rows):
    n = rows.shape[0]
    for j in range(TOKEN_TILE_ROWS):
        ref[pl.ds(j, n, stride=TOKEN_TILE_ROWS), :] = rows[:, j * LANES:(j + 1) * LANES]


def _load_token_major(ref, n):
    return jnp.concatenate(
        [ref[pl.ds(j, n, stride=TOKEN_TILE_ROWS), :] for j in range(TOKEN_TILE_ROWS)], axis=1)


def _layer_norm(z, g, b):
    mu = jnp.mean(z, axis=-1, keepdims=True)
    zc = z - mu
    var = jnp.mean(zc * zc, axis=-1, keepdims=True)
    return zc * lax.rsqrt(var + NORM_EPS) * g + b


def _split_bf16(t):
    hi = t.astype(BF16)
    return hi, (t - hi.astype(F32)).astype(BF16)


def _mod_kernel(c_ref, w_ref, b_ref, o_ref):
    c = c_ref[...]
    c_hi, c_lo = _split_bf16(c * jax.nn.sigmoid(c))
    w_hi, w_lo = _split_bf16(w_ref[...])
    dot = functools.partial(jnp.dot, preferred_element_type=F32)
    o_ref[...] = dot(c_hi, w_hi) + dot(c_lo, w_hi) + dot(c_hi, w_lo) + b_ref[...]


def _modulation(cond_rows, w_mod, b_mod):
    n_out = w_mod.shape[-1]
    tn = D_MODEL
    return pl.pallas_call(
        _mod_kernel,
        grid=(DEPTH, n_out // tn),
        in_specs=[
            pl.BlockSpec((SUBLANES, D_MODEL), lambda l, j: (0, 0)),
            pl.BlockSpec((None, D_MODEL, tn), lambda l, j: (l, 0, j)),
            pl.BlockSpec((None, 1, tn), lambda l, j: (l, 0, j)),
        ],
        out_specs=pl.BlockSpec((None, SUBLANES, tn), lambda l, j: (l, 0, j)),
        out_shape=jax.ShapeDtypeStruct((DEPTH, SUBLANES, n_out), F32),
        compiler_params=_cparams(("arbitrary", "arbitrary")),
        name="modulation",
    )(cond_rows, w_mod, b_mod.reshape(DEPTH, 1, n_out))


def _group_mean_sq(t, bd):
    hi, lo = _split_bf16(t * t)
    return (jnp.dot(hi, bd, preferred_element_type=F32) + jnp.dot(lo, bd, preferred_element_type=F32))


def _rope_chunk(y, c, s_signed):
    n = y.shape[-1]
    lane = lax.broadcasted_iota(I32, y.shape, 1)
    nxt = pltpu.roll(y, n - 1, axis=1)
    prv = pltpu.roll(y, 1, axis=1)
    partner = jnp.where((lane & 1) == 0, nxt, prv)
    return y * c + partner * s_signed


def _l0_in_kernel(x_ref, sh_ref, sc_ref, w_ref, bd_ref, qg_ref, kg_ref, cos_ref, sin_ref,
                  q_ref, k_ref, v_ref, p_ref):
    u = x_ref[...] * (1.0 + sc_ref[...]) + sh_ref[...]
    h = jnp.dot(u.astype(BF16), w_ref[...], preferred_element_type=F32)
    q = h[:, :ATTN_WIDTH]
    k = h[:, ATTN_WIDTH:ATTN_WIDTH + KV_WIDTH]
    bd = bd_ref[...]
    qn = q * lax.rsqrt(_group_mean_sq(q, bd) + NORM_EPS) * qg_ref[...]
    kn = k * lax.rsqrt(_group_mean_sq(k, bd[:KV_WIDTH, :KV_WIDTH]) + NORM_EPS) * kg_ref[...]
    scale = HEAD_DIM ** -0.5 * LOG2_E
    heads_per_kv = N_Q_HEADS // N_KV_HEADS
    lane = lax.broadcasted_iota(I32, (x_ref.shape[0], LANES), 1)
    for c in range(ATTN_WIDTH // LANES):
        sl = slice(c * LANES, (c + 1) * LANES)
        pair = _rope_chunk(qn[:, sl], cos_ref[:, sl], sin_ref[:, sl]) * scale
        swapped = pltpu.roll(pair, HEAD_DIM, axis=1)
        g = (2 * c) // heads_per_kv
        keep = (lane < HEAD_DIM) if g == 0 else (lane >= HEAD_DIM)
        first, second = (pair, swapped) if g == 0 else (swapped, pair)
        q_ref[:, (2 * c) * LANES:(2 * c + 1) * LANES] = jnp.where(keep, first, 0.0).astype(BF16)
        q_ref[:, (2 * c + 1) * LANES:(2 * c + 2) * LANES] = jnp.where(keep, second, 0.0).astype(BF16)
    k_ref[...] = _rope_chunk(kn, cos_ref[:, :KV_WIDTH], sin_ref[:, :KV_WIDTH]).astype(BF16)
    v_ref[...] = h[:, ATTN_WIDTH + KV_WIDTH:ATTN_WIDTH + 2 * KV_WIDTH].astype(BF16)
    p_ref[...] = h[:, ATTN_WIDTH + 2 * KV_WIDTH:]


def _l0_in_proj(x, shift, scale, w_in_bf, bd, qg, kg, cos_t, sin_t, ts):
    b, l, _ = x.shape
    per_batch = shift.shape[0] == b
    mod_map = (lambda bi, si: (bi, 0, 0)) if per_batch else (lambda bi, si: (0, 0, 0))
    full = lambda bi, si: (0, 0)
    return pl.pallas_call(
        _l0_in_kernel,
        grid=(b, l // ts),
        in_specs=[
            pl.BlockSpec((None, ts, D_MODEL), lambda bi, si: (bi, si, 0)),
            pl.BlockSpec((None, 1, D_MODEL), mod_map),
            pl.BlockSpec((None, 1, D_MODEL), mod_map),
            pl.BlockSpec((D_MODEL, MIX_IN_WIDTH), full),
            pl.BlockSpec((ATTN_WIDTH, ATTN_WIDTH), full),
            pl.BlockSpec((1, ATTN_WIDTH), full),
            pl.BlockSpec((1, KV_WIDTH), full),
            pl.BlockSpec((ts, ATTN_WIDTH), lambda bi, si: (si, 0)),
            pl.BlockSpec((ts, ATTN_WIDTH), lambda bi, si: (si, 0)),
        ],
        out_specs=[
            pl.BlockSpec((None, ts, N_Q_HEADS * LANES), lambda bi, si: (bi, si, 0)),
            pl.BlockSpec((None, ts, KV_WIDTH), lambda bi, si: (bi, si, 0)),
            pl.BlockSpec((None, ts, KV_WIDTH), lambda bi, si: (bi, si, 0)),
            pl.BlockSpec((None, ts, POOL_WIDTH), lambda bi, si: (bi, si, 0)),
        ],
        out_shape=[
            jax.ShapeDtypeStruct((b, l, N_Q_HEADS * LANES), BF16),
            jax.ShapeDtypeStruct((b, l, KV_WIDTH), BF16),
            jax.ShapeDtypeStruct((b, l, KV_WIDTH), BF16),
            jax.ShapeDtypeStruct((b, l, POOL_WIDTH), F32),
        ],
        compiler_params=_cparams(("arbitrary", "arbitrary")),
        name="l0_in_proj",
    )(x, shift, scale, w_in_bf, bd, qg, kg, cos_t, sin_t)


_VT_ROWS = HEAD_DIM + 16


_KV_CHUNK = 2176
_MAX_ROWS = 64


def _attn_kernel(q_ref, k_ref, vt_ref, o_ref, ot_ref, *s_refs):
    heads_per_kv = N_Q_HEADS // N_KV_HEADS
    lk, tq = k_ref.shape[0], q_ref.shape[0]
    chunks = [(c0, min(_KV_CHUNK, lk - c0)) for c0 in range(0, lk, _KV_CHUNK)]

    def scores(h, c0, n, m):
        qh = q_ref[:, h * LANES:(h + 1) * LANES]
        s_c = lax.dot_general(k_ref[c0:c0 + n, :], qh, _NT, preferred_element_type=F32)
        s_refs[h % 2][c0:c0 + n, :] = s_c
        mc = jnp.max(s_c.reshape(n // _MAX_ROWS, _MAX_ROWS, tq), axis=0)
        return mc if m is None else jnp.maximum(m, mc)

    def weighted(h, c0, n, m, o_t):
        p_c = jnp.exp2(s_refs[h % 2][c0:c0 + n, :] - m).astype(BF16)
        o_c = jnp.dot(vt_ref[h // heads_per_kv, :, c0:c0 + n], p_c, preferred_element_type=F32)
        return o_c if o_t is None else o_t + o_c

    m_next = None
    for c0, n in chunks:
        m_next = scores(0, c0, n, m_next)
    for h in range(N_Q_HEADS):
        g = h // heads_per_kv
        m = jnp.max(m_next, axis=0, keepdims=True)
        m_next = None
        o_t = None
        for c0, n in chunks:
            o_t = weighted(h, c0, n, m, o_t)
            if h + 1 < N_Q_HEADS:
                m_next = scores(h + 1, c0, n, m_next)
        ot_ref[h * HEAD_DIM:(h + 1) * HEAD_DIM, :] = (
            o_t[:HEAD_DIM, :] / o_t[HEAD_DIM:HEAD_DIM + 1, :])
    o_ref[...] = ot_ref[...].T.astype(BF16)


_BOUNDED_KV_CHUNK = 1088


def _attn_bounded_kernel(bound_ref, q_ref, k_ref, vt_ref, o_ref, ot_ref):
    heads_per_kv = N_Q_HEADS // N_KV_HEADS
    lk = k_ref.shape[0]
    shift = bound_ref[0]
    for h in range(N_Q_HEADS):
        g = h // heads_per_kv
        qh = q_ref[:, h * LANES:(h + 1) * LANES]
        o_t = None
        for c0 in range(0, lk, _BOUNDED_KV_CHUNK):
            n = min(_BOUNDED_KV_CHUNK, lk - c0)
            s_c = lax.dot_general(k_ref[c0:c0 + n, :], qh, _NT, preferred_element_type=F32)
            p_c = jnp.exp2(s_c - shift).astype(BF16)
            o_c = jnp.dot(vt_ref[g, :, c0:c0 + n], p_c, preferred_element_type=F32)
            o_t = o_c if o_t is None else o_t + o_c
        ot_ref[h * HEAD_DIM:(h + 1) * HEAD_DIM, :] = (
            o_t[:HEAD_DIM, :] / o_t[HEAD_DIM:HEAD_DIM + 1, :])
    o_ref[...] = ot_ref[...].T.astype(BF16)


_MAX_SAFE_SCORE_BOUND = 40.0


def _attention(q, k_all, vt, score_bound, tq):
    b, s, qw = q.shape
    lk = k_all.shape[1]
    specs = dict(
        grid=(b, s // tq),
        out_specs=pl.BlockSpec((None, tq, ATTN_WIDTH), lambda bi, qi: (bi, qi, 0)),
        out_shape=jax.ShapeDtypeStruct((b, s, ATTN_WIDTH), BF16),
        compiler_params=_cparams(("arbitrary", "arbitrary")),
    )
    in_specs = [
        pl.BlockSpec((None, tq, qw), lambda bi, qi: (bi, qi, 0)),
        pl.BlockSpec((None, lk, KV_WIDTH), lambda bi, qi: (bi, 0, 0)),
        pl.BlockSpec((None, N_KV_HEADS, _VT_ROWS, lk), lambda bi, qi: (bi, 0, 0, 0)),
    ]

    def bounded(_):
        return pl.pallas_call(
            _attn_bounded_kernel,
            in_specs=[pl.BlockSpec(memory_space=pltpu.SMEM)] + in_specs,
            scratch_shapes=[pltpu.VMEM((ATTN_WIDTH, tq), F32)],
            name="attention_bounded", **specs,
        )(score_bound.reshape(1), q, k_all, vt)

    def exact(_):
        return pl.pallas_call(
            _attn_kernel,
            in_specs=in_specs,
            scratch_shapes=[pltpu.VMEM((ATTN_WIDTH, tq), F32), pltpu.VMEM((lk, tq), F32),
                            pltpu.VMEM((lk, tq), F32)],
            name="attention", **specs,
        )(q, k_all, vt)

    return lax.cond(score_bound < _MAX_SAFE_SCORE_BOUND, bounded, exact, None)


def _router_weights(w_router_layer):
    hi, lo = _split_bf16(w_router_layer.T)
    return jnp.concatenate([hi, lo], axis=0)


def _post_tail(resid, y, gate, lng, lnb, shf, scf, wr2):
    h = _layer_norm(DEEPNORM_ALPHA * resid + gate * y, lng, lnb)
    u = h * (1.0 + scf) + shf
    u_hi, u_lo = _split_bf16(u)
    first = lax.dot_general(wr2, u_hi, _NT, preferred_element_type=F32)
    second = lax.dot_general(wr2[:N_EXPERTS], u_lo, _NT, preferred_element_type=F32)
    logits = first[:N_EXPERTS] + first[N_EXPERTS:] + second
    ex = jnp.exp(logits - jnp.max(logits, axis=0, keepdims=True))
    aff = ex / jnp.sum(ex, axis=0, keepdims=True)
    return h, u, aff


def _fill_halo(ext_ref, prev_ref, main_ref, next_ref, ts):
    st = pl.program_id(1)
    last = pl.num_programs(1) - 1
    ext_ref[0:HALO, :] = jnp.where(st > 0, prev_ref[...], 0.0)
    ext_ref[HALO:HALO + ts, :] = main_ref[...]
    ext_ref[HALO + ts:HALO + ts + HALO, :] = jnp.where(st < last, next_ref[...], 0.0)


def _halo_specs(ts, width, seq):
    r = ts // HALO
    nblk = seq // HALO
    return [
        pl.BlockSpec((None, ts, width), lambda bi, si: (bi, si, 0)),
        pl.BlockSpec((None, HALO, width), lambda bi, si: (bi, jnp.maximum(si * r - 1, 0), 0)),
        pl.BlockSpec((None, HALO, width), lambda bi, si: (bi, jnp.minimum((si + 1) * r, nblk - 1), 0)),
    ]


def _tail_specs(ts):
    row = lambda bi, si: (bi, 0, 0)
    full = lambda bi, si: (0, 0)
    in_specs = [
        pl.BlockSpec((None, ts, D_MODEL), lambda bi, si: (bi, si, 0)),
        pl.BlockSpec((None, 1, D_MODEL), row),
        pl.BlockSpec((1, D_MODEL), full),
        pl.BlockSpec((1, D_MODEL), full),
        pl.BlockSpec((None, 1, D_MODEL), row),
        pl.BlockSpec((None, 1, D_MODEL), row),
        pl.BlockSpec((2 * N_EXPERTS, D_MODEL), full),
    ]
    out_specs = [
        pl.BlockSpec((None, ts, D_MODEL), lambda bi, si: (bi, si, 0)),
        pl.BlockSpec((None, ts * TOKEN_TILE_ROWS, LANES), lambda bi, si: (bi, si, 0)),
        pl.BlockSpec((None, N_EXPERTS, ts), lambda bi, si: (bi, 0, si)),
    ]
    return in_specs, out_specs


def _tail_out_shape(b, s):
    return [jax.ShapeDtypeStruct((b, s, D_MODEL), F32),
            jax.ShapeDtypeStruct((b, s * TOKEN_TILE_ROWS, LANES), F32),
            jax.ShapeDtypeStruct((b, N_EXPERTS, s), F32)]


def _l0_out_kernel(seq, a_ref, p_ref, pprev_ref, pnext_ref, wpool_ref, pscale_ref, wout_ref,
                   x_ref, gate_ref, lng_ref, lnb_ref, shf_ref, scf_ref, wr_ref,
                   h_ref, u_ref, aff_ref, pext_ref):
    ts = p_ref.shape[0]
    _fill_halo(pext_ref, pprev_ref, p_ref, pnext_ref, ts)
    t = pl.program_id(1) * ts + lax.broadcasted_iota(I32, (ts, 1), 0)
    outs = []
    for g, w in enumerate(POOL_WINDOWS):
        sl = slice(g * POOL_GROUP_DIM, (g + 1) * POOL_GROUP_DIM)
        fwd = pext_ref[:, sl]
        n_ext = fwd.shape[0]
        span = 1
        while span < w // 2:
            fwd = fwd + pltpu.roll(fwd, n_ext - span, axis=0)
            span *= 2
        acc = (fwd + pltpu.roll(fwd, w // 2, axis=0))[HALO:HALO + ts, :]
        cnt = jnp.minimum(t + w // 2, seq) - jnp.maximum(t - w // 2, 0)
        pooled = acc / cnt.astype(F32) - p_ref[:, sl]
        og = jnp.dot(pooled.astype(BF16), wpool_ref[g], preferred_element_type=F32)
        outs.append((og * pscale_ref[:, sl]).astype(BF16))
    pool = jnp.concatenate(outs, axis=-1)
    y = (jnp.dot(a_ref[...], wout_ref[:ATTN_WIDTH, :], preferred_element_type=F32)
         + jnp.dot(pool, wout_ref[ATTN_WIDTH:, :], preferred_element_type=F32))
    h, u, aff = _post_tail(x_ref[...], y, gate_ref[...], lng_ref[...], lnb_ref[...],
                           shf_ref[...], scf_ref[...], wr_ref[...])
    h_ref[...] = h
    _store_token_major(u_ref, u)
    aff_ref[...] = aff


def _l0_out_proj(a, p, wpool_bf, pscale, wout_bf, x, gate, lng, lnb, shf, scf, wr_t, ts):
    b, s, _ = x.shape
    full2 = lambda bi, si: (0, 0)
    tail_in, tail_out = _tail_specs(ts)
    in_specs = (
        [pl.BlockSpec((None, ts, ATTN_WIDTH), lambda bi, si: (bi, si, 0))]
        + _halo_specs(ts, POOL_WIDTH, s)
        + [pl.BlockSpec((len(POOL_WINDOWS), POOL_GROUP_DIM, POOL_GROUP_DIM), lambda bi, si: (0, 0, 0)),
           pl.BlockSpec((1, POOL_WIDTH), full2),
           pl.BlockSpec((ATTN_WIDTH + POOL_WIDTH, D_MODEL), full2)]
        + tail_in)
    return pl.pallas_call(
        functools.partial(_l0_out_kernel, s),
        grid=(b, s // ts),
        in_specs=in_specs,
        out_specs=tail_out,
        out_shape=_tail_out_shape(b, s),
        scratch_shapes=[pltpu.VMEM((ts + 2 * HALO, POOL_WIDTH), F32)],
        compiler_params=_cparams(("arbitrary", "arbitrary")),
        name="l0_out_proj",
    )(a, p, p, p, wpool_bf, pscale, wout_bf, x, gate, lng, lnb, shf, scf, wr_t)


def _exclusive_cumsum_lanes(mask, tri):
    n = mask.shape[1]
    w = tri.shape[0]
    carry = jnp.zeros((mask.shape[0], 1), F32)
    parts = []
    for c in range(n // w):
        blk = mask[:, c * w:(c + 1) * w]
        inc = jnp.dot(blk.astype(BF16), tri, preferred_element_type=F32)
        parts.append(inc - blk + carry)
        carry = carry + inc[:, w - 1:w]
    return jnp.concatenate(parts, axis=1)


_TOKEN_SPLIT = 64
_SLOT_LO = 32
_PACKED_VALUES = 5


def _select_kernel(cap, aff_ref, out_ref):
    aff = aff_ref[...]
    e = aff.shape[0]
    as_f32 = lambda word: pltpu.bitcast(word, F32)

    def refine(i, thr):
        cand = thr | jnp.left_shift(jnp.int32(1), 30 - i)
        cnt = jnp.sum(jnp.where(aff >= as_f32(cand), 1.0, 0.0), axis=1, keepdims=True)
        return jnp.where(cnt >= cap, cand, thr)

    thr = lax.fori_loop(0, 31, refine, jnp.zeros((e, 1), I32))
    w = 2 * LANES
    r = lax.broadcasted_iota(I32, (w, w), 0)
    c = lax.broadcasted_iota(I32, (w, w), 1)
    tri = jnp.where(r <= c, 1.0, 0.0).astype(BF16)
    gt = jnp.where(aff >= as_f32(thr + 1), 1.0, 0.0)
    eq = jnp.where(aff >= as_f32(thr), 1.0, 0.0) - gt
    need = cap - jnp.sum(gt, axis=1, keepdims=True)
    sel = gt + eq * jnp.where(_exclusive_cumsum_lanes(eq, tri) < need, 1.0, 0.0)
    pos = _exclusive_cumsum_lanes(sel, tri)
    dest = jnp.where(sel > 0.5, pos, float(cap)).astype(I32)

    n = aff.shape[1]
    n_hi = cap // _SLOT_LO
    d_hi = dest // _SLOT_LO
    d_lo = dest % _SLOT_LO
    t = lax.broadcasted_iota(I32, (1, n), 1)
    t_hi = (t // _TOKEN_SPLIT).astype(F32)
    t_lo = (t % _TOKEN_SPLIT).astype(F32)
    a1 = aff.astype(BF16).astype(F32)
    r1 = aff - a1
    a2 = r1.astype(BF16).astype(F32)
    a3 = r1 - a2
    row_hi = lax.broadcasted_iota(I32, (n_hi, n), 0)
    row_lo = lax.broadcasted_iota(I32, (_SLOT_LO, n), 0)
    for ei in range(e):
        rs = slice(ei, ei + 1)
        h = jnp.where(d_hi[rs] == row_hi, 1.0, 0.0)
        l = jnp.where(d_lo[rs] == row_lo, 1.0, 0.0).astype(BF16)
        vals = jnp.concatenate([h * t_hi, h * t_lo, h * a1[rs], h * a2[rs], h * a3[rs]], axis=0)
        out_ref[ei] = lax.dot_general(vals.astype(BF16), l, _NT, preferred_element_type=F32)


def _select(aff, cap):
    b, e, n = aff.shape
    rows = _PACKED_VALUES * (cap // _SLOT_LO)
    return pl.pallas_call(
        functools.partial(_select_kernel, cap),
        grid=(b,),
        in_specs=[pl.BlockSpec((None, e, n), lambda bi: (bi, 0, 0))],
        out_specs=pl.BlockSpec((None, e, rows, _SLOT_LO), lambda bi: (bi, 0, 0, 0)),
        out_shape=jax.ShapeDtypeStruct((b, e, rows, _SLOT_LO), F32),
        compiler_params=_cparams(("arbitrary",)),
        name="expert_select",
    )(aff)


def _token_tile(t):
    return pl.ds(pl.multiple_of(t * TOKEN_TILE_ROWS, TOKEN_TILE_ROWS), TOKEN_TILE_ROWS)


_GATHER_AHEAD = 2
_FF_BLOCK = 512


def _expert_kernel(cap, seq, idx_ref, u_hbm, wg_ref, wu_ref, wd_ref, y_ref,
                   xe_ref, wg_bf, wu_bf, wd_bf, sem):
    b = pl.program_id(1)
    nb = pl.num_programs(1)
    step = pl.program_id(0) * nb + b
    nsteps = pl.num_programs(0) * nb
    nbuf = _GATHER_AHEAD + 1
    slot = step % nbuf

    def start_gather(k, into, rolled):
        ek = k // nb
        bk = k % nb
        base = (bk * N_EXPERTS + ek) * cap
        tok0 = bk * seq

        def start(i, priority):
            tok = tok0 + idx_ref[base + i]
            pltpu.make_async_copy(u_hbm.at[_token_tile(tok), :], xe_ref.at[into, _token_tile(i), :],
                                  sem.at[into]).start(priority=priority)

        if rolled:
            def issue(ci, carry):
                for j in range(SUBLANES):
                    start(ci * SUBLANES + j, j % 2)
                return carry

            lax.fori_loop(0, cap // SUBLANES, issue, 0)
        else:
            for i in range(cap):
                start(i, i % 2)

    def wait_gather(into):
        pltpu.make_async_copy(u_hbm.at[pl.ds(0, cap * TOKEN_TILE_ROWS), :], xe_ref.at[into], sem.at[into]).wait()

    @pl.when(step == 0)
    def _():
        for k in range(_GATHER_AHEAD):
            start_gather(k, k, rolled=True)

    @pl.when(b == 0)
    def _():
        wg_bf[...] = wg_ref[...].astype(BF16)
        wu_bf[...] = wu_ref[...].astype(BF16)
        wd_bf[...] = wd_ref[...].astype(BF16)

    wait_gather(slot)
    x = _load_token_major(xe_ref.at[slot], cap).astype(BF16)
    start_gather(jnp.minimum(step + _GATHER_AHEAD, nsteps - 1), (step + _GATHER_AHEAD) % nbuf, rolled=False)
    blocks = [slice(f0, f0 + _FF_BLOCK) for f0 in range(0, EXPERT_FF, _FF_BLOCK)]
    pre = [(jnp.dot(x, wg_bf[:, fs], preferred_element_type=F32),
            jnp.dot(x, wu_bf[:, fs], preferred_element_type=F32)) for fs in blocks]
    y = None
    for fs, (hg, hu) in zip(blocks, pre):
        act = (hg * jax.nn.sigmoid(hg) * hu).astype(BF16)
        part = jnp.dot(act, wd_bf[fs, :], preferred_element_type=F32)
        y = part if y is None else y + part
    _store_token_major(y_ref, y)

    @pl.when(step == nsteps - 1)
    def _():
        for k in range(1, nbuf):
            wait_gather((step + k) % nbuf)


def _expert_ffn(idx_flat, u_rows, layer, w_g, w_u, w_d, batch, seq, cap):
    wspec = lambda: pl.BlockSpec((None, None, D_MODEL, EXPERT_FF), lambda e, b, idx: (layer, e, 0, 0))
    grid_spec = pltpu.PrefetchScalarGridSpec(
        num_scalar_prefetch=1,
        grid=(N_EXPERTS, batch),
        in_specs=[
            pl.BlockSpec(memory_space=pl.ANY),
            wspec(), wspec(),
            pl.BlockSpec((None, None, EXPERT_FF, D_MODEL), lambda e, b, idx: (layer, e, 0, 0)),
        ],
        out_specs=pl.BlockSpec((None, None, cap * TOKEN_TILE_ROWS, LANES), lambda e, b, idx: (b, e, 0, 0)),
        scratch_shapes=[
            pltpu.VMEM((_GATHER_AHEAD + 1, cap * TOKEN_TILE_ROWS, LANES), F32),
            pltpu.VMEM((D_MODEL, EXPERT_FF), BF16),
            pltpu.VMEM((D_MODEL, EXPERT_FF), BF16),
            pltpu.VMEM((EXPERT_FF, D_MODEL), BF16),
            pltpu.SemaphoreType.DMA((_GATHER_AHEAD + 1,)),
        ],
    )
    return pl.pallas_call(
        functools.partial(_expert_kernel, cap, seq),
        grid_spec=grid_spec,
        out_shape=jax.ShapeDtypeStruct((batch, N_EXPERTS, cap * TOKEN_TILE_ROWS, LANES), F32),
        compiler_params=_cparams(("arbitrary", "arbitrary")),
        name="expert_ffn",
    )(idx_flat, u_rows, w_g, w_u, w_d)


_COMBINE_UNROLL = 16


def _combine_kernel(cap, idx_ref, gate_ref, y_ref, f_ref):
    b = pl.program_id(0)
    e = pl.program_id(1)

    @pl.when(e == 0)
    def _():
        f_ref[...] = jnp.zeros_like(f_ref)

    base = (b * N_EXPERTS + e) * cap

    def chunk(ci, carry):
        i0 = ci * _COMBINE_UNROLL
        tiles = [_token_tile(idx_ref[base + i0 + j]) for j in range(_COMBINE_UNROLL)]
        sums = [f_ref[tiles[j], :] + y_ref[_token_tile(i0 + j), :] * gate_ref[base + i0 + j]
                for j in range(_COMBINE_UNROLL)]
        for j in range(_COMBINE_UNROLL):
            f_ref[tiles[j], :] = sums[j]
        return carry

    lax.fori_loop(0, cap // _COMBINE_UNROLL, chunk, 0)


def _combine(idx_flat, gate_flat, y, seq, cap):
    batch = y.shape[0]
    grid_spec = pltpu.PrefetchScalarGridSpec(
        num_scalar_prefetch=2,
        grid=(batch, N_EXPERTS),
        in_specs=[pl.BlockSpec((None, None, cap * TOKEN_TILE_ROWS, LANES), lambda b, e, idx, gate: (b, e, 0, 0))],
        out_specs=pl.BlockSpec((None, seq * TOKEN_TILE_ROWS, LANES), lambda b, e, idx, gate: (b, 0, 0)),
    )
    return pl.pallas_call(
        functools.partial(_combine_kernel, cap),
        grid_spec=grid_spec,
        out_shape=jax.ShapeDtypeStruct((batch, seq * TOKEN_TILE_ROWS, LANES), F32),
        compiler_params=_cparams(("arbitrary", "arbitrary")),
        name="expert_combine",
    )(idx_flat, gate_flat, y)


def _moe(u, aff, layer, w_g, w_u, w_d):
    b, _, n = aff.shape
    cap = CAPACITY_FACTOR * n // N_EXPERTS
    packed = _select(aff, cap).reshape(b, N_EXPERTS, _PACKED_VALUES, cap)
    idx_flat = (packed[:, :, 0] * _TOKEN_SPLIT + packed[:, :, 1]).astype(I32).reshape(-1)
    gate_flat = (packed[:, :, 2] + packed[:, :, 3] + packed[:, :, 4]).reshape(-1)
    y = _expert_ffn(idx_flat, u.reshape(b * n * TOKEN_TILE_ROWS, LANES), layer, w_g, w_u, w_d, b, n, cap)
    return _combine(idx_flat, gate_flat, y, n, cap)


def _l1_in_kernel(h_ref, f_ref, gf_ref, lng_ref, lnb_ref, sh_ref, sc_ref, w_ref, h2_ref, bg_ref, cx_ref):
    f = _load_token_major(f_ref, h_ref.shape[0])
    h2 = _layer_norm(DEEPNORM_ALPHA * h_ref[...] + gf_ref[...] * f, lng_ref[...], lnb_ref[...])
    h2_ref[...] = h2
    u = h2 * (1.0 + sc_ref[...]) + sh_ref[...]
    z = jnp.dot(u.astype(BF16), w_ref[...], preferred_element_type=F32)
    bg_ref[...] = z[:, :D_MODEL]
    cx_ref[...] = z[:, D_MODEL:2 * D_MODEL] * z[:, 2 * D_MODEL:]


def _l1_in_proj(h, f, gate_f, lng, lnb, shift, scale, w_bf, ts):
    b, s, _ = h.shape
    tok = lambda: pl.BlockSpec((None, ts, D_MODEL), lambda bi, si: (bi, si, 0))
    row = lambda: pl.BlockSpec((None, 1, D_MODEL), lambda bi, si: (bi, 0, 0))
    vec = lambda: pl.BlockSpec((1, D_MODEL), lambda bi, si: (0, 0))
    return pl.pallas_call(
        _l1_in_kernel,
        grid=(b, s // ts),
        in_specs=[tok(), pl.BlockSpec((None, ts * TOKEN_TILE_ROWS, LANES), lambda bi, si: (bi, si, 0)),
                  row(), vec(), vec(), row(), row(),
                  pl.BlockSpec((D_MODEL, 3 * D_MODEL), lambda bi, si: (0, 0))],
        out_specs=[tok(), tok(), tok()],
        out_shape=[jax.ShapeDtypeStruct((b, s, D_MODEL), F32)] * 3,
        compiler_params=_cparams(("arbitrary", "arbitrary")),
        name="l1_in_proj",
    )(h, f, gate_f, lng, lnb, shift, scale, w_bf)


def _l1_out_kernel(cx_ref, cprev_ref, cnext_ref, bg_ref, cw_ref, wout_ref,
                   x_ref, gate_ref, lng_ref, lnb_ref, shf_ref, scf_ref, wr_ref,
                   h_ref, u_ref, aff_ref, cext_ref):
    ts = cx_ref.shape[0]
    _fill_halo(cext_ref, cprev_ref, cx_ref, cnext_ref, ts)
    z = (cext_ref[HALO - 1:HALO - 1 + ts, :] * cw_ref[0:1, :]
         + cx_ref[...] * cw_ref[1:2, :]
         + cext_ref[HALO + 1:HALO + 1 + ts, :] * cw_ref[2:3, :])
    y = jnp.dot((bg_ref[...] * z).astype(BF16), wout_ref[...], preferred_element_type=F32)
    h, u, aff = _post_tail(x_ref[...], y, gate_ref[...], lng_ref[...], lnb_ref[...],
                           shf_ref[...], scf_ref[...], wr_ref[...])
    h_ref[...] = h
    _store_token_major(u_ref, u)
    aff_ref[...] = aff


def _l1_out_proj(cx, bg, conv_w, wout_bf, x, gate, lng, lnb, shf, scf, wr_t, ts):
    b, s, _ = x.shape
    tail_in, tail_out = _tail_specs(ts)
    in_specs = (
        _halo_specs(ts, D_MODEL, s)
        + [pl.BlockSpec((None, ts, D_MODEL), lambda bi, si: (bi, si, 0)),
           pl.BlockSpec(conv_w.shape, lambda bi, si: (0, 0)),
           pl.BlockSpec((D_MODEL, D_MODEL), lambda bi, si: (0, 0))]
        + tail_in)
    return pl.pallas_call(
        _l1_out_kernel,
        grid=(b, s // ts),
        in_specs=in_specs,
        out_specs=tail_out,
        out_shape=_tail_out_shape(b, s),
        scratch_shapes=[pltpu.VMEM((ts + 2 * HALO, D_MODEL), F32)],
        compiler_params=_cparams(("arbitrary", "arbitrary")),
        name="l1_out_proj",
    )(cx, cx, cx, bg, conv_w, wout_bf, x, gate, lng, lnb, shf, scf, wr_t)


def _final_kernel(h_ref, f_ref, gf_ref, lng_ref, lnb_ref, o_ref):
    f = _load_token_major(f_ref, h_ref.shape[0])
    o_ref[...] = _layer_norm(DEEPNORM_ALPHA * h_ref[...] + gf_ref[...] * f, lng_ref[...], lnb_ref[...])


def _final_norm(h, f, gate_f, lng, lnb, ts):
    b, s, _ = h.shape
    tok = lambda: pl.BlockSpec((None, ts, D_MODEL), lambda bi, si: (bi, si, 0))
    return pl.pallas_call(
        _final_kernel,
        grid=(b, s // ts),
        in_specs=[tok(), pl.BlockSpec((None, ts * TOKEN_TILE_ROWS, LANES), lambda bi, si: (bi, si, 0)),
                  pl.BlockSpec((None, 1, D_MODEL), lambda bi, si: (bi, 0, 0)),
                  pl.BlockSpec((1, D_MODEL), lambda bi, si: (0, 0)),
                  pl.BlockSpec((1, D_MODEL), lambda bi, si: (0, 0))],
        out_specs=tok(),
        out_shape=jax.ShapeDtypeStruct((b, s, D_MODEL), F32),
        compiler_params=_cparams(("arbitrary", "arbitrary")),
        name="final_norm",
    )(h, f, gate_f, lng, lnb)


def _rope_lane_tables(seq):
    rows = seq // GRID_W
    row = jnp.repeat(jnp.arange(rows, dtype=F32), GRID_W)
    col = jnp.tile(jnp.arange(GRID_W, dtype=F32), rows)
    axis_dim = HEAD_DIM // 2
    inv_freq = ROPE_THETA ** (-jnp.arange(0, axis_dim, 2, dtype=F32) / axis_dim)
    ang = jnp.concatenate([row[:, None] * inv_freq, col[:, None] * inv_freq], axis=-1)
    cos = jnp.repeat(jnp.cos(ang), 2, axis=-1)
    sin = jnp.repeat(jnp.sin(ang), 2, axis=-1)
    sign = jnp.tile(jnp.array([-1.0, 1.0], F32), HEAD_DIM // 2)
    return jnp.tile(cos, (1, N_Q_HEADS)), jnp.tile(sin * sign, (1, N_Q_HEADS))


def kernel(x, c, ctx, c_ctx, w_mod, b_mod, ln_mix_g, ln_mix_b, ln_ffn_g, ln_ffn_b, w_mix_in, q_norm_g, k_norm_g, w_pool_grp, pool_scale, w_mix_out, w_conv_in, conv_w, w_conv_out, w_router, w_exp_gate, w_exp_up, w_exp_down):
    b, s, d = x.shape
    lc = ctx.shape[1]
    ts = 512

    cond_rows = jnp.zeros((SUBLANES, d), F32).at[:b].set(c).at[b].set(c_ctx)
    mods = _modulation(cond_rows, w_mod, b_mod).reshape(DEPTH, SUBLANES, 6, d)
    m_lat = lambda layer, k: mods[layer, :b, k][:, None, :]
    m_ctx = lambda layer, k: mods[layer, b:b + 1, k][:, None, :]
    vec = lambda t: t.reshape(1, -1)

    w_in_bf = w_mix_in[0].astype(BF16)
    grp = lax.broadcasted_iota(I32, (ATTN_WIDTH, ATTN_WIDTH), 0) // HEAD_DIM
    bd = jnp.where(grp == grp.T, 1.0 / HEAD_DIM, 0.0).astype(BF16)
    qg = jnp.tile(q_norm_g[0], N_Q_HEADS).reshape(1, -1)
    kg = jnp.tile(k_norm_g[0], N_KV_HEADS).reshape(1, -1)
    cos_t, sin_t = _rope_lane_tables(s)
    q, k, v, p = _l0_in_proj(x, m_lat(0, 0), m_lat(0, 1), w_in_bf, bd, qg, kg, cos_t, sin_t, ts)
    ones = jnp.ones((lc, ATTN_WIDTH), F32)
    _, kc, vc, _ = _l0_in_proj(ctx, m_ctx(0, 0), m_ctx(0, 1), w_in_bf, bd, qg, kg, ones, 0.0 * ones, lc)
    k_all = jnp.concatenate([kc, k], axis=1)
    v_t = jnp.swapaxes(jnp.concatenate([vc, v], axis=1), 1, 2).reshape(b, N_KV_HEADS, HEAD_DIM, lc + s)
    v_t = jnp.concatenate([v_t, jnp.ones((b, N_KV_HEADS, _VT_ROWS - HEAD_DIM, lc + s), BF16)], axis=2)
    score_bound = (1.02 * HEAD_DIM ** 0.5 * LOG2_E) * jnp.max(jnp.abs(q_norm_g[0])) * jnp.max(jnp.abs(k_norm_g[0]))
    a = _attention(q, k_all, v_t, score_bound, 512)
    h, u, aff = _l0_out_proj(a, p, w_pool_grp[0].astype(BF16), vec(pool_scale[0]), w_mix_out[0].astype(BF16),
                             x, m_lat(0, 2), vec(ln_mix_g[0]), vec(ln_mix_b[0]), m_lat(0, 3), m_lat(0, 4),
                             _router_weights(w_router[0]), ts)
    f = _moe(u, aff, 0, w_exp_gate, w_exp_up, w_exp_down)

    h, bg, cx = _l1_in_proj(h, f, m_lat(0, 5), vec(ln_ffn_g[0]), vec(ln_ffn_b[0]), m_lat(1, 0), m_lat(1, 1),
                            w_conv_in[0].astype(BF16), ts)
    h, u, aff = _l1_out_proj(cx, bg, conv_w[0], w_conv_out[0].astype(BF16),
                             h, m_lat(1, 2), vec(ln_mix_g[1]), vec(ln_mix_b[1]), m_lat(1, 3), m_lat(1, 4),
                             _router_weights(w_router[1]), ts)
    f = _moe(u, aff, 1, w_exp_gate, w_exp_up, w_exp_down)
    return _final_norm(h, f, m_lat(1, 5), vec(ln_ffn_g[1]), vec(ln_ffn_b[1]), ts)
```

```python
import functools

import jax
import jax.numpy as jnp
from jax import lax
from jax.experimental import pallas as pl
from jax.experimental.pallas import tpu as pltpu

F32, BF16, I32 = jnp.float32, jnp.bfloat16, jnp.int32

D_MODEL = 1024
DEPTH = 2
GRID_W = 64
N_Q_HEADS = 8
N_KV_HEADS = 2
HEAD_DIM = 64
ATTN_WIDTH = N_Q_HEADS * HEAD_DIM
KV_WIDTH = N_KV_HEADS * HEAD_DIM
ROPE_THETA = 10000.0
POOL_WINDOWS = (2, 4, 8, 16)
POOL_GROUP_DIM = 128
POOL_WIDTH = len(POOL_WINDOWS) * POOL_GROUP_DIM
MIX_IN_WIDTH = ATTN_WIDTH + 2 * KV_WIDTH + POOL_WIDTH
N_EXPERTS = 16
EXPERT_FF = 1024
CAPACITY_FACTOR = 2
NORM_EPS = 1e-6
DEEPNORM_ALPHA = (2 * DEPTH) ** 0.25
LOG2_E = 1.4426950408889634

LANES = 128
SUBLANES = 8
HALO = SUBLANES
VMEM_LIMIT = 56 * 1024 * 1024

_NT = (((1,), (1,)), ((), ()))


def _cparams(sem):
    return pltpu.CompilerParams(dimension_semantics=sem, vmem_limit_bytes=VMEM_LIMIT)


TOKEN_TILE_ROWS = D_MODEL // LANES


def _store_token_major(ref, rows):
    n = rows.shape[0]
    for j in range(TOKEN_TILE_ROWS):
        ref[pl.ds(j, n, stride=TOKEN_TILE_ROWS), :] = rows[:, j * LANES:(j + 1) * LANES]


def _load_token_major(ref, n):
    return jnp.concatenate(
        [ref[pl.ds(j, n, stride=TOKEN_TILE_ROWS), :] for j in range(TOKEN_TILE_ROWS)], axis=1)


def _layer_norm(z, g, b):
    mu = jnp.mean(z, axis=-1, keepdims=True)
    zc = z - mu
    var = jnp.mean(zc * zc, axis=-1, keepdims=True)
    return zc * lax.rsqrt(var + NORM_EPS) * g + b


def _split_bf16(t):
    hi = t.astype(BF16)
    return hi, (t - hi.astype(F32)).astype(BF16)


def _mod_kernel(c_ref, w_ref, b_ref, o_ref):
    c = c_ref[...]
    c_hi, c_lo = _split_bf16(c * jax.nn.sigmoid(c))
    w_hi, w_lo = _split_bf16(w_ref[...])
    dot = functools.partial(jnp.dot, preferred_element_type=F32)
    o_ref[...] = dot(c_hi, w_hi) + dot(c_lo, w_hi) + dot(c_hi, w_lo) + b_ref[...]


def _modulation(cond_rows, w_mod, b_mod):
    n_out = w_mod.shape[-1]
    tn = D_MODEL
    return pl.pallas_call(
        _mod_kernel,
        grid=(DEPTH, n_out // tn),
        in_specs=[
            pl.BlockSpec((SUBLANES, D_MODEL), lambda l, j: (0, 0)),
            pl.BlockSpec((None, D_MODEL, tn), lambda l, j: (l, 0, j)),
            pl.BlockSpec((None, 1, tn), lambda l, j: (l, 0, j)),
        ],
        out_specs=pl.BlockSpec((None, SUBLANES, tn), lambda l, j: (l, 0, j)),
        out_shape=jax.ShapeDtypeStruct((DEPTH, SUBLANES, n_out), F32),
        compiler_params=_cparams(("arbitrary", "arbitrary")),
        name="modulation",
    )(cond_rows, w_mod, b_mod.reshape(DEPTH, 1, n_out))


def _group_mean_sq(t, bd):
    hi, lo = _split_bf16(t * t)
    return (jnp.dot(hi, bd, preferred_element_type=F32) + jnp.dot(lo, bd, preferred_element_type=F32))


def _rope_chunk(y, c, s_signed):
    n = y.shape[-1]
    lane = lax.broadcasted_iota(I32, y.shape, 1)
    nxt = pltpu.roll(y, n - 1, axis=1)
    prv = pltpu.roll(y, 1, axis=1)
    partner = jnp.where((lane & 1) == 0, nxt, prv)
    return y * c + partner * s_signed


def _l0_in_kernel(x_ref, sh_ref, sc_ref, w_ref, bd_ref, qg_ref, kg_ref, cos_ref, sin_ref,
                  q_ref, k_ref, v_ref, p_ref):
    u = x_ref[...] * (1.0 + sc_ref[...]) + sh_ref[...]
    h = jnp.dot(u.astype(BF16), w_ref[...], preferred_element_type=F32)
    q = h[:, :ATTN_WIDTH]
    k = h[:, ATTN_WIDTH:ATTN_WIDTH + KV_WIDTH]
    bd = bd_ref[...]
    qn = q * lax.rsqrt(_group_mean_sq(q, bd) + NORM_EPS) * qg_ref[...]
    kn = k * lax.rsqrt(_group_mean_sq(k, bd[:KV_WIDTH, :KV_WIDTH]) + NORM_EPS) * kg_ref[...]
    scale = HEAD_DIM ** -0.5 * LOG2_E
    heads_per_kv = N_Q_HEADS // N_KV_HEADS
    lane = lax.broadcasted_iota(I32, (x_ref.shape[0], LANES), 1)
    for c in range(ATTN_WIDTH // LANES):
        sl = slice(c * LANES, (c + 1) * LANES)
        pair = _rope_chunk(qn[:, sl], cos_ref[:, sl], sin_ref[:, sl]) * scale
        swapped = pltpu.roll(pair, HEAD_DIM, axis=1)
        g = (2 * c) // heads_per_kv
        keep = (lane < HEAD_DIM) if g == 0 else (lane >= HEAD_DIM)
        first, second = (pair, swapped) if g == 0 else (swapped, pair)
        q_ref[:, (2 * c) * LANES:(2 * c + 1) * LANES] = jnp.where(keep, first, 0.0).astype(BF16)
        q_ref[:, (2 * c + 1) * LANES:(2 * c + 2) * LANES] = jnp.where(keep, second, 0.0).astype(BF16)
    k_ref[...] = _rope_chunk(kn, cos_ref[:, :KV_WIDTH], sin_ref[:, :KV_WIDTH]).astype(BF16)
    v_ref[...] = h[:, ATTN_WIDTH + KV_WIDTH:ATTN_WIDTH + 2 * KV_WIDTH].astype(BF16)
    p_ref[...] = h[:, ATTN_WIDTH + 2 * KV_WIDTH:]


def _l0_in_proj(x, shift, scale, w_in_bf, bd, qg, kg, cos_t, sin_t, ts):
    b, l, _ = x.shape
    per_batch = shift.shape[0] == b
    mod_map = (lambda bi, si: (bi, 0, 0)) if per_batch else (lambda bi, si: (0, 0, 0))
    full = lambda bi, si: (0, 0)
    return pl.pallas_call(
        _l0_in_kernel,
        grid=(b, l // ts),
        in_specs=[
            pl.BlockSpec((None, ts, D_MODEL), lambda bi, si: (bi, si, 0)),
            pl.BlockSpec((None, 1, D_MODEL), mod_map),
            pl.BlockSpec((None, 1, D_MODEL), mod_map),
            pl.BlockSpec((D_MODEL, MIX_IN_WIDTH), full),
            pl.BlockSpec((ATTN_WIDTH, ATTN_WIDTH), full),
            pl.BlockSpec((1, ATTN_WIDTH), full),
            pl.BlockSpec((1, KV_WIDTH), full),
            pl.BlockSpec((ts, ATTN_WIDTH), lambda bi, si: (si, 0)),
            pl.BlockSpec((ts, ATTN_WIDTH), lambda bi, si: (si, 0)),
        ],
        out_specs=[
            pl.BlockSpec((None, ts, N_Q_HEADS * LANES), lambda bi, si: (bi, si, 0)),
            pl.BlockSpec((None, ts, KV_WIDTH), lambda bi, si: (bi, si, 0)),
            pl.BlockSpec((None, ts, KV_WIDTH), lambda bi, si: (bi, si, 0)),
            pl.BlockSpec((None, ts, POOL_WIDTH), lambda bi, si: (bi, si, 0)),
        ],
        out_shape=[
            jax.ShapeDtypeStruct((b, l, N_Q_HEADS * LANES), BF16),
            jax.ShapeDtypeStruct((b, l, KV_WIDTH), BF16),
            jax.ShapeDtypeStruct((b, l, KV_WIDTH), BF16),
            jax.ShapeDtypeStruct((b, l, POOL_WIDTH), F32),
        ],
        compiler_params=_cparams(("arbitrary", "arbitrary")),
        name="l0_in_proj",
    )(x, shift, scale, w_in_bf, bd, qg, kg, cos_t, sin_t)


_VT_ROWS = HEAD_DIM + 16


_KV_CHUNK = 2176
_MAX_ROWS = 64


def _attn_kernel(q_ref, k_ref, vt_ref, o_ref, ot_ref, *s_refs):
    heads_per_kv = N_Q_HEADS // N_KV_HEADS
    lk, tq = k_ref.shape[0], q_ref.shape[0]
    chunks = [(c0, min(_KV_CHUNK, lk - c0)) for c0 in range(0, lk, _KV_CHUNK)]

    def scores(h, c0, n, m):
        qh = q_ref[:, h * LANES:(h + 1) * LANES]
        s_c = lax.dot_general(k_ref[c0:c0 + n, :], qh, _NT, preferred_element_type=F32)
        s_refs[h % 2][c0:c0 + n, :] = s_c
        mc = jnp.max(s_c.reshape(n // _MAX_ROWS, _MAX_ROWS, tq), axis=0)
        return mc if m is None else jnp.maximum(m, mc)

    def weighted(h, c0, n, m, o_t):
        p_c = jnp.exp2(s_refs[h % 2][c0:c0 + n, :] - m).astype(BF16)
        o_c = jnp.dot(vt_ref[h // heads_per_kv, :, c0:c0 + n], p_c, preferred_element_type=F32)
        return o_c if o_t is None else o_t + o_c

    m_next = None
    for c0, n in chunks:
        m_next = scores(0, c0, n, m_next)
    for h in range(N_Q_HEADS):
        m = jnp.max(m_next, axis=0, keepdims=True)
        m_next = None
        o_t = None
        for c0, n in chunks:
            o_t = weighted(h, c0, n, m, o_t)
            if h + 1 < N_Q_HEADS:
                m_next = scores(h + 1, c0, n, m_next)
        ot_ref[h * HEAD_DIM:(h + 1) * HEAD_DIM, :] = (
            o_t[:HEAD_DIM, :] / o_t[HEAD_DIM:HEAD_DIM + 1, :])
    o_ref[...] = ot_ref[...].T.astype(BF16)


_BOUNDED_KV_CHUNK = 1088


def _attn_bounded_kernel(bound_ref, q_ref, k_ref, vt_ref, o_ref, ot_ref):
    heads_per_kv = N_Q_HEADS // N_KV_HEADS
    lk = k_ref.shape[0]
    shift = bound_ref[0]
    for h in range(N_Q_HEADS):
        g = h // heads_per_kv
        qh = q_ref[:, h * LANES:(h + 1) * LANES]
        o_t = None
        for c0 in range(0, lk, _BOUNDED_KV_CHUNK):
            n = min(_BOUNDED_KV_CHUNK, lk - c0)
            s_c = lax.dot_general(k_ref[c0:c0 + n, :], qh, _NT, preferred_element_type=F32)
            p_c = jnp.exp2(s_c - shift).astype(BF16)
            o_c = jnp.dot(vt_ref[g, :, c0:c0 + n], p_c, preferred_element_type=F32)
            o_t = o_c if o_t is None else o_t + o_c
        ot_ref[h * HEAD_DIM:(h + 1) * HEAD_DIM, :] = (
            o_t[:HEAD_DIM, :] / o_t[HEAD_DIM:HEAD_DIM + 1, :])
    o_ref[...] = ot_ref[...].T.astype(BF16)


_MAX_SAFE_SCORE_BOUND = 40.0


def _attention(q, k_all, vt, score_bound, tq):
    b, s, qw = q.shape
    lk = k_all.shape[1]
    specs = dict(
        grid=(b, s // tq),
        out_specs=pl.BlockSpec((None, tq, ATTN_WIDTH), lambda bi, qi: (bi, qi, 0)),
        out_shape=jax.ShapeDtypeStruct((b, s, ATTN_WIDTH), BF16),
        compiler_params=_cparams(("arbitrary", "arbitrary")),
    )
    in_specs = [
        pl.BlockSpec((None, tq, qw), lambda bi, qi: (bi, qi, 0)),
        pl.BlockSpec((None, lk, KV_WIDTH), lambda bi, qi: (bi, 0, 0)),
        pl.BlockSpec((None, N_KV_HEADS, _VT_ROWS, lk), lambda bi, qi: (bi, 0, 0, 0)),
    ]

    def bounded(_):
        return pl.pallas_call(
            _attn_bounded_kernel,
            in_specs=[pl.BlockSpec(memory_space=pltpu.SMEM)] + in_specs,
            scratch_shapes=[pltpu.VMEM((ATTN_WIDTH, tq), F32)],
            name="attention_bounded", **specs,
        )(score_bound.reshape(1), q, k_all, vt)

    def exact(_):
        return pl.pallas_call(
            _attn_kernel,
            in_specs=in_specs,
            scratch_shapes=[pltpu.VMEM((ATTN_WIDTH, tq), F32), pltpu.VMEM((lk, tq), F32),
                            pltpu.VMEM((lk, tq), F32)],
            name="attention", **specs,
        )(q, k_all, vt)

    return lax.cond(score_bound < _MAX_SAFE_SCORE_BOUND, bounded, exact, None)


def _router_weights(w_router_layer):
    hi, lo = _split_bf16(w_router_layer.T)
    return jnp.concatenate([hi, lo], axis=0)


def _post_tail(resid, y, gate, lng, lnb, shf, scf, wr2):
    h = _layer_norm(DEEPNORM_ALPHA * resid + gate * y, lng, lnb)
    u = h * (1.0 + scf) + shf
    u_hi, u_lo = _split_bf16(u)
    first = lax.dot_general(wr2, u_hi, _NT, preferred_element_type=F32)
    second = lax.dot_general(wr2[:N_EXPERTS], u_lo, _NT, preferred_element_type=F32)
    logits = first[:N_EXPERTS] + first[N_EXPERTS:] + second
    ex = jnp.exp(logits - jnp.max(logits, axis=0, keepdims=True))
    aff = ex / jnp.sum(ex, axis=0, keepdims=True)
    return h, u, aff


def _fill_halo(ext_ref, prev_ref, main_ref, next_ref, ts):
    st = pl.program_id(1)
    last = pl.num_programs(1) - 1
    ext_ref[0:HALO, :] = jnp.where(st > 0, prev_ref[...], 0.0)
    ext_ref[HALO:HALO + ts, :] = main_ref[...]
    ext_ref[HALO + ts:HALO + ts + HALO, :] = jnp.where(st < last, next_ref[...], 0.0)


def _halo_specs(ts, width, seq):
    r = ts // HALO
    nblk = seq // HALO
    return [
        pl.BlockSpec((None, ts, width), lambda bi, si: (bi, si, 0)),
        pl.BlockSpec((None, HALO, width), lambda bi, si: (bi, jnp.maximum(si * r - 1, 0), 0)),
        pl.BlockSpec((None, HALO, width), lambda bi, si: (bi, jnp.minimum((si + 1) * r, nblk - 1), 0)),
    ]


def _tail_specs(ts):
    row = lambda bi, si: (bi, 0, 0)
    full = lambda bi, si: (0, 0)
    in_specs = [
        pl.BlockSpec((None, ts, D_MODEL), lambda bi, si: (bi, si, 0)),
        pl.BlockSpec((None, 1, D_MODEL), row),
        pl.BlockSpec((1, D_MODEL), full),
        pl.BlockSpec((1, D_MODEL), full),
        pl.BlockSpec((None, 1, D_MODEL), row),
        pl.BlockSpec((None, 1, D_MODEL), row),
        pl.BlockSpec((2 * N_EXPERTS, D_MODEL), full),
    ]
    out_specs = [
        pl.BlockSpec((None, ts, D_MODEL), lambda bi, si: (bi, si, 0)),
        pl.BlockSpec((None, ts * TOKEN_TILE_ROWS, LANES), lambda bi, si: (bi, si, 0)),
        pl.BlockSpec((None, N_EXPERTS, ts), lambda bi, si: (bi, 0, si)),
    ]
    return in_specs, out_specs


def _tail_out_shape(b, s):
    return [jax.ShapeDtypeStruct((b, s, D_MODEL), F32),
            jax.ShapeDtypeStruct((b, s * TOKEN_TILE_ROWS, LANES), F32),
            jax.ShapeDtypeStruct((b, N_EXPERTS, s), F32)]


def _l0_out_kernel(seq, a_ref, p_ref, pprev_ref, pnext_ref, wpool_ref, pscale_ref, wout_ref,
                   x_ref, gate_ref, lng_ref, lnb_ref, shf_ref, scf_ref, wr_ref,
                   h_ref, u_ref, aff_ref, pext_ref):
    ts = p_ref.shape[0]
    _fill_halo(pext_ref, pprev_ref, p_ref, pnext_ref, ts)
    t = pl.program_id(1) * ts + lax.broadcasted_iota(I32, (ts, 1), 0)
    outs = []
    for g, w in enumerate(POOL_WINDOWS):
        sl = slice(g * POOL_GROUP_DIM, (g + 1) * POOL_GROUP_DIM)
        fwd = pext_ref[:, sl]
        n_ext = fwd.shape[0]
        span = 1
        while span < w // 2:
            fwd = fwd + pltpu.roll(fwd, n_ext - span, axis=0)
            span *= 2
        acc = (fwd + pltpu.roll(fwd, w // 2, axis=0))[HALO:HALO + ts, :]
        cnt = jnp.minimum(t + w // 2, seq) - jnp.maximum(t - w // 2, 0)
        pooled = acc / cnt.astype(F32) - p_ref[:, sl]
        og = jnp.dot(pooled.astype(BF16), wpool_ref[g], preferred_element_type=F32)
        outs.append((og * pscale_ref[:, sl]).astype(BF16))
    pool = jnp.concatenate(outs, axis=-1)
    y = (jnp.dot(a_ref[...], wout_ref[:ATTN_WIDTH, :], preferred_element_type=F32)
         + jnp.dot(pool, wout_ref[ATTN_WIDTH:, :], preferred_element_type=F32))
    h, u, aff = _post_tail(x_ref[...], y, gate_ref[...], lng_ref[...], lnb_ref[...],
                           shf_ref[...], scf_ref[...], wr_ref[...])
    h_ref[...] = h
    _store_token_major(u_ref, u)
    aff_ref[...] = aff


def _l0_out_proj(a, p, wpool_bf, pscale, wout_bf, x, gate, lng, lnb, shf, scf, wr_t, ts):
    b, s, _ = x.shape
    full2 = lambda bi, si: (0, 0)
    tail_in, tail_out = _tail_specs(ts)
    in_specs = (
        [pl.BlockSpec((None, ts, ATTN_WIDTH), lambda bi, si: (bi, si, 0))]
        + _halo_specs(ts, POOL_WIDTH, s)
        + [pl.BlockSpec((len(POOL_WINDOWS), POOL_GROUP_DIM, POOL_GROUP_DIM), lambda bi, si: (0, 0, 0)),
           pl.BlockSpec((1, POOL_WIDTH), full2),
           pl.BlockSpec((ATTN_WIDTH + POOL_WIDTH, D_MODEL), full2)]
        + tail_in)
    return pl.pallas_call(
        functools.partial(_l0_out_kernel, s),
        grid=(b, s // ts),
        in_specs=in_specs,
        out_specs=tail_out,
        out_shape=_tail_out_shape(b, s),
        scratch_shapes=[pltpu.VMEM((ts + 2 * HALO, POOL_WIDTH), F32)],
        compiler_params=_cparams(("arbitrary", "arbitrary")),
        name="l0_out_proj",
    )(a, p, p, p, wpool_bf, pscale, wout_bf, x, gate, lng, lnb, shf, scf, wr_t)


def _exclusive_cumsum_lanes(mask, tri):
    n = mask.shape[1]
    w = tri.shape[0]
    carry = jnp.zeros((mask.shape[0], 1), F32)
    parts = []
    for c in range(n // w):
        blk = mask[:, c * w:(c + 1) * w]
        inc = jnp.dot(blk.astype(BF16), tri, preferred_element_type=F32)
        parts.append(inc - blk + carry)
        carry = carry + inc[:, w - 1:w]
    return jnp.concatenate(parts, axis=1)


_TOKEN_SPLIT = 64
_SLOT_LO = 32
_PACKED_VALUES = 5


def _select_kernel(cap, aff_ref, out_ref):
    aff = aff_ref[...]
    e = aff.shape[0]
    as_f32 = lambda word: pltpu.bitcast(word, F32)

    def refine(i, thr):
        cand = thr | jnp.left_shift(jnp.int32(1), 30 - i)
        cnt = jnp.sum(jnp.where(aff >= as_f32(cand), 1.0, 0.0), axis=1, keepdims=True)
        return jnp.where(cnt >= cap, cand, thr)

    thr = lax.fori_loop(0, 31, refine, jnp.zeros((e, 1), I32))
    w = 2 * LANES
    r = lax.broadcasted_iota(I32, (w, w), 0)
    c = lax.broadcasted_iota(I32, (w, w), 1)
    tri = jnp.where(r <= c, 1.0, 0.0).astype(BF16)
    gt = jnp.where(aff >= as_f32(thr + 1), 1.0, 0.0)
    eq = jnp.where(aff >= as_f32(thr), 1.0, 0.0) - gt
    need = cap - jnp.sum(gt, axis=1, keepdims=True)
    sel = gt + eq * jnp.where(_exclusive_cumsum_lanes(eq, tri) < need, 1.0, 0.0)
    pos = _exclusive_cumsum_lanes(sel, tri)
    dest = jnp.where(sel > 0.5, pos, float(cap)).astype(I32)

    n = aff.shape[1]
    n_hi = cap // _SLOT_LO
    d_hi = dest // _SLOT_LO
    d_lo = dest % _SLOT_LO
    t = lax.broadcasted_iota(I32, (1, n), 1)
    t_hi = (t // _TOKEN_SPLIT).astype(F32)
    t_lo = (t % _TOKEN_SPLIT).astype(F32)
    a1 = aff.astype(BF16).astype(F32)
    r1 = aff - a1
    a2 = r1.astype(BF16).astype(F32)
    a3 = r1 - a2
    row_hi = lax.broadcasted_iota(I32, (n_hi, n), 0)
    row_lo = lax.broadcasted_iota(I32, (_SLOT_LO, n), 0)
    for ei in range(e):
        rs = slice(ei, ei + 1)
        h = jnp.where(d_hi[rs] == row_hi, 1.0, 0.0)
        l = jnp.where(d_lo[rs] == row_lo, 1.0, 0.0).astype(BF16)
        vals = jnp.concatenate([h * t_hi, h * t_lo, h * a1[rs], h * a2[rs], h * a3[rs]], axis=0)
        out_ref[ei] = lax.dot_general(vals.astype(BF16), l, _NT, preferred_element_type=F32)


def _select(aff, cap):
    b, e, n = aff.shape
    rows = _PACKED_VALUES * (cap // _SLOT_LO)
    return pl.pallas_call(
        functools.partial(_select_kernel, cap),
        grid=(b,),
        in_specs=[pl.BlockSpec((None, e, n), lambda bi: (bi, 0, 0))],
        out_specs=pl.BlockSpec((None, e, rows, _SLOT_LO), lambda bi: (bi, 0, 0, 0)),
        out_shape=jax.ShapeDtypeStruct((b, e, rows, _SLOT_LO), F32),
        compiler_params=_cparams(("arbitrary",)),
        name="expert_select",
    )(aff)


def _token_tile(t):
    return pl.ds(pl.multiple_of(t * TOKEN_TILE_ROWS, TOKEN_TILE_ROWS), TOKEN_TILE_ROWS)


_GATHER_AHEAD = 2
_FF_BLOCK = 512


def _expert_kernel(cap, seq, idx_ref, u_hbm, wg_ref, wu_ref, wd_ref, y_ref,
                   xe_ref, wg_bf, wu_bf, wd_bf, sem):
    b = pl.program_id(1)
    nb = pl.num_programs(1)
    step = pl.program_id(0) * nb + b
    nsteps = pl.num_programs(0) * nb
    nbuf = _GATHER_AHEAD + 1
    slot = step % nbuf

    def start_gather(k, into, rolled):
        ek = k // nb
        bk = k % nb
        base = (bk * N_EXPERTS + ek) * cap
        tok0 = bk * seq

        def start(i, priority):
            tok = tok0 + idx_ref[base + i]
            pltpu.make_async_copy(u_hbm.at[_token_tile(tok), :], xe_ref.at[into, _token_tile(i), :],
                                  sem.at[into]).start(priority=priority)

        if rolled:
            def issue(ci, carry):
                for j in range(SUBLANES):
                    start(ci * SUBLANES + j, j % 2)
                return carry

            lax.fori_loop(0, cap // SUBLANES, issue, 0)
        else:
            for i in range(cap):
                start(i, i % 2)

    def wait_gather(into):
        pltpu.make_async_copy(u_hbm.at[pl.ds(0, cap * TOKEN_TILE_ROWS), :], xe_ref.at[into], sem.at[into]).wait()

    @pl.when(step == 0)
    def _():
        for k in range(_GATHER_AHEAD):
            start_gather(k, k, rolled=True)

    @pl.when(b == 0)
    def _():
        wg_bf[...] = wg_ref[...].astype(BF16)
        wu_bf[...] = wu_ref[...].astype(BF16)
        wd_bf[...] = wd_ref[...].astype(BF16)

    wait_gather(slot)
    x = _load_token_major(xe_ref.at[slot], cap).astype(BF16)
    start_gather(jnp.minimum(step + _GATHER_AHEAD, nsteps - 1), (step + _GATHER_AHEAD) % nbuf, rolled=False)
    blocks = [slice(f0, f0 + _FF_BLOCK) for f0 in range(0, EXPERT_FF, _FF_BLOCK)]
    pre = [(jnp.dot(x, wg_bf[:, fs], preferred_element_type=F32),
            jnp.dot(x, wu_bf[:, fs], preferred_element_type=F32)) for fs in blocks]
    y = None
    for fs, (hg, hu) in zip(blocks, pre):
        act = (hg * jax.nn.sigmoid(hg) * hu).astype(BF16)
        part = jnp.dot(act, wd_bf[fs, :], preferred_element_type=F32)
        y = part if y is None else y + part
    _store_token_major(y_ref, y)

    @pl.when(step == nsteps - 1)
    def _():
        for k in range(1, nbuf):
            wait_gather((step + k) % nbuf)


def _expert_ffn(idx_flat, u_rows, layer, w_g, w_u, w_d, batch, seq, cap):
    wspec = lambda: pl.BlockSpec((None, None, D_MODEL, EXPERT_FF), lambda e, b, idx: (layer, e, 0, 0))
    grid_spec = pltpu.PrefetchScalarGridSpec(
        num_scalar_prefetch=1,
        grid=(N_EXPERTS, batch),
        in_specs=[
            pl.BlockSpec(memory_space=pl.ANY),
            wspec(), wspec(),
            pl.BlockSpec((None, None, EXPERT_FF, D_MODEL), lambda e, b, idx: (layer, e, 0, 0)),
        ],
        out_specs=pl.BlockSpec((None, None, cap * TOKEN_TILE_ROWS, LANES), lambda e, b, idx: (b, e, 0, 0)),
        scratch_shapes=[
            pltpu.VMEM((_GATHER_AHEAD + 1, cap * TOKEN_TILE_ROWS, LANES), F32),
            pltpu.VMEM((D_MODEL, EXPERT_FF), BF16),
            pltpu.VMEM((D_MODEL, EXPERT_FF), BF16),
            pltpu.VMEM((EXPERT_FF, D_MODEL), BF16),
            pltpu.SemaphoreType.DMA((_GATHER_AHEAD + 1,)),
        ],
    )
    return pl.pallas_call(
        functools.partial(_expert_kernel, cap, seq),
        grid_spec=grid_spec,
        out_shape=jax.ShapeDtypeStruct((batch, N_EXPERTS, cap * TOKEN_TILE_ROWS, LANES), F32),
        compiler_params=_cparams(("arbitrary", "arbitrary")),
        name="expert_ffn",
    )(idx_flat, u_rows, w_g, w_u, w_d)


_COMBINE_UNROLL = 16


def _combine_kernel(cap, idx_ref, gate_ref, y_ref, f_ref):
    b = pl.program_id(0)
    e = pl.program_id(1)

    @pl.when(e == 0)
    def _():
        f_ref[...] = jnp.zeros_like(f_ref)

    base = (b * N_EXPERTS + e) * cap

    def chunk(ci, carry):
        i0 = ci * _COMBINE_UNROLL
        tiles = [_token_tile(idx_ref[base + i0 + j]) for j in range(_COMBINE_UNROLL)]
        sums = [f_ref[tiles[j], :] + y_ref[_token_tile(i0 + j), :] * gate_ref[base + i0 + j]
                for j in range(_COMBINE_UNROLL)]
        for j in range(_COMBINE_UNROLL):
            f_ref[tiles[j], :] = sums[j]
        return carry

    lax.fori_loop(0, cap // _COMBINE_UNROLL, chunk, 0)


def _combine(idx_flat, gate_flat, y, seq, cap):
    batch = y.shape[0]
    grid_spec = pltpu.PrefetchScalarGridSpec(
        num_scalar_prefetch=2,
        grid=(batch, N_EXPERTS),
        in_specs=[pl.BlockSpec((None, None, cap * TOKEN_TILE_ROWS, LANES), lambda b, e, idx, gate: (b, e, 0, 0))],
        out_specs=pl.BlockSpec((None, seq * TOKEN_TILE_ROWS, LANES), lambda b, e, idx, gate: (b, 0, 0)),
    )
    return pl.pallas_call(
        functools.partial(_combine_kernel, cap),
        grid_spec=grid_spec,
        out_shape=jax.ShapeDtypeStruct((batch, seq * TOKEN_TILE_ROWS, LANES), F32),
        compiler_params=_cparams(("arbitrary", "arbitrary")),
        name="expert_combine",
    )(idx_flat, gate_flat, y)


def _moe(u, aff, layer, w_g, w_u, w_d):
    b, _, n = aff.shape
    cap = CAPACITY_FACTOR * n // N_EXPERTS
    packed = _select(aff, cap).reshape(b, N_EXPERTS, _PACKED_VALUES, cap)
    idx_flat = (packed[:, :, 0] * _TOKEN_SPLIT + packed[:, :, 1]).astype(I32).reshape(-1)
    gate_flat = (packed[:, :, 2] + packed[:, :, 3] + packed[:, :, 4]).reshape(-1)
    y = _expert_ffn(idx_flat, u.reshape(b * n * TOKEN_TILE_ROWS, LANES), layer, w_g, w_u, w_d, b, n, cap)
    return _combine(idx_flat, gate_flat, y, n, cap)


def _l1_in_kernel(h_ref, f_ref, gf_ref, lng_ref, lnb_ref, sh_ref, sc_ref, w_ref, h2_ref, bg_ref, cx_ref):
    f = _load_token_major(f_ref, h_ref.shape[0])
    h2 = _layer_norm(DEEPNORM_ALPHA * h_ref[...] + gf_ref[...] * f, lng_ref[...], lnb_ref[...])
    h2_ref[...] = h2
    u = h2 * (1.0 + sc_ref[...]) + sh_ref[...]
    z = jnp.dot(u.astype(BF16), w_ref[...], preferred_element_type=F32)
    bg_ref[...] = z[:, :D_MODEL]
    cx_ref[...] = z[:, D_MODEL:2 * D_MODEL] * z[:, 2 * D_MODEL:]


def _l1_in_proj(h, f, gate_f, lng, lnb, shift, scale, w_bf, ts):
    b, s, _ = h.shape
    tok = lambda: pl.BlockSpec((None, ts, D_MODEL), lambda bi, si: (bi, si, 0))
    row = lambda: pl.BlockSpec((None, 1, D_MODEL), lambda bi, si: (bi, 0, 0))
    vec = lambda: pl.BlockSpec((1, D_MODEL), lambda bi, si: (0, 0))
    return pl.pallas_call(
        _l1_in_kernel,
        grid=(b, s // ts),
        in_specs=[tok(), pl.BlockSpec((None, ts * TOKEN_TILE_ROWS, LANES), lambda bi, si: (bi, si, 0)),
                  row(), vec(), vec(), row(), row(),
                  pl.BlockSpec((D_MODEL, 3 * D_MODEL), lambda bi, si: (0, 0))],
        out_specs=[tok(), tok(), tok()],
        out_shape=[jax.ShapeDtypeStruct((b, s, D_MODEL), F32)] * 3,
        compiler_params=_cparams(("arbitrary", "arbitrary")),
        name="l1_in_proj",
    )(h, f, gate_f, lng, lnb, shift, scale, w_bf)


def _l1_out_kernel(cx_ref, cprev_ref, cnext_ref, bg_ref, cw_ref, wout_ref,
                   x_ref, gate_ref, lng_ref, lnb_ref, shf_ref, scf_ref, wr_ref,
                   h_ref, u_ref, aff_ref, cext_ref):
    ts = cx_ref.shape[0]
    _fill_halo(cext_ref, cprev_ref, cx_ref, cnext_ref, ts)
    z = (cext_ref[HALO - 1:HALO - 1 + ts, :] * cw_ref[0:1, :]
         + cx_ref[...] * cw_ref[1:2, :]
         + cext_ref[HALO + 1:HALO + 1 + ts, :] * cw_ref[2:3, :])
    y = jnp.dot((bg_ref[...] * z).astype(BF16), wout_ref[...], preferred_element_type=F32)
    h, u, aff = _post_tail(x_ref[...], y, gate_ref[...], lng_ref[...], lnb_ref[...],
                           shf_ref[...], scf_ref[...], wr_ref[...])
    h_ref[...] = h
    _store_token_major(u_ref, u)
    aff_ref[...] = aff


def _l1_out_proj(cx, bg, conv_w, wout_bf, x, gate, lng, lnb, shf, scf, wr_t, ts):
    b, s, _ = x.shape
    tail_in, tail_out = _tail_specs(ts)
    in_specs = (
        _halo_specs(ts, D_MODEL, s)
        + [pl.BlockSpec((None, ts, D_MODEL), lambda bi, si: (bi, si, 0)),
           pl.BlockSpec(conv_w.shape, lambda bi, si: (0, 0)),
           pl.BlockSpec((D_MODEL, D_MODEL), lambda bi, si: (0, 0))]
        + tail_in)
    return pl.pallas_call(
        _l1_out_kernel,
        grid=(b, s // ts),
        in_specs=in_specs,
        out_specs=tail_out,
        out_shape=_tail_out_shape(b, s),
        scratch_shapes=[pltpu.VMEM((ts + 2 * HALO, D_MODEL), F32)],
        compiler_params=_cparams(("arbitrary", "arbitrary")),
        name="l1_out_proj",
    )(cx, cx, cx, bg, conv_w, wout_bf, x, gate, lng, lnb, shf, scf, wr_t)


def _final_kernel(h_ref, f_ref, gf_ref, lng_ref, lnb_ref, o_ref):
    f = _load_token_major(f_ref, h_ref.shape[0])
    o_ref[...] = _layer_norm(DEEPNORM_ALPHA * h_ref[...] + gf_ref[...] * f, lng_ref[...], lnb_ref[...])


def _final_norm(h, f, gate_f, lng, lnb, ts):
    b, s, _ = h.shape
    tok = lambda: pl.BlockSpec((None, ts, D_MODEL), lambda bi, si: (bi, si, 0))
    return pl.pallas_call(
        _final_kernel,
        grid=(b, s // ts),
        in_specs=[tok(), pl.BlockSpec((None, ts * TOKEN_TILE_ROWS, LANES), lambda bi, si: (bi, si, 0)),
                  pl.BlockSpec((None, 1, D_MODEL), lambda bi, si: (bi, 0, 0)),
                  pl.BlockSpec((1, D_MODEL), lambda bi, si: (0, 0)),
                  pl.BlockSpec((1, D_MODEL), lambda bi, si: (0, 0))],
        out_specs=tok(),
        out_shape=jax.ShapeDtypeStruct((b, s, D_MODEL), F32),
        compiler_params=_cparams(("arbitrary", "arbitrary")),
        name="final_norm",
    )(h, f, gate_f, lng, lnb)


def _rope_lane_tables(seq):
    rows = seq // GRID_W
    row = jnp.repeat(jnp.arange(rows, dtype=F32), GRID_W)
    col = jnp.tile(jnp.arange(GRID_W, dtype=F32), rows)
    axis_dim = HEAD_DIM // 2
    inv_freq = ROPE_THETA ** (-jnp.arange(0, axis_dim, 2, dtype=F32) / axis_dim)
    ang = jnp.concatenate([row[:, None] * inv_freq, col[:, None] * inv_freq], axis=-1)
    cos = jnp.repeat(jnp.cos(ang), 2, axis=-1)
    sin = jnp.repeat(jnp.sin(ang), 2, axis=-1)
    sign = jnp.tile(jnp.array([-1.0, 1.0], F32), HEAD_DIM // 2)
    return jnp.tile(cos, (1, N_Q_HEADS)), jnp.tile(sin * sign, (1, N_Q_HEADS))


def kernel(x, c, ctx, c_ctx, w_mod, b_mod, ln_mix_g, ln_mix_b, ln_ffn_g, ln_ffn_b, w_mix_in, q_norm_g, k_norm_g, w_pool_grp, pool_scale, w_mix_out, w_conv_in, conv_w, w_conv_out, w_router, w_exp_gate, w_exp_up, w_exp_down):
    b, s, d = x.shape
    lc = ctx.shape[1]
    ts = 512

    cond_rows = jnp.zeros((SUBLANES, d), F32).at[:b].set(c).at[b].set(c_ctx)
    mods = _modulation(cond_rows, w_mod, b_mod).reshape(DEPTH, SUBLANES, 6, d)
    m_lat = lambda layer, k: mods[layer, :b, k][:, None, :]
    m_ctx = lambda layer, k: mods[layer, b:b + 1, k][:, None, :]
    vec = lambda t: t.reshape(1, -1)

    w_in_bf = w_mix_in[0].astype(BF16)
    grp = lax.broadcasted_iota(I32, (ATTN_WIDTH, ATTN_WIDTH), 0) // HEAD_DIM
    bd = jnp.where(grp == grp.T, 1.0 / HEAD_DIM, 0.0).astype(BF16)
    qg = jnp.tile(q_norm_g[0], N_Q_HEADS).reshape(1, -1)
    kg = jnp.tile(k_norm_g[0], N_KV_HEADS).reshape(1, -1)
    cos_t, sin_t = _rope_lane_tables(s)
    q, k, v, p = _l0_in_proj(x, m_lat(0, 0), m_lat(0, 1), w_in_bf, bd, qg, kg, cos_t, sin_t, ts)
    ones = jnp.ones((lc, ATTN_WIDTH), F32)
    _, kc, vc, _ = _l0_in_proj(ctx, m_ctx(0, 0), m_ctx(0, 1), w_in_bf, bd, qg, kg, ones, 0.0 * ones, lc)
    k_all = jnp.concatenate([kc, k], axis=1)
    v_t = jnp.swapaxes(jnp.concatenate([vc, v], axis=1), 1, 2).reshape(b, N_KV_HEADS, HEAD_DIM, lc + s)
    v_t = jnp.concatenate([v_t, jnp.ones((b, N_KV_HEADS, _VT_ROWS - HEAD_DIM, lc + s), BF16)], axis=2)
    score_bound = (1.02 * HEAD_DIM ** 0.5 * LOG2_E) * jnp.max(jnp.abs(q_norm_g[0])) * jnp.max(jnp.abs(k_norm_g[0]))
    a = _attention(q, k_all, v_t, score_bound, 512)
    h, u, aff = _l0_out_proj(a, p, w_pool_grp[0].astype(BF16), vec(pool_scale[0]), w_mix_out[0].astype(BF16),
                             x, m_lat(0, 2), vec(ln_mix_g[0]), vec(ln_mix_b[0]), m_lat(0, 3), m_lat(0, 4),
                             _router_weights(w_router[0]), ts)
    f = _moe(u, aff, 0, w_exp_gate, w_exp_up, w_exp_down)

    h, bg, cx = _l1_in_proj(h, f, m_lat(0, 5), vec(ln_ffn_g[0]), vec(ln_ffn_b[0]), m_lat(1, 0), m_lat(1, 1),
                            w_conv_in[0].astype(BF16), ts)
    h, u, aff = _l1_out_proj(cx, bg, conv_w[0], w_conv_out[0].astype(BF16),
                             h, m_lat(1, 2), vec(ln_mix_g[1]), vec(ln_mix_b[1]), m_lat(1, 3), m_lat(1, 4),
                             _router_weights(w_router[1]), ts)
    f = _moe(u, aff, 1, w_exp_gate, w_exp_up, w_exp_down)
    return _final_norm(h, f, m_lat(1, 5), vec(ln_ffn_g[1]), vec(ln_ffn_b[1]), 2 * ts)
```

```python
import functools

import jax
import jax.numpy as jnp
from jax import lax
from jax.experimental import pallas as pl
from jax.experimental.pallas import tpu as pltpu

F32, BF16, I32 = jnp.float32, jnp.bfloat16, jnp.int32

D_MODEL = 1024
DEPTH = 2
GRID_W = 64
N_Q_HEADS = 8
N_KV_HEADS = 2
HEAD_DIM = 64
ATTN_WIDTH = N_Q_HEADS * HEAD_DIM
KV_WIDTH = N_KV_HEADS * HEAD_DIM
ROPE_THETA = 10000.0
POOL_WINDOWS = (2, 4, 8, 16)
POOL_GROUP_DIM = 128
POOL_WIDTH = len(POOL_WINDOWS) * POOL_GROUP_DIM
MIX_IN_WIDTH = ATTN_WIDTH + 2 * KV_WIDTH + POOL_WIDTH
N_EXPERTS = 16
EXPERT_FF = 1024
CAPACITY_FACTOR = 2
NORM_EPS = 1e-6
DEEPNORM_ALPHA = (2 * DEPTH) ** 0.25
LOG2_E = 1.4426950408889634

LANES = 128
SUBLANES = 8
HALO = SUBLANES
VMEM_LIMIT = 56 * 1024 * 1024

_NT = (((1,), (1,)), ((), ()))


def _cparams(sem):
    return pltpu.CompilerParams(dimension_semantics=sem, vmem_limit_bytes=VMEM_LIMIT)


TOKEN_TILE_ROWS = D_MODEL // LANES


def _store_token_major(ref, rows):
    n = rows.shape[0]
    for j in range(TOKEN_TILE_ROWS):
        ref[pl.ds(j, n, stride=TOKEN_TILE_ROWS), :] = rows[:, j * LANES:(j + 1) * LANES]


def _load_token_major(ref, n):
    return jnp.concatenate(
        [ref[pl.ds(j, n, stride=TOKEN_TILE_ROWS), :] for j in range(TOKEN_TILE_ROWS)], axis=1)


def _layer_norm(z, g, b):
    mu = jnp.mean(z, axis=-1, keepdims=True)
    zc = z - mu
    var = jnp.mean(zc * zc, axis=-1, keepdims=True)
    return zc * lax.rsqrt(var + NORM_EPS) * g + b


def _split_bf16(t):
    hi = t.astype(BF16)
    return hi, (t - hi.astype(F32)).astype(BF16)


def _mod_kernel(c_ref, w_ref, b_ref, o_ref):
    c = c_ref[...]
    c_hi, c_lo = _split_bf16(c * jax.nn.sigmoid(c))
    w_hi, w_lo = _split_bf16(w_ref[...])
    dot = functools.partial(jnp.dot, preferred_element_type=F32)
    o_ref[...] = dot(c_hi, w_hi) + dot(c_lo, w_hi) + dot(c_hi, w_lo) + b_ref[...]


def _modulation(cond_rows, w_mod, b_mod):
    n_out = w_mod.shape[-1]
    tn = D_MODEL
    return pl.pallas_call(
        _mod_kernel,
        grid=(DEPTH, n_out // tn),
        in_specs=[
            pl.BlockSpec((SUBLANES, D_MODEL), lambda l, j: (0, 0)),
            pl.BlockSpec((None, D_MODEL, tn), lambda l, j: (l, 0, j)),
            pl.BlockSpec((None, 1, tn), lambda l, j: (l, 0, j)),
        ],
        out_specs=pl.BlockSpec((None, SUBLANES, tn), lambda l, j: (l, 0, j)),
        out_shape=jax.ShapeDtypeStruct((DEPTH, SUBLANES, n_out), F32),
        compiler_params=_cparams(("arbitrary", "arbitrary")),
        name="modulation",
    )(cond_rows, w_mod, b_mod.reshape(DEPTH, 1, n_out))


def _group_mean_sq(t, bd):
    hi, lo = _split_bf16(t * t)
    return (jnp.dot(hi, bd, preferred_element_type=F32) + jnp.dot(lo, bd, preferred_element_type=F32))


def _rope_chunk(y, c, s_signed):
    n = y.shape[-1]
    lane = lax.broadcasted_iota(I32, y.shape, 1)
    nxt = pltpu.roll(y, n - 1, axis=1)
    prv = pltpu.roll(y, 1, axis=1)
    partner = jnp.where((lane & 1) == 0, nxt, prv)
    return y * c + partner * s_signed


def _l0_in_kernel(x_ref, sh_ref, sc_ref, w_ref, bd_ref, qg_ref, kg_ref, cos_ref, sin_ref,
                  q_ref, k_ref, v_ref, p_ref):
    u = x_ref[...] * (1.0 + sc_ref[...]) + sh_ref[...]
    h = jnp.dot(u.astype(BF16), w_ref[...], preferred_element_type=F32)
    q = h[:, :ATTN_WIDTH]
    k = h[:, ATTN_WIDTH:ATTN_WIDTH + KV_WIDTH]
    bd = bd_ref[...]
    qn = q * lax.rsqrt(_group_mean_sq(q, bd) + NORM_EPS) * qg_ref[...]
    kn = k * lax.rsqrt(_group_mean_sq(k, bd[:KV_WIDTH, :KV_WIDTH]) + NORM_EPS) * kg_ref[...]
    scale = HEAD_DIM ** -0.5 * LOG2_E
    heads_per_kv = N_Q_HEADS // N_KV_HEADS
    lane = lax.broadcasted_iota(I32, (x_ref.shape[0], LANES), 1)
    for c in range(ATTN_WIDTH // LANES):
        sl = slice(c * LANES, (c + 1) * LANES)
        pair = _rope_chunk(qn[:, sl], cos_ref[:, sl], sin_ref[:, sl]) * scale
        swapped = pltpu.roll(pair, HEAD_DIM, axis=1)
        g = (2 * c) // heads_per_kv
        keep = (lane < HEAD_DIM) if g == 0 else (lane >= HEAD_DIM)
        first, second = (pair, swapped) if g == 0 else (swapped, pair)
        q_ref[:, (2 * c) * LANES:(2 * c + 1) * LANES] = jnp.where(keep, first, 0.0).astype(BF16)
        q_ref[:, (2 * c + 1) * LANES:(2 * c + 2) * LANES] = jnp.where(keep, second, 0.0).astype(BF16)
    k_ref[...] = _rope_chunk(kn, cos_ref[:, :KV_WIDTH], sin_ref[:, :KV_WIDTH]).astype(BF16)
    v_ref[...] = h[:, ATTN_WIDTH + KV_WIDTH:ATTN_WIDTH + 2 * KV_WIDTH].astype(BF16)
    p_ref[...] = h[:, ATTN_WIDTH + 2 * KV_WIDTH:]


def _l0_in_proj(x, shift, scale, w_in_bf, bd, qg, kg, cos_t, sin_t, ts):
    b, l, _ = x.shape
    per_batch = shift.shape[0] == b
    mod_map = (lambda bi, si: (bi, 0, 0)) if per_batch else (lambda bi, si: (0, 0, 0))
    full = lambda bi, si: (0, 0)
    return pl.pallas_call(
        _l0_in_kernel,
        grid=(b, l // ts),
        in_specs=[
            pl.BlockSpec((None, ts, D_MODEL), lambda bi, si: (bi, si, 0)),
            pl.BlockSpec((None, 1, D_MODEL), mod_map),
            pl.BlockSpec((None, 1, D_MODEL), mod_map),
            pl.BlockSpec((D_MODEL, MIX_IN_WIDTH), full),
            pl.BlockSpec((ATTN_WIDTH, ATTN_WIDTH), full),
            pl.BlockSpec((1, ATTN_WIDTH), full),
            pl.BlockSpec((1, KV_WIDTH), full),
            pl.BlockSpec((ts, ATTN_WIDTH), lambda bi, si: (si, 0)),
            pl.BlockSpec((ts, ATTN_WIDTH), lambda bi, si: (si, 0)),
        ],
        out_specs=[
            pl.BlockSpec((None, ts, N_Q_HEADS * LANES), lambda bi, si: (bi, si, 0)),
            pl.BlockSpec((None, ts, KV_WIDTH), lambda bi, si: (bi, si, 0)),
            pl.BlockSpec((None, ts, KV_WIDTH), lambda bi, si: (bi, si, 0)),
            pl.BlockSpec((None, ts, POOL_WIDTH), lambda bi, si: (bi, si, 0)),
        ],
        out_shape=[
            jax.ShapeDtypeStruct((b, l, N_Q_HEADS * LANES), BF16),
            jax.ShapeDtypeStruct((b, l, KV_WIDTH), BF16),
            jax.ShapeDtypeStruct((b, l, KV_WIDTH), BF16),
            jax.ShapeDtypeStruct((b, l, POOL_WIDTH), F32),
        ],
        compiler_params=_cparams(("arbitrary", "arbitrary")),
        name="l0_in_proj",
    )(x, shift, scale, w_in_bf, bd, qg, kg, cos_t, sin_t)


_VT_ROWS = HEAD_DIM + 16


_KV_CHUNK = 2176
_MAX_ROWS = 64


def _attn_kernel(q_ref, k_ref, vt_ref, o_ref, ot_ref, *s_refs):
    heads_per_kv = N_Q_HEADS // N_KV_HEADS
    lk, tq = k_ref.shape[0], q_ref.shape[0]
    chunks = [(c0, min(_KV_CHUNK, lk - c0)) for c0 in range(0, lk, _KV_CHUNK)]

    def scores(h, c0, n, m):
        qh = q_ref[:, h * LANES:(h + 1) * LANES]
        s_c = lax.dot_general(k_ref[c0:c0 + n, :], qh, _NT, preferred_element_type=F32)
        s_refs[h % 2][c0:c0 + n, :] = s_c
        mc = jnp.max(s_c.reshape(n // _MAX_ROWS, _MAX_ROWS, tq), axis=0)
        return mc if m is None else jnp.maximum(m, mc)

    def weighted(h, c0, n, m, o_t):
        p_c = jnp.exp2(s_refs[h % 2][c0:c0 + n, :] - m).astype(BF16)
        o_c = jnp.dot(vt_ref[h // heads_per_kv, :, c0:c0 + n], p_c, preferred_element_type=F32)
        return o_c if o_t is None else o_t + o_c

    m_next = None
    for c0, n in chunks:
        m_next = scores(0, c0, n, m_next)
    for h in range(N_Q_HEADS):
        m = jnp.max(m_next, axis=0, keepdims=True)
        m_next = None
        o_t = None
        for c0, n in chunks:
            o_t = weighted(h, c0, n, m, o_t)
            if h + 1 < N_Q_HEADS:
                m_next = scores(h + 1, c0, n, m_next)
        ot_ref[h * HEAD_DIM:(h + 1) * HEAD_DIM, :] = (
            o_t[:HEAD_DIM, :] / o_t[HEAD_DIM:HEAD_DIM + 1, :])
    o_ref[...] = ot_ref[...].T.astype(BF16)


_BOUNDED_KV_CHUNK = 2176


def _attn_bounded_kernel(bound_ref, q_ref, k_ref, vt_ref, o_ref, ot_ref):
    heads_per_kv = N_Q_HEADS // N_KV_HEADS
    lk = k_ref.shape[0]
    shift = bound_ref[0]
    for h in range(N_Q_HEADS):
        g = h // heads_per_kv
        qh = q_ref[:, h * LANES:(h + 1) * LANES]
        o_t = None
        for c0 in range(0, lk, _BOUNDED_KV_CHUNK):
            n = min(_BOUNDED_KV_CHUNK, lk - c0)
            s_c = lax.dot_general(k_ref[c0:c0 + n, :], qh, _NT, preferred_element_type=F32)
            p_c = jnp.exp2(s_c - shift).astype(BF16)
            o_c = jnp.dot(vt_ref[g, :, c0:c0 + n], p_c, preferred_element_type=F32)
            o_t = o_c if o_t is None else o_t + o_c
        ot_ref[h * HEAD_DIM:(h + 1) * HEAD_DIM, :] = (
            o_t[:HEAD_DIM, :] / o_t[HEAD_DIM:HEAD_DIM + 1, :])
    o_ref[...] = ot_ref[...].T.astype(BF16)


_MAX_SAFE_SCORE_BOUND = 40.0


def _attention(q, k_all, vt, score_bound, tq):
    b, s, qw = q.shape
    lk = k_all.shape[1]
    specs = dict(
        grid=(b, s // tq),
        out_specs=pl.BlockSpec((None, tq, ATTN_WIDTH), lambda bi, qi: (bi, qi, 0)),
        out_shape=jax.ShapeDtypeStruct((b, s, ATTN_WIDTH), BF16),
        compiler_params=_cparams(("arbitrary", "arbitrary")),
    )
    in_specs = [
        pl.BlockSpec((None, tq, qw), lambda bi, qi: (bi, qi, 0)),
        pl.BlockSpec((None, lk, KV_WIDTH), lambda bi, qi: (bi, 0, 0)),
        pl.BlockSpec((None, N_KV_HEADS, _VT_ROWS, lk), lambda bi, qi: (bi, 0, 0, 0)),
    ]

    def bounded(_):
        return pl.pallas_call(
            _attn_bounded_kernel,
            in_specs=[pl.BlockSpec(memory_space=pltpu.SMEM)] + in_specs,
            scratch_shapes=[pltpu.VMEM((ATTN_WIDTH, tq), F32)],
            name="attention_bounded", **specs,
        )(score_bound.reshape(1), q, k_all, vt)

    def exact(_):
        return pl.pallas_call(
            _attn_kernel,
            in_specs=in_specs,
            scratch_shapes=[pltpu.VMEM((ATTN_WIDTH, tq), F32), pltpu.VMEM((lk, tq), F32),
                            pltpu.VMEM((lk, tq), F32)],
            name="attention", **specs,
        )(q, k_all, vt)

    return lax.cond(score_bound < _MAX_SAFE_SCORE_BOUND, bounded, exact, None)


def _router_weights(w_router_layer):
    hi, lo = _split_bf16(w_router_layer.T)
    return jnp.concatenate([hi, lo], axis=0)


def _post_tail(resid, y, gate, lng, lnb, shf, scf, wr2):
    h = _layer_norm(DEEPNORM_ALPHA * resid + gate * y, lng, lnb)
    u = h * (1.0 + scf) + shf
    u_hi, u_lo = _split_bf16(u)
    first = lax.dot_general(wr2, u_hi, _NT, preferred_element_type=F32)
    second = lax.dot_general(wr2[:N_EXPERTS], u_lo, _NT, preferred_element_type=F32)
    logits = first[:N_EXPERTS] + first[N_EXPERTS:] + second
    ex = jnp.exp(logits - jnp.max(logits, axis=0, keepdims=True))
    aff = ex / jnp.sum(ex, axis=0, keepdims=True)
    return h, u, aff


def _fill_halo(ext_ref, prev_ref, main_ref, next_ref, ts):
    st = pl.program_id(1)
    last = pl.num_programs(1) - 1
    ext_ref[0:HALO, :] = jnp.where(st > 0, prev_ref[...], 0.0)
    ext_ref[HALO:HALO + ts, :] = main_ref[...]
    ext_ref[HALO + ts:HALO + ts + HALO, :] = jnp.where(st < last, next_ref[...], 0.0)


def _halo_specs(ts, width, seq):
    r = ts // HALO
    nblk = seq // HALO
    return [
        pl.BlockSpec((None, ts, width), lambda bi, si: (bi, si, 0)),
        pl.BlockSpec((None, HALO, width), lambda bi, si: (bi, jnp.maximum(si * r - 1, 0), 0)),
        pl.BlockSpec((None, HALO, width), lambda bi, si: (bi, jnp.minimum((si + 1) * r, nblk - 1), 0)),
    ]


def _tail_specs(ts):
    row = lambda bi, si: (bi, 0, 0)
    full = lambda bi, si: (0, 0)
    in_specs = [
        pl.BlockSpec((None, ts, D_MODEL), lambda bi, si: (bi, si, 0)),
        pl.BlockSpec((None, 1, D_MODEL), row),
        pl.BlockSpec((1, D_MODEL), full),
        pl.BlockSpec((1, D_MODEL), full),
        pl.BlockSpec((None, 1, D_MODEL), row),
        pl.BlockSpec((None, 1, D_MODEL), row),
        pl.BlockSpec((2 * N_EXPERTS, D_MODEL), full),
    ]
    out_specs = [
        pl.BlockSpec((None, ts, D_MODEL), lambda bi, si: (bi, si, 0)),
        pl.BlockSpec((None, ts * TOKEN_TILE_ROWS, LANES), lambda bi, si: (bi, si, 0)),
        pl.BlockSpec((None, N_EXPERTS, ts), lambda bi, si: (bi, 0, si)),
    ]
    return in_specs, out_specs


def _tail_out_shape(b, s):
    return [jax.ShapeDtypeStruct((b, s, D_MODEL), F32),
            jax.ShapeDtypeStruct((b, s * TOKEN_TILE_ROWS, LANES), F32),
            jax.ShapeDtypeStruct((b, N_EXPERTS, s), F32)]


def _l0_out_kernel(seq, a_ref, p_ref, pprev_ref, pnext_ref, wpool_ref, pscale_ref, wout_ref,
                   x_ref, gate_ref, lng_ref, lnb_ref, shf_ref, scf_ref, wr_ref,
                   h_ref, u_ref, aff_ref, pext_ref):
    ts = p_ref.shape[0]
    _fill_halo(pext_ref, pprev_ref, p_ref, pnext_ref, ts)
    t = pl.program_id(1) * ts + lax.broadcasted_iota(I32, (ts, 1), 0)
    outs = []
    for g, w in enumerate(POOL_WINDOWS):
        sl = slice(g * POOL_GROUP_DIM, (g + 1) * POOL_GROUP_DIM)
        fwd = pext_ref[:, sl]
        n_ext = fwd.shape[0]
        span = 1
        while span < w // 2:
            fwd = fwd + pltpu.roll(fwd, n_ext - span, axis=0)
            span *= 2
        acc = (fwd + pltpu.roll(fwd, w // 2, axis=0))[HALO:HALO + ts, :]
        cnt = jnp.minimum(t + w // 2, seq) - jnp.maximum(t - w // 2, 0)
        pooled = acc / cnt.astype(F32) - p_ref[:, sl]
        og = jnp.dot(pooled.astype(BF16), wpool_ref[g], preferred_element_type=F32)
        outs.append((og * pscale_ref[:, sl]).astype(BF16))
    pool = jnp.concatenate(outs, axis=-1)
    y = (jnp.dot(a_ref[...], wout_ref[:ATTN_WIDTH, :], preferred_element_type=F32)
         + jnp.dot(pool, wout_ref[ATTN_WIDTH:, :], preferred_element_type=F32))
    h, u, aff = _post_tail(x_ref[...], y, gate_ref[...], lng_ref[...], lnb_ref[...],
                           shf_ref[...], scf_ref[...], wr_ref[...])
    h_ref[...] = h
    _store_token_major(u_ref, u)
    aff_ref[...] = aff


def _l0_out_proj(a, p, wpool_bf, pscale, wout_bf, x, gate, lng, lnb, shf, scf, wr_t, ts):
    b, s, _ = x.shape
    full2 = lambda bi, si: (0, 0)
    tail_in, tail_out = _tail_specs(ts)
    in_specs = (
        [pl.BlockSpec((None, ts, ATTN_WIDTH), lambda bi, si: (bi, si, 0))]
        + _halo_specs(ts, POOL_WIDTH, s)
        + [pl.BlockSpec((len(POOL_WINDOWS), POOL_GROUP_DIM, POOL_GROUP_DIM), lambda bi, si: (0, 0, 0)),
           pl.BlockSpec((1, POOL_WIDTH), full2),
           pl.BlockSpec((ATTN_WIDTH + POOL_WIDTH, D_MODEL), full2)]
        + tail_in)
    return pl.pallas_call(
        functools.partial(_l0_out_kernel, s),
        grid=(b, s // ts),
        in_specs=in_specs,
        out_specs=tail_out,
        out_shape=_tail_out_shape(b, s),
        scratch_shapes=[pltpu.VMEM((ts + 2 * HALO, POOL_WIDTH), F32)],
        compiler_params=_cparams(("arbitrary", "arbitrary")),
        name="l0_out_proj",
    )(a, p, p, p, wpool_bf, pscale, wout_bf, x, gate, lng, lnb, shf, scf, wr_t)


def _exclusive_cumsum_lanes(mask, tri):
    n = mask.shape[1]
    w = tri.shape[0]
    carry = jnp.zeros((mask.shape[0], 1), F32)
    parts = []
    for c in range(n // w):
        blk = mask[:, c * w:(c + 1) * w]
        inc = jnp.dot(blk.astype(BF16), tri, preferred_element_type=F32)
        parts.append(inc - blk + carry)
        carry = carry + inc[:, w - 1:w]
    return jnp.concatenate(parts, axis=1)


_TOKEN_SPLIT = 64
_SLOT_LO = 32
_PACKED_VALUES = 5


def _select_kernel(cap, aff_ref, out_ref):
    aff = aff_ref[...]
    e = aff.shape[0]
    as_f32 = lambda word: pltpu.bitcast(word, F32)

    def refine(i, thr):
        cand = thr | jnp.left_shift(jnp.int32(1), 30 - i)
        cnt = jnp.sum(jnp.where(aff >= as_f32(cand), 1.0, 0.0), axis=1, keepdims=True)
        return jnp.where(cnt >= cap, cand, thr)

    thr = lax.fori_loop(0, 31, refine, jnp.zeros((e, 1), I32))
    w = 2 * LANES
    r = lax.broadcasted_iota(I32, (w, w), 0)
    c = lax.broadcasted_iota(I32, (w, w), 1)
    tri = jnp.where(r <= c, 1.0, 0.0).astype(BF16)
    gt = jnp.where(aff >= as_f32(thr + 1), 1.0, 0.0)
    eq = jnp.where(aff >= as_f32(thr), 1.0, 0.0) - gt
    need = cap - jnp.sum(gt, axis=1, keepdims=True)
    sel = gt + eq * jnp.where(_exclusive_cumsum_lanes(eq, tri) < need, 1.0, 0.0)
    pos = _exclusive_cumsum_lanes(sel, tri)
    dest = jnp.where(sel > 0.5, pos, float(cap)).astype(I32)

    n = aff.shape[1]
    n_hi = cap // _SLOT_LO
    d_hi = dest // _SLOT_LO
    d_lo = dest % _SLOT_LO
    t = lax.broadcasted_iota(I32, (1, n), 1)
    t_hi = (t // _TOKEN_SPLIT).astype(F32)
    t_lo = (t % _TOKEN_SPLIT).astype(F32)
    a1 = aff.astype(BF16).astype(F32)
    r1 = aff - a1
    a2 = r1.astype(BF16).astype(F32)
    a3 = r1 - a2
    row_hi = lax.broadcasted_iota(I32, (n_hi, n), 0)
    row_lo = lax.broadcasted_iota(I32, (_SLOT_LO, n), 0)
    for ei in range(e):
        rs = slice(ei, ei + 1)
        h = jnp.where(d_hi[rs] == row_hi, 1.0, 0.0)
        l = jnp.where(d_lo[rs] == row_lo, 1.0, 0.0).astype(BF16)
        vals = jnp.concatenate([h * t_hi, h * t_lo, h * a1[rs], h * a2[rs], h * a3[rs]], axis=0)
        out_ref[ei] = lax.dot_general(vals.astype(BF16), l, _NT, preferred_element_type=F32)


def _select(aff, cap):
    b, e, n = aff.shape
    rows = _PACKED_VALUES * (cap // _SLOT_LO)
    return pl.pallas_call(
        functools.partial(_select_kernel, cap),
        grid=(b,),
        in_specs=[pl.BlockSpec((None, e, n), lambda bi: (bi, 0, 0))],
        out_specs=pl.BlockSpec((None, e, rows, _SLOT_LO), lambda bi: (bi, 0, 0, 0)),
        out_shape=jax.ShapeDtypeStruct((b, e, rows, _SLOT_LO), F32),
        compiler_params=_cparams(("arbitrary",)),
        name="expert_select",
    )(aff)


def _token_tile(t):
    return pl.ds(pl.multiple_of(t * TOKEN_TILE_ROWS, TOKEN_TILE_ROWS), TOKEN_TILE_ROWS)


_GATHER_AHEAD = 2
_FF_BLOCK = 512


def _expert_kernel(cap, seq, idx_ref, u_hbm, wg_ref, wu_ref, wd_ref, y_ref,
                   xe_ref, wg_bf, wu_bf, wd_bf, sem):
    b = pl.program_id(1)
    nb = pl.num_programs(1)
    step = pl.program_id(0) * nb + b
    nsteps = pl.num_programs(0) * nb
    nbuf = _GATHER_AHEAD + 1
    slot = step % nbuf

    def start_gather(k, into, rolled):
        ek = k // nb
        bk = k % nb
        base = (bk * N_EXPERTS + ek) * cap
        tok0 = bk * seq

        def start(i, priority):
            tok = tok0 + idx_ref[base + i]
            pltpu.make_async_copy(u_hbm.at[_token_tile(tok), :], xe_ref.at[into, _token_tile(i), :],
                                  sem.at[into]).start(priority=priority)

        if rolled:
            def issue(ci, carry):
                for j in range(SUBLANES):
                    start(ci * SUBLANES + j, j % 2)
                return carry

            lax.fori_loop(0, cap // SUBLANES, issue, 0)
        else:
            for i in range(cap):
                start(i, i % 2)

    def wait_gather(into):
        pltpu.make_async_copy(u_hbm.at[pl.ds(0, cap * TOKEN_TILE_ROWS), :], xe_ref.at[into], sem.at[into]).wait()

    @pl.when(step == 0)
    def _():
        for k in range(_GATHER_AHEAD):
            start_gather(k, k, rolled=True)

    @pl.when(b == 0)
    def _():
        wg_bf[...] = wg_ref[...].astype(BF16)
        wu_bf[...] = wu_ref[...].astype(BF16)
        wd_bf[...] = wd_ref[...].astype(BF16)

    wait_gather(slot)
    x = _load_token_major(xe_ref.at[slot], cap).astype(BF16)
    start_gather(jnp.minimum(step + _GATHER_AHEAD, nsteps - 1), (step + _GATHER_AHEAD) % nbuf, rolled=False)
    blocks = [slice(f0, f0 + _FF_BLOCK) for f0 in range(0, EXPERT_FF, _FF_BLOCK)]
    pre = [(jnp.dot(x, wg_bf[:, fs], preferred_element_type=F32),
            jnp.dot(x, wu_bf[:, fs], preferred_element_type=F32)) for fs in blocks]
    y = None
    for fs, (hg, hu) in zip(blocks, pre):
        act = (hg * jax.nn.sigmoid(hg) * hu).astype(BF16)
        part = jnp.dot(act, wd_bf[fs, :], preferred_element_type=F32)
        y = part if y is None else y + part
    _store_token_major(y_ref, y)

    @pl.when(step == nsteps - 1)
    def _():
        for k in range(1, nbuf):
            wait_gather((step + k) % nbuf)


def _expert_ffn(idx_flat, u_rows, layer, w_g, w_u, w_d, batch, seq, cap):
    wspec = lambda: pl.BlockSpec((None, None, D_MODEL, EXPERT_FF), lambda e, b, idx: (layer, e, 0, 0))
    grid_spec = pltpu.PrefetchScalarGridSpec(
        num_scalar_prefetch=1,
        grid=(N_EXPERTS, batch),
        in_specs=[
            pl.BlockSpec(memory_space=pl.ANY),
            wspec(), wspec(),
            pl.BlockSpec((None, None, EXPERT_FF, D_MODEL), lambda e, b, idx: (layer, e, 0, 0)),
        ],
        out_specs=pl.BlockSpec((None, None, cap * TOKEN_TILE_ROWS, LANES), lambda e, b, idx: (b, e, 0, 0)),
        scratch_shapes=[
            pltpu.VMEM((_GATHER_AHEAD + 1, cap * TOKEN_TILE_ROWS, LANES), F32),
            pltpu.VMEM((D_MODEL, EXPERT_FF), BF16),
            pltpu.VMEM((D_MODEL, EXPERT_FF), BF16),
            pltpu.VMEM((EXPERT_FF, D_MODEL), BF16),
            pltpu.SemaphoreType.DMA((_GATHER_AHEAD + 1,)),
        ],
    )
    return pl.pallas_call(
        functools.partial(_expert_kernel, cap, seq),
        grid_spec=grid_spec,
        out_shape=jax.ShapeDtypeStruct((batch, N_EXPERTS, cap * TOKEN_TILE_ROWS, LANES), F32),
        compiler_params=_cparams(("arbitrary", "arbitrary")),
        name="expert_ffn",
    )(idx_flat, u_rows, w_g, w_u, w_d)


_COMBINE_UNROLL = 16


def _combine_kernel(cap, idx_ref, gate_ref, y_ref, f_ref):
    b = pl.program_id(0)
    e = pl.program_id(1)

    @pl.when(e == 0)
    def _():
        f_ref[...] = jnp.zeros_like(f_ref)

    base = (b * N_EXPERTS + e) * cap

    def chunk(ci, carry):
        i0 = ci * _COMBINE_UNROLL
        tiles = [_token_tile(idx_ref[base + i0 + j]) for j in range(_COMBINE_UNROLL)]
        sums = [f_ref[tiles[j], :] + y_ref[_token_tile(i0 + j), :] * gate_ref[base + i0 + j]
                for j in range(_COMBINE_UNROLL)]
        for j in range(_COMBINE_UNROLL):
            f_ref[tiles[j], :] = sums[j]
        return carry

    lax.fori_loop(0, cap // _COMBINE_UNROLL, chunk, 0)


def _combine(idx_flat, gate_flat, y, seq, cap):
    batch = y.shape[0]
    grid_spec = pltpu.PrefetchScalarGridSpec(
        num_scalar_prefetch=2,
        grid=(batch, N_EXPERTS),
        in_specs=[pl.BlockSpec((None, None, cap * TOKEN_TILE_ROWS, LANES), lambda b, e, idx, gate: (b, e, 0, 0))],
        out_specs=pl.BlockSpec((None, seq * TOKEN_TILE_ROWS, LANES), lambda b, e, idx, gate: (b, 0, 0)),
    )
    return pl.pallas_call(
        functools.partial(_combine_kernel, cap),
        grid_spec=grid_spec,
        out_shape=jax.ShapeDtypeStruct((batch, seq * TOKEN_TILE_ROWS, LANES), F32),
        compiler_params=_cparams(("arbitrary", "arbitrary")),
        name="expert_combine",
    )(idx_flat, gate_flat, y)


def _moe(u, aff, layer, w_g, w_u, w_d):
    b, _, n = aff.shape
    cap = CAPACITY_FACTOR * n // N_EXPERTS
    packed = _select(aff, cap).reshape(b, N_EXPERTS, _PACKED_VALUES, cap)
    idx_flat = (packed[:, :, 0] * _TOKEN_SPLIT + packed[:, :, 1]).astype(I32).reshape(-1)
    gate_flat = (packed[:, :, 2] + packed[:, :, 3] + packed[:, :, 4]).reshape(-1)
    y = _expert_ffn(idx_flat, u.reshape(b * n * TOKEN_TILE_ROWS, LANES), layer, w_g, w_u, w_d, b, n, cap)
    return _combine(idx_flat, gate_flat, y, n, cap)


def _l1_in_kernel(h_ref, f_ref, gf_ref, lng_ref, lnb_ref, sh_ref, sc_ref, w_ref, h2_ref, bg_ref, cx_ref):
    f = _load_token_major(f_ref, h_ref.shape[0])
    h2 = _layer_norm(DEEPNORM_ALPHA * h_ref[...] + gf_ref[...] * f, lng_ref[...], lnb_ref[...])
    h2_ref[...] = h2
    u = h2 * (1.0 + sc_ref[...]) + sh_ref[...]
    z = jnp.dot(u.astype(BF16), w_ref[...], preferred_element_type=F32)
    bg_ref[...] = z[:, :D_MODEL]
    cx_ref[...] = z[:, D_MODEL:2 * D_MODEL] * z[:, 2 * D_MODEL:]


def _l1_in_proj(h, f, gate_f, lng, lnb, shift, scale, w_bf, ts):
    b, s, _ = h.shape
    tok = lambda: pl.BlockSpec((None, ts, D_MODEL), lambda bi, si: (bi, si, 0))
    row = lambda: pl.BlockSpec((None, 1, D_MODEL), lambda bi, si: (bi, 0, 0))
    vec = lambda: pl.BlockSpec((1, D_MODEL), lambda bi, si: (0, 0))
    return pl.pallas_call(
        _l1_in_kernel,
        grid=(b, s // ts),
        in_specs=[tok(), pl.BlockSpec((None, ts * TOKEN_TILE_ROWS, LANES), lambda bi, si: (bi, si, 0)),
                  row(), vec(), vec(), row(), row(),
                  pl.BlockSpec((D_MODEL, 3 * D_MODEL), lambda bi, si: (0, 0))],
        out_specs=[tok(), tok(), tok()],
        out_shape=[jax.ShapeDtypeStruct((b, s, D_MODEL), F32)] * 3,
        compiler_params=_cparams(("arbitrary", "arbitrary")),
        name="l1_in_proj",
    )(h, f, gate_f, lng, lnb, shift, scale, w_bf)


def _l1_out_kernel(cx_ref, cprev_ref, cnext_ref, bg_ref, cw_ref, wout_ref,
                   x_ref, gate_ref, lng_ref, lnb_ref, shf_ref, scf_ref, wr_ref,
                   h_ref, u_ref, aff_ref, cext_ref):
    ts = cx_ref.shape[0]
    _fill_halo(cext_ref, cprev_ref, cx_ref, cnext_ref, ts)
    z = (cext_ref[HALO - 1:HALO - 1 + ts, :] * cw_ref[0:1, :]
         + cx_ref[...] * cw_ref[1:2, :]
         + cext_ref[HALO + 1:HALO + 1 + ts, :] * cw_ref[2:3, :])
    y = jnp.dot((bg_ref[...] * z).astype(BF16), wout_ref[...], preferred_element_type=F32)
    h, u, aff = _post_tail(x_ref[...], y, gate_ref[...], lng_ref[...], lnb_ref[...],
                           shf_ref[...], scf_ref[...], wr_ref[...])
    h_ref[...] = h
    _store_token_major(u_ref, u)
    aff_ref[...] = aff


def _l1_out_proj(cx, bg, conv_w, wout_bf, x, gate, lng, lnb, shf, scf, wr_t, ts):
    b, s, _ = x.shape
    tail_in, tail_out = _tail_specs(ts)
    in_specs = (
        _halo_specs(ts, D_MODEL, s)
        + [pl.BlockSpec((None, ts, D_MODEL), lambda bi, si: (bi, si, 0)),
           pl.BlockSpec(conv_w.shape, lambda bi, si: (0, 0)),
           pl.BlockSpec((D_MODEL, D_MODEL), lambda bi, si: (0, 0))]
        + tail_in)
    return pl.pallas_call(
        _l1_out_kernel,
        grid=(b, s // ts),
        in_specs=in_specs,
        out_specs=tail_out,
        out_shape=_tail_out_shape(b, s),
        scratch_shapes=[pltpu.VMEM((ts + 2 * HALO, D_MODEL), F32)],
        compiler_params=_cparams(("arbitrary", "arbitrary")),
        name="l1_out_proj",
    )(cx, cx, cx, bg, conv_w, wout_bf, x, gate, lng, lnb, shf, scf, wr_t)


def _final_kernel(h_ref, f_ref, gf_ref, lng_ref, lnb_ref, o_ref):
    f = _load_token_major(f_ref, h_ref.shape[0])
    o_ref[...] = _layer_norm(DEEPNORM_ALPHA * h_ref[...] + gf_ref[...] * f, lng_ref[...], lnb_ref[...])


def _final_norm(h, f, gate_f, lng, lnb, ts):
    b, s, _ = h.shape
    tok = lambda: pl.BlockSpec((None, ts, D_MODEL), lambda bi, si: (bi, si, 0))
    return pl.pallas_call(
        _final_kernel,
        grid=(b, s // ts),
        in_specs=[tok(), pl.BlockSpec((None, ts * TOKEN_TILE_ROWS, LANES), lambda bi, si: (bi, si, 0)),
                  pl.BlockSpec((None, 1, D_MODEL), lambda bi, si: (bi, 0, 0)),
                  pl.BlockSpec((1, D_MODEL), lambda bi, si: (0, 0)),
                  pl.BlockSpec((1, D_MODEL), lambda bi, si: (0, 0))],
        out_specs=tok(),
        out_shape=jax.ShapeDtypeStruct((b, s, D_MODEL), F32),
        compiler_params=_cparams(("arbitrary", "arbitrary")),
        name="final_norm",
    )(h, f, gate_f, lng, lnb)


def _rope_lane_tables(seq):
    rows = seq // GRID_W
    row = jnp.repeat(jnp.arange(rows, dtype=F32), GRID_W)
    col = jnp.tile(jnp.arange(GRID_W, dtype=F32), rows)
    axis_dim = HEAD_DIM // 2
    inv_freq = ROPE_THETA ** (-jnp.arange(0, axis_dim, 2, dtype=F32) / axis_dim)
    ang = jnp.concatenate([row[:, None] * inv_freq, col[:, None] * inv_freq], axis=-1)
    cos = jnp.repeat(jnp.cos(ang), 2, axis=-1)
    sin = jnp.repeat(jnp.sin(ang), 2, axis=-1)
    sign = jnp.tile(jnp.array([-1.0, 1.0], F32), HEAD_DIM // 2)
    return jnp.tile(cos, (1, N_Q_HEADS)), jnp.tile(sin * sign, (1, N_Q_HEADS))


def kernel(x, c, ctx, c_ctx, w_mod, b_mod, ln_mix_g, ln_mix_b, ln_ffn_g, ln_ffn_b, w_mix_in, q_norm_g, k_norm_g, w_pool_grp, pool_scale, w_mix_out, w_conv_in, conv_w, w_conv_out, w_router, w_exp_gate, w_exp_up, w_exp_down):
    b, s, d = x.shape
    lc = ctx.shape[1]
    ts = 512

    cond_rows = jnp.zeros((SUBLANES, d), F32).at[:b].set(c).at[b].set(c_ctx)
    mods = _modulation(cond_rows, w_mod, b_mod).reshape(DEPTH, SUBLANES, 6, d)
    m_lat = lambda layer, k: mods[layer, :b, k][:, None, :]
    m_ctx = lambda layer, k: mods[layer, b:b + 1, k][:, None, :]
    vec = lambda t: t.reshape(1, -1)

    w_in_bf = w_mix_in[0].astype(BF16)
    grp = lax.broadcasted_iota(I32, (ATTN_WIDTH, ATTN_WIDTH), 0) // HEAD_DIM
    bd = jnp.where(grp == grp.T, 1.0 / HEAD_DIM, 0.0).astype(BF16)
    qg = jnp.tile(q_norm_g[0], N_Q_HEADS).reshape(1, -1)
    kg = jnp.tile(k_norm_g[0], N_KV_HEADS).reshape(1, -1)
    cos_t, sin_t = _rope_lane_tables(s)
    q, k, v, p = _l0_in_proj(x, m_lat(0, 0), m_lat(0, 1), w_in_bf, bd, qg, kg, cos_t, sin_t, ts)
    ones = jnp.ones((lc, ATTN_WIDTH), F32)
    _, kc, vc, _ = _l0_in_proj(ctx, m_ctx(0, 0), m_ctx(0, 1), w_in_bf, bd, qg, kg, ones, 0.0 * ones, lc)
    k_all = jnp.concatenate([kc, k], axis=1)
    v_t = jnp.swapaxes(jnp.concatenate([vc, v], axis=1), 1, 2).reshape(b, N_KV_HEADS, HEAD_DIM, lc + s)
    v_t = jnp.concatenate([v_t, jnp.ones((b, N_KV_HEADS, _VT_ROWS - HEAD_DIM, lc + s), BF16)], axis=2)
    score_bound = (1.02 * HEAD_DIM ** 0.5 * LOG2_E) * jnp.max(jnp.abs(q_norm_g[0])) * jnp.max(jnp.abs(k_norm_g[0]))
    a = _attention(q, k_all, v_t, score_bound, 512)
    h, u, aff = _l0_out_proj(a, p, w_pool_grp[0].astype(BF16), vec(pool_scale[0]), w_mix_out[0].astype(BF16),
                             x, m_lat(0, 2), vec(ln_mix_g[0]), vec(ln_mix_b[0]), m_lat(0, 3), m_lat(0, 4),
                             _router_weights(w_router[0]), ts)
    f = _moe(u, aff, 0, w_exp_gate, w_exp_up, w_exp_down)

    h, bg, cx = _l1_in_proj(h, f, m_lat(0, 5), vec(ln_ffn_g[0]), vec(ln_ffn_b[0]), m_lat(1, 0), m_lat(1, 1),
                            w_conv_in[0].astype(BF16), ts)
    h, u, aff = _l1_out_proj(cx, bg, conv_w[0], w_conv_out[0].astype(BF16),
                             h, m_lat(1, 2), vec(ln_mix_g[1]), vec(ln_mix_b[1]), m_lat(1, 3), m_lat(1, 4),
                             _router_weights(w_router[1]), ts)
    f = _moe(u, aff, 1, w_exp_gate, w_exp_up, w_exp_down)
    return _final_norm(h, f, m_lat(1, 5), vec(ln_ffn_g[1]), vec(ln_ffn_b[1]), 2 * ts)
```

```python
import functools

import jax
import jax.numpy as jnp
from jax import lax
from jax.experimental import pallas as pl
from jax.experimental.pallas import tpu as pltpu

F32, BF16, I32 = jnp.float32, jnp.bfloat16, jnp.int32

D_MODEL = 1024
DEPTH = 2
GRID_W = 64
N_Q_HEADS = 8
N_KV_HEADS = 2
HEAD_DIM = 64
ATTN_WIDTH = N_Q_HEADS * HEAD_DIM
KV_WIDTH = N_KV_HEADS * HEAD_DIM
ROPE_THETA = 10000.0
POOL_WINDOWS = (2, 4, 8, 16)
POOL_GROUP_DIM = 128
POOL_WIDTH = len(POOL_WINDOWS) * POOL_GROUP_DIM
MIX_IN_WIDTH = ATTN_WIDTH + 2 * KV_WIDTH + POOL_WIDTH
N_EXPERTS = 16
EXPERT_FF = 1024
CAPACITY_FACTOR = 2
NORM_EPS = 1e-6
DEEPNORM_ALPHA = (2 * DEPTH) ** 0.25
LOG2_E = 1.4426950408889634

LANES = 128
SUBLANES = 8
HALO = SUBLANES
VMEM_LIMIT = 56 * 1024 * 1024

_NT = (((1,), (1,)), ((), ()))


def _cparams(sem):
    return pltpu.CompilerParams(dimension_semantics=sem, vmem_limit_bytes=VMEM_LIMIT)


TOKEN_TILE_ROWS = D_MODEL // LANES


def _store_token_major(ref, rows):
    n = rows.shape[0]
    for j in range(TOKEN_TILE_ROWS):
        ref[pl.ds(j, n, stride=TOKEN_TILE_ROWS), :] = rows[:, j * LANES:(j + 1) * LANES]


def _load_token_major(ref, n):
    return jnp.concatenate(
        [ref[pl.ds(j, n, stride=TOKEN_TILE_ROWS), :] for j in range(TOKEN_TILE_ROWS)], axis=1)


def _layer_norm(z, g, b):
    mu = jnp.mean(z, axis=-1, keepdims=True)
    zc = z - mu
    var = jnp.mean(zc * zc, axis=-1, keepdims=True)
    return zc * lax.rsqrt(var + NORM_EPS) * g + b


def _split_bf16(t):
    hi = t.astype(BF16)
    return hi, (t - hi.astype(F32)).astype(BF16)


def _mod_kernel(c_ref, w_ref, b_ref, o_ref):
    c = c_ref[...]
    c_hi, c_lo = _split_bf16(c * jax.nn.sigmoid(c))
    w_hi, w_lo = _split_bf16(w_ref[...])
    dot = functools.partial(jnp.dot, preferred_element_type=F32)
    o_ref[...] = dot(c_hi, w_hi) + dot(c_lo, w_hi) + dot(c_hi, w_lo) + b_ref[...]


def _modulation(cond_rows, w_mod, b_mod):
    n_out = w_mod.shape[-1]
    tn = D_MODEL
    return pl.pallas_call(
        _mod_kernel,
        grid=(DEPTH, n_out // tn),
        in_specs=[
            pl.BlockSpec((SUBLANES, D_MODEL), lambda l, j: (0, 0)),
            pl.BlockSpec((None, D_MODEL, tn), lambda l, j: (l, 0, j)),
            pl.BlockSpec((None, 1, tn), lambda l, j: (l, 0, j)),
        ],
        out_specs=pl.BlockSpec((None, SUBLANES, tn), lambda l, j: (l, 0, j)),
        out_shape=jax.ShapeDtypeStruct((DEPTH, SUBLANES, n_out), F32),
        compiler_params=_cparams(("arbitrary", "arbitrary")),
        name="modulation",
    )(cond_rows, w_mod, b_mod.reshape(DEPTH, 1, n_out))


def _group_mean_sq(t, bd):
    hi, lo = _split_bf16(t * t)
    return (jnp.dot(hi, bd, preferred_element_type=F32) + jnp.dot(lo, bd, preferred_element_type=F32))


def _rope_chunk(y, c, s_signed):
    n = y.shape[-1]
    lane = lax.broadcasted_iota(I32, y.shape, 1)
    nxt = pltpu.roll(y, n - 1, axis=1)
    prv = pltpu.roll(y, 1, axis=1)
    partner = jnp.where((lane & 1) == 0, nxt, prv)
    return y * c + partner * s_signed


def _l0_in_kernel(x_ref, sh_ref, sc_ref, w_ref, bd_ref, qg_ref, kg_ref, cos_ref, sin_ref,
                  q_ref, k_ref, v_ref, p_ref):
    u = x_ref[...] * (1.0 + sc_ref[...]) + sh_ref[...]
    h = jnp.dot(u.astype(BF16), w_ref[...], preferred_element_type=F32)
    q = h[:, :ATTN_WIDTH]
    k = h[:, ATTN_WIDTH:ATTN_WIDTH + KV_WIDTH]
    bd = bd_ref[...]
    qn = q * lax.rsqrt(_group_mean_sq(q, bd) + NORM_EPS) * qg_ref[...]
    kn = k * lax.rsqrt(_group_mean_sq(k, bd[:KV_WIDTH, :KV_WIDTH]) + NORM_EPS) * kg_ref[...]
    scale = HEAD_DIM ** -0.5 * LOG2_E
    heads_per_kv = N_Q_HEADS // N_KV_HEADS
    lane = lax.broadcasted_iota(I32, (x_ref.shape[0], LANES), 1)
    for c in range(ATTN_WIDTH // LANES):
        sl = slice(c * LANES, (c + 1) * LANES)
        pair = _rope_chunk(qn[:, sl], cos_ref[:, sl], sin_ref[:, sl]) * scale
        swapped = pltpu.roll(pair, HEAD_DIM, axis=1)
        g = (2 * c) // heads_per_kv
        keep = (lane < HEAD_DIM) if g == 0 else (lane >= HEAD_DIM)
        first, second = (pair, swapped) if g == 0 else (swapped, pair)
        q_ref[:, (2 * c) * LANES:(2 * c + 1) * LANES] = jnp.where(keep, first, 0.0).astype(BF16)
        q_ref[:, (2 * c + 1) * LANES:(2 * c + 2) * LANES] = jnp.where(keep, second, 0.0).astype(BF16)
    k_ref[...] = _rope_chunk(kn, cos_ref[:, :KV_WIDTH], sin_ref[:, :KV_WIDTH]).astype(BF16)
    v_ref[...] = h[:, ATTN_WIDTH + KV_WIDTH:ATTN_WIDTH + 2 * KV_WIDTH].astype(BF16)
    p_ref[...] = h[:, ATTN_WIDTH + 2 * KV_WIDTH:]


def _l0_in_proj(x, shift, scale, w_in_bf, bd, qg, kg, cos_t, sin_t, ts):
    b, l, _ = x.shape
    per_batch = shift.shape[0] == b
    mod_map = (lambda bi, si: (bi, 0, 0)) if per_batch else (lambda bi, si: (0, 0, 0))
    full = lambda bi, si: (0, 0)
    return pl.pallas_call(
        _l0_in_kernel,
        grid=(b, l // ts),
        in_specs=[
            pl.BlockSpec((None, ts, D_MODEL), lambda bi, si: (bi, si, 0)),
            pl.BlockSpec((None, 1, D_MODEL), mod_map),
            pl.BlockSpec((None, 1, D_MODEL), mod_map),
            pl.BlockSpec((D_MODEL, MIX_IN_WIDTH), full),
            pl.BlockSpec((ATTN_WIDTH, ATTN_WIDTH), full),
            pl.BlockSpec((1, ATTN_WIDTH), full),
            pl.BlockSpec((1, KV_WIDTH), full),
            pl.BlockSpec((ts, ATTN_WIDTH), lambda bi, si: (si, 0)),
            pl.BlockSpec((ts, ATTN_WIDTH), lambda bi, si: (si, 0)),
        ],
        out_specs=[
            pl.BlockSpec((None, ts, N_Q_HEADS * LANES), lambda bi, si: (bi, si, 0)),
            pl.BlockSpec((None, ts, KV_WIDTH), lambda bi, si: (bi, si, 0)),
            pl.BlockSpec((None, ts, KV_WIDTH), lambda bi, si: (bi, si, 0)),
            pl.BlockSpec((None, ts, POOL_WIDTH), lambda bi, si: (bi, si, 0)),
        ],
        out_shape=[
            jax.ShapeDtypeStruct((b, l, N_Q_HEADS * LANES), BF16),
            jax.ShapeDtypeStruct((b, l, KV_WIDTH), BF16),
            jax.ShapeDtypeStruct((b, l, KV_WIDTH), BF16),
            jax.ShapeDtypeStruct((b, l, POOL_WIDTH), F32),
        ],
        compiler_params=_cparams(("arbitrary", "arbitrary")),
        name="l0_in_proj",
    )(x, shift, scale, w_in_bf, bd, qg, kg, cos_t, sin_t)


_VT_ROWS = HEAD_DIM + 16


_KV_CHUNK = 2176
_MAX_ROWS = 64


def _attn_kernel(q_ref, k_ref, vt_ref, o_ref, ot_ref, *s_refs):
    heads_per_kv = N_Q_HEADS // N_KV_HEADS
    lk, tq = k_ref.shape[0], q_ref.shape[0]
    chunks = [(c0, min(_KV_CHUNK, lk - c0)) for c0 in range(0, lk, _KV_CHUNK)]

    def scores(h, c0, n, m):
        qh = q_ref[:, h * LANES:(h + 1) * LANES]
        s_c = lax.dot_general(k_ref[c0:c0 + n, :], qh, _NT, preferred_element_type=F32)
        s_refs[h % 2][c0:c0 + n, :] = s_c
        mc = jnp.max(s_c.reshape(n // _MAX_ROWS, _MAX_ROWS, tq), axis=0)
        return mc if m is None else jnp.maximum(m, mc)

    def weighted(h, c0, n, m, o_t):
        p_c = jnp.exp2(s_refs[h % 2][c0:c0 + n, :] - m).astype(BF16)
        o_c = jnp.dot(vt_ref[h // heads_per_kv, :, c0:c0 + n], p_c, preferred_element_type=F32)
        return o_c if o_t is None else o_t + o_c

    m_next = None
    for c0, n in chunks:
        m_next = scores(0, c0, n, m_next)
    for h in range(N_Q_HEADS):
        m = jnp.max(m_next, axis=0, keepdims=True)
        m_next = None
        o_t = None
        for c0, n in chunks:
            o_t = weighted(h, c0, n, m, o_t)
            if h + 1 < N_Q_HEADS:
                m_next = scores(h + 1, c0, n, m_next)
        ot_ref[h * HEAD_DIM:(h + 1) * HEAD_DIM, :] = (
            o_t[:HEAD_DIM, :] / o_t[HEAD_DIM:HEAD_DIM + 1, :])
    o_ref[...] = ot_ref[...].T.astype(BF16)


_BOUNDED_KV_CHUNK = 2176


def _attn_bounded_kernel(bound_ref, q_ref, k_ref, vt_ref, o_ref, ot_ref):
    heads_per_kv = N_Q_HEADS // N_KV_HEADS
    lk = k_ref.shape[0]
    shift = bound_ref[0]
    for h in range(N_Q_HEADS):
        g = h // heads_per_kv
        qh = q_ref[:, h * LANES:(h + 1) * LANES]
        o_t = None
        for c0 in range(0, lk, _BOUNDED_KV_CHUNK):
            n = min(_BOUNDED_KV_CHUNK, lk - c0)
            s_c = lax.dot_general(k_ref[c0:c0 + n, :], qh, _NT, preferred_element_type=F32)
            p_c = jnp.exp2(s_c - shift).astype(BF16)
            o_c = jnp.dot(vt_ref[g, :, c0:c0 + n], p_c, preferred_element_type=F32)
            o_t = o_c if o_t is None else o_t + o_c
        ot_ref[h * HEAD_DIM:(h + 1) * HEAD_DIM, :] = (
            o_t[:HEAD_DIM, :] / o_t[HEAD_DIM:HEAD_DIM + 1, :])
    o_ref[...] = ot_ref[...].T.astype(BF16)


_MAX_SAFE_SCORE_BOUND = 40.0


def _attention(q, k_all, vt, score_bound, tq):
    b, s, qw = q.shape
    lk = k_all.shape[1]
    specs = dict(
        grid=(b, s // tq),
        out_specs=pl.BlockSpec((None, tq, ATTN_WIDTH), lambda bi, qi: (bi, qi, 0)),
        out_shape=jax.ShapeDtypeStruct((b, s, ATTN_WIDTH), BF16),
        compiler_params=_cparams(("arbitrary", "arbitrary")),
    )
    in_specs = [
        pl.BlockSpec((None, tq, qw), lambda bi, qi: (bi, qi, 0)),
        pl.BlockSpec((None, lk, KV_WIDTH), lambda bi, qi: (bi, 0, 0)),
        pl.BlockSpec((None, N_KV_HEADS, _VT_ROWS, lk), lambda bi, qi: (bi, 0, 0, 0)),
    ]

    def bounded(_):
        return pl.pallas_call(
            _attn_bounded_kernel,
            in_specs=[pl.BlockSpec(memory_space=pltpu.SMEM)] + in_specs,
            scratch_shapes=[pltpu.VMEM((ATTN_WIDTH, tq), F32)],
            name="attention_bounded", **specs,
        )(score_bound.reshape(1), q, k_all, vt)

    def exact(_):
        return pl.pallas_call(
            _attn_kernel,
            in_specs=in_specs,
            scratch_shapes=[pltpu.VMEM((ATTN_WIDTH, tq), F32), pltpu.VMEM((lk, tq), F32),
                            pltpu.VMEM((lk, tq), F32)],
            name="attention", **specs,
        )(q, k_all, vt)

    return lax.cond(score_bound < _MAX_SAFE_SCORE_BOUND, bounded, exact, None)


def _router_weights(w_router_layer):
    hi, lo = _split_bf16(w_router_layer.T)
    return jnp.concatenate([hi, lo], axis=0)


def _post_tail(resid, y, gate, lng, lnb, shf, scf, wr2):
    h = _layer_norm(DEEPNORM_ALPHA * resid + gate * y, lng, lnb)
    u = h * (1.0 + scf) + shf
    u_hi, u_lo = _split_bf16(u)
    first = lax.dot_general(wr2, u_hi, _NT, preferred_element_type=F32)
    second = lax.dot_general(wr2[:N_EXPERTS], u_lo, _NT, preferred_element_type=F32)
    logits = first[:N_EXPERTS] + first[N_EXPERTS:] + second
    ex = jnp.exp(logits - jnp.max(logits, axis=0, keepdims=True))
    aff = ex / jnp.sum(ex, axis=0, keepdims=True)
    return h, u, aff


def _fill_halo(ext_ref, prev_ref, main_ref, next_ref, ts):
    st = pl.program_id(1)
    last = pl.num_programs(1) - 1
    ext_ref[0:HALO, :] = jnp.where(st > 0, prev_ref[...], 0.0)
    ext_ref[HALO:HALO + ts, :] = main_ref[...]
    ext_ref[HALO + ts:HALO + ts + HALO, :] = jnp.where(st < last, next_ref[...], 0.0)


def _halo_specs(ts, width, seq):
    r = ts // HALO
    nblk = seq // HALO
    return [
        pl.BlockSpec((None, ts, width), lambda bi, si: (bi, si, 0)),
        pl.BlockSpec((None, HALO, width), lambda bi, si: (bi, jnp.maximum(si * r - 1, 0), 0)),
        pl.BlockSpec((None, HALO, width), lambda bi, si: (bi, jnp.minimum((si + 1) * r, nblk - 1), 0)),
    ]


def _tail_specs(ts):
    row = lambda bi, si: (bi, 0, 0)
    full = lambda bi, si: (0, 0)
    in_specs = [
        pl.BlockSpec((None, ts, D_MODEL), lambda bi, si: (bi, si, 0)),
        pl.BlockSpec((None, 1, D_MODEL), row),
        pl.BlockSpec((1, D_MODEL), full),
        pl.BlockSpec((1, D_MODEL), full),
        pl.BlockSpec((None, 1, D_MODEL), row),
        pl.BlockSpec((None, 1, D_MODEL), row),
        pl.BlockSpec((2 * N_EXPERTS, D_MODEL), full),
    ]
    out_specs = [
        pl.BlockSpec((None, ts, D_MODEL), lambda bi, si: (bi, si, 0)),
        pl.BlockSpec((None, ts * TOKEN_TILE_ROWS, LANES), lambda bi, si: (bi, si, 0)),
        pl.BlockSpec((None, N_EXPERTS, ts), lambda bi, si: (bi, 0, si)),
    ]
    return in_specs, out_specs


def _tail_out_shape(b, s):
    return [jax.ShapeDtypeStruct((b, s, D_MODEL), F32),
            jax.ShapeDtypeStruct((b, s * TOKEN_TILE_ROWS, LANES), F32),
            jax.ShapeDtypeStruct((b, N_EXPERTS, s), F32)]


def _l0_out_kernel(seq, a_ref, p_ref, pprev_ref, pnext_ref, wpool_ref, pscale_ref, wout_ref,
                   x_ref, gate_ref, lng_ref, lnb_ref, shf_ref, scf_ref, wr_ref,
                   h_ref, u_ref, aff_ref, pext_ref):
    ts = p_ref.shape[0]
    _fill_halo(pext_ref, pprev_ref, p_ref, pnext_ref, ts)
    t = pl.program_id(1) * ts + lax.broadcasted_iota(I32, (ts, 1), 0)
    outs = []
    for g, w in enumerate(POOL_WINDOWS):
        sl = slice(g * POOL_GROUP_DIM, (g + 1) * POOL_GROUP_DIM)
        fwd = pext_ref[:, sl]
        n_ext = fwd.shape[0]
        span = 1
        while span < w // 2:
            fwd = fwd + pltpu.roll(fwd, n_ext - span, axis=0)
            span *= 2
        acc = (fwd + pltpu.roll(fwd, w // 2, axis=0))[HALO:HALO + ts, :]
        cnt = jnp.minimum(t + w // 2, seq) - jnp.maximum(t - w // 2, 0)
        pooled = acc / cnt.astype(F32) - p_ref[:, sl]
        og = jnp.dot(pooled.astype(BF16), wpool_ref[g], preferred_element_type=F32)
        outs.append((og * pscale_ref[:, sl]).astype(BF16))
    pool = jnp.concatenate(outs, axis=-1)
    y = (jnp.dot(a_ref[...], wout_ref[:ATTN_WIDTH, :], preferred_element_type=F32)
         + jnp.dot(pool, wout_ref[ATTN_WIDTH:, :], preferred_element_type=F32))
    h, u, aff = _post_tail(x_ref[...], y, gate_ref[...], lng_ref[...], lnb_ref[...],
                           shf_ref[...], scf_ref[...], wr_ref[...])
    h_ref[...] = h
    _store_token_major(u_ref, u)
    aff_ref[...] = aff


def _l0_out_proj(a, p, wpool_bf, pscale, wout_bf, x, gate, lng, lnb, shf, scf, wr_t, ts):
    b, s, _ = x.shape
    full2 = lambda bi, si: (0, 0)
    tail_in, tail_out = _tail_specs(ts)
    in_specs = (
        [pl.BlockSpec((None, ts, ATTN_WIDTH), lambda bi, si: (bi, si, 0))]
        + _halo_specs(ts, POOL_WIDTH, s)
        + [pl.BlockSpec((len(POOL_WINDOWS), POOL_GROUP_DIM, POOL_GROUP_DIM), lambda bi, si: (0, 0, 0)),
           pl.BlockSpec((1, POOL_WIDTH), full2),
           pl.BlockSpec((ATTN_WIDTH + POOL_WIDTH, D_MODEL), full2)]
        + tail_in)
    return pl.pallas_call(
        functools.partial(_l0_out_kernel, s),
        grid=(b, s // ts),
        in_specs=in_specs,
        out_specs=tail_out,
        out_shape=_tail_out_shape(b, s),
        scratch_shapes=[pltpu.VMEM((ts + 2 * HALO, POOL_WIDTH), F32)],
        compiler_params=_cparams(("arbitrary", "arbitrary")),
        name="l0_out_proj",
    )(a, p, p, p, wpool_bf, pscale, wout_bf, x, gate, lng, lnb, shf, scf, wr_t)


def _exclusive_cumsum_lanes(mask, tri):
    n = mask.shape[1]
    w = tri.shape[0]
    carry = jnp.zeros((mask.shape[0], 1), F32)
    parts = []
    for c in range(n // w):
        blk = mask[:, c * w:(c + 1) * w]
        inc = jnp.dot(blk.astype(BF16), tri, preferred_element_type=F32)
        parts.append(inc - blk + carry)
        carry = carry + inc[:, w - 1:w]
    return jnp.concatenate(parts, axis=1)


_TOKEN_SPLIT = 64
_SLOT_LO = 32
_PACKED_VALUES = 5


def _select_kernel(cap, aff_ref, out_ref):
    aff = aff_ref[...]
    e = aff.shape[0]
    as_f32 = lambda word: pltpu.bitcast(word, F32)

    def refine(i, thr):
        cand = thr | jnp.left_shift(jnp.int32(1), 30 - i)
        cnt = jnp.sum(jnp.where(aff >= as_f32(cand), 1.0, 0.0), axis=1, keepdims=True)
        return jnp.where(cnt >= cap, cand, thr)

    thr = lax.fori_loop(0, 31, refine, jnp.zeros((e, 1), I32))
    w = 2 * LANES
    r = lax.broadcasted_iota(I32, (w, w), 0)
    c = lax.broadcasted_iota(I32, (w, w), 1)
    tri = jnp.where(r <= c, 1.0, 0.0).astype(BF16)
    gt = jnp.where(aff >= as_f32(thr + 1), 1.0, 0.0)
    eq = jnp.where(aff >= as_f32(thr), 1.0, 0.0) - gt
    need = cap - jnp.sum(gt, axis=1, keepdims=True)
    sel = gt + eq * jnp.where(_exclusive_cumsum_lanes(eq, tri) < need, 1.0, 0.0)
    pos = _exclusive_cumsum_lanes(sel, tri)
    dest = jnp.where(sel > 0.5, pos, float(cap)).astype(I32)

    n = aff.shape[1]
    n_hi = cap // _SLOT_LO
    d_hi = dest // _SLOT_LO
    d_lo = dest % _SLOT_LO
    t = lax.broadcasted_iota(I32, (1, n), 1)
    t_hi = (t // _TOKEN_SPLIT).astype(F32)
    t_lo = (t % _TOKEN_SPLIT).astype(F32)
    a1 = aff.astype(BF16).astype(F32)
    r1 = aff - a1
    a2 = r1.astype(BF16).astype(F32)
    a3 = r1 - a2
    row_hi = lax.broadcasted_iota(I32, (n_hi, n), 0)
    row_lo = lax.broadcasted_iota(I32, (_SLOT_LO, n), 0)
    for ei in range(e):
        rs = slice(ei, ei + 1)
        h = jnp.where(d_hi[rs] == row_hi, 1.0, 0.0)
        l = jnp.where(d_lo[rs] == row_lo, 1.0, 0.0).astype(BF16)
        vals = jnp.concatenate([h * t_hi, h * t_lo, h * a1[rs], h * a2[rs], h * a3[rs]], axis=0)
        out_ref[ei] = lax.dot_general(vals.astype(BF16), l, _NT, preferred_element_type=F32)


def _select(aff, cap):
    b, e, n = aff.shape
    rows = _PACKED_VALUES * (cap // _SLOT_LO)
    return pl.pallas_call(
        functools.partial(_select_kernel, cap),
        grid=(b,),
        in_specs=[pl.BlockSpec((None, e, n), lambda bi: (bi, 0, 0))],
        out_specs=pl.BlockSpec((None, e, rows, _SLOT_LO), lambda bi: (bi, 0, 0, 0)),
        out_shape=jax.ShapeDtypeStruct((b, e, rows, _SLOT_LO), F32),
        compiler_params=_cparams(("arbitrary",)),
        name="expert_select",
    )(aff)


def _token_tile(t):
    return pl.ds(pl.multiple_of(t * TOKEN_TILE_ROWS, TOKEN_TILE_ROWS), TOKEN_TILE_ROWS)


_GATHER_AHEAD = 2
_FF_BLOCK = 512


def _expert_kernel(cap, seq, idx_ref, u_hbm, wg_ref, wu_ref, wd_ref, y_ref,
                   xe_ref, wg_bf, wu_bf, wd_bf, sem):
    b = pl.program_id(1)
    nb = pl.num_programs(1)
    step = pl.program_id(0) * nb + b
    nsteps = pl.num_programs(0) * nb
    nbuf = _GATHER_AHEAD + 1
    slot = step % nbuf

    def start_gather(k, into, rolled):
        ek = k // nb
        bk = k % nb
        base = (bk * N_EXPERTS + ek) * cap
        tok0 = bk * seq

        def start(i, priority):
            tok = tok0 + idx_ref[base + i]
            pltpu.make_async_copy(u_hbm.at[_token_tile(tok), :], xe_ref.at[into, _token_tile(i), :],
                                  sem.at[into]).start(priority=priority)

        if rolled:
            def issue(ci, carry):
                for j in range(SUBLANES):
                    start(ci * SUBLANES + j, j % 2)
                return carry

            lax.fori_loop(0, cap // SUBLANES, issue, 0)
        else:
            for i in range(cap):
                start(i, i % 2)

    def wait_gather(into):
        pltpu.make_async_copy(u_hbm.at[pl.ds(0, cap * TOKEN_TILE_ROWS), :], xe_ref.at[into], sem.at[into]).wait()

    @pl.when(step == 0)
    def _():
        for k in range(_GATHER_AHEAD):
            start_gather(k, k, rolled=True)

    @pl.when(b == 0)
    def _():
        wg_bf[...] = wg_ref[...].astype(BF16)
        wu_bf[...] = wu_ref[...].astype(BF16)
        wd_bf[...] = wd_ref[...].astype(BF16)

    wait_gather(slot)
    x = _load_token_major(xe_ref.at[slot], cap).astype(BF16)
    start_gather(jnp.minimum(step + _GATHER_AHEAD, nsteps - 1), (step + _GATHER_AHEAD) % nbuf, rolled=False)
    blocks = [slice(f0, f0 + _FF_BLOCK) for f0 in range(0, EXPERT_FF, _FF_BLOCK)]
    pre = [(jnp.dot(x, wg_bf[:, fs], preferred_element_type=F32),
            jnp.dot(x, wu_bf[:, fs], preferred_element_type=F32)) for fs in blocks]
    y = None
    for fs, (hg, hu) in zip(blocks, pre):
        act = (hg * jax.nn.sigmoid(hg) * hu).astype(BF16)
        part = jnp.dot(act, wd_bf[fs, :], preferred_element_type=F32)
        y = part if y is None else y + part
    _store_token_major(y_ref, y)

    @pl.when(step == nsteps - 1)
    def _():
        for k in range(1, nbuf):
            wait_gather((step + k) % nbuf)


def _expert_ffn(idx_flat, u_rows, layer, w_g, w_u, w_d, batch, seq, cap):
    wspec = lambda: pl.BlockSpec((None, None, D_MODEL, EXPERT_FF), lambda e, b, idx: (layer, e, 0, 0))
    grid_spec = pltpu.PrefetchScalarGridSpec(
        num_scalar_prefetch=1,
        grid=(N_EXPERTS, batch),
        in_specs=[
            pl.BlockSpec(memory_space=pl.ANY),
            wspec(), wspec(),
            pl.BlockSpec((None, None, EXPERT_FF, D_MODEL), lambda e, b, idx: (layer, e, 0, 0)),
        ],
        out_specs=pl.BlockSpec((None, None, cap * TOKEN_TILE_ROWS, LANES), lambda e, b, idx: (b, e, 0, 0)),
        scratch_shapes=[
            pltpu.VMEM((_GATHER_AHEAD + 1, cap * TOKEN_TILE_ROWS, LANES), F32),
            pltpu.VMEM((D_MODEL, EXPERT_FF), BF16),
            pltpu.VMEM((D_MODEL, EXPERT_FF), BF16),
            pltpu.VMEM((EXPERT_FF, D_MODEL), BF16),
            pltpu.SemaphoreType.DMA((_GATHER_AHEAD + 1,)),
        ],
    )
    return pl.pallas_call(
        functools.partial(_expert_kernel, cap, seq),
        grid_spec=grid_spec,
        out_shape=jax.ShapeDtypeStruct((batch, N_EXPERTS, cap * TOKEN_TILE_ROWS, LANES), F32),
        compiler_params=_cparams(("arbitrary", "arbitrary")),
        name="expert_ffn",
    )(idx_flat, u_rows, w_g, w_u, w_d)


_COMBINE_UNROLL = 16


def _combine_kernel(cap, idx_ref, gate_ref, y_ref, f_ref):
    b = pl.program_id(0)
    e = pl.program_id(1)

    @pl.when(e == 0)
    def _():
        f_ref[...] = jnp.zeros_like(f_ref)

    base = (b * N_EXPERTS + e) * cap

    def chunk(ci, carry):
        i0 = ci * _COMBINE_UNROLL
        tiles = [_token_tile(idx_ref[base + i0 + j]) for j in range(_COMBINE_UNROLL)]
        sums = [f_ref[tiles[j], :] + y_ref[_token_tile(i0 + j), :] * gate_ref[base + i0 + j]
                for j in range(_COMBINE_UNROLL)]
        for j in range(_COMBINE_UNROLL):
            f_ref[tiles[j], :] = sums[j]
        return carry

    lax.fori_loop(0, cap // _COMBINE_UNROLL, chunk, 0)


def _combine(idx_flat, gate_flat, y, seq, cap):
    batch = y.shape[0]
    grid_spec = pltpu.PrefetchScalarGridSpec(
        num_scalar_prefetch=2,
        grid=(batch, N_EXPERTS),
        in_specs=[pl.BlockSpec((None, None, cap * TOKEN_TILE_ROWS, LANES), lambda b, e, idx, gate: (b, e, 0, 0))],
        out_specs=pl.BlockSpec((None, seq * TOKEN_TILE_ROWS, LANES), lambda b, e, idx, gate: (b, 0, 0)),
    )
    return pl.pallas_call(
        functools.partial(_combine_kernel, cap),
        grid_spec=grid_spec,
        out_shape=jax.ShapeDtypeStruct((batch, seq * TOKEN_TILE_ROWS, LANES), F32),
        compiler_params=_cparams(("arbitrary", "arbitrary")),
        name="expert_combine",
    )(idx_flat, gate_flat, y)


def _moe(u, aff, layer, w_g, w_u, w_d):
    b, _, n = aff.shape
    cap = CAPACITY_FACTOR * n // N_EXPERTS
    packed = _select(aff, cap).reshape(b, N_EXPERTS, _PACKED_VALUES, cap)
    idx_flat = (packed[:, :, 0] * _TOKEN_SPLIT + packed[:, :, 1]).astype(I32).reshape(-1)
    gate_flat = (packed[:, :, 2] + packed[:, :, 3] + packed[:, :, 4]).reshape(-1)
    y = _expert_ffn(idx_flat, u.reshape(b * n * TOKEN_TILE_ROWS, LANES), layer, w_g, w_u, w_d, b, n, cap)
    return _combine(idx_flat, gate_flat, y, n, cap)


def _l1_in_kernel(h_ref, f_ref, gf_ref, lng_ref, lnb_ref, sh_ref, sc_ref, w_ref, h2_ref, bg_ref, cx_ref):
    f = _load_token_major(f_ref, h_ref.shape[0])
    h2 = _layer_norm(DEEPNORM_ALPHA * h_ref[...] + gf_ref[...] * f, lng_ref[...], lnb_ref[...])
    h2_ref[...] = h2
    u = h2 * (1.0 + sc_ref[...]) + sh_ref[...]
    z = jnp.dot(u.astype(BF16), w_ref[...], preferred_element_type=F32)
    bg_ref[...] = z[:, :D_MODEL]
    cx_ref[...] = z[:, D_MODEL:2 * D_MODEL] * z[:, 2 * D_MODEL:]


def _l1_in_proj(h, f, gate_f, lng, lnb, shift, scale, w_bf, ts):
    b, s, _ = h.shape
    tok = lambda: pl.BlockSpec((None, ts, D_MODEL), lambda bi, si: (bi, si, 0))
    row = lambda: pl.BlockSpec((None, 1, D_MODEL), lambda bi, si: (bi, 0, 0))
    vec = lambda: pl.BlockSpec((1, D_MODEL), lambda bi, si: (0, 0))
    return pl.pallas_call(
        _l1_in_kernel,
        grid=(b, s // ts),
        in_specs=[tok(), pl.BlockSpec((None, ts * TOKEN_TILE_ROWS, LANES), lambda bi, si: (bi, si, 0)),
                  row(), vec(), vec(), row(), row(),
                  pl.BlockSpec((D_MODEL, 3 * D_MODEL), lambda bi, si: (0, 0))],
        out_specs=[tok(), tok(), tok()],
        out_shape=[jax.ShapeDtypeStruct((b, s, D_MODEL), F32)] * 3,
        compiler_params=_cparams(("arbitrary", "arbitrary")),
        name="l1_in_proj",
    )(h, f, gate_f, lng, lnb, shift, scale, w_bf)


def _l1_out_kernel(cx_ref, cprev_ref, cnext_ref, bg_ref, cw_ref, wout_ref,
                   x_ref, gate_ref, lng_ref, lnb_ref, shf_ref, scf_ref, wr_ref,
                   h_ref, u_ref, aff_ref, cext_ref):
    ts = cx_ref.shape[0]
    _fill_halo(cext_ref, cprev_ref, cx_ref, cnext_ref, ts)
    z = (cext_ref[HALO - 1:HALO - 1 + ts, :] * cw_ref[0:1, :]
         + cx_ref[...] * cw_ref[1:2, :]
         + cext_ref[HALO + 1:HALO + 1 + ts, :] * cw_ref[2:3, :])
    y = jnp.dot((bg_ref[...] * z).astype(BF16), wout_ref[...], preferred_element_type=F32)
    h, u, aff = _post_tail(x_ref[...], y, gate_ref[...], lng_ref[...], lnb_ref[...],
                           shf_ref[...], scf_ref[...], wr_ref[...])
    h_ref[...] = h
    _store_token_major(u_ref, u)
    aff_ref[...] = aff


def _l1_out_proj(cx, bg, conv_w, wout_bf, x, gate, lng, lnb, shf, scf, wr_t, ts):
    b, s, _ = x.shape
    tail_in, tail_out = _tail_specs(ts)
    in_specs = (
        _halo_specs(ts, D_MODEL, s)
        + [pl.BlockSpec((None, ts, D_MODEL), lambda bi, si: (bi, si, 0)),
           pl.BlockSpec(conv_w.shape, lambda bi, si: (0, 0)),
           pl.BlockSpec((D_MODEL, D_MODEL), lambda bi, si: (0, 0))]
        + tail_in)
    return pl.pallas_call(
        _l1_out_kernel,
        grid=(b, s // ts),
        in_specs=in_specs,
        out_specs=tail_out,
        out_shape=_tail_out_shape(b, s),
        scratch_shapes=[pltpu.VMEM((ts + 2 * HALO, D_MODEL), F32)],
        compiler_params=_cparams(("arbitrary", "arbitrary")),
        name="l1_out_proj",
    )(cx, cx, cx, bg, conv_w, wout_bf, x, gate, lng, lnb, shf, scf, wr_t)


def _final_kernel(h_ref, f_ref, gf_ref, lng_ref, lnb_ref, o_ref):
    f = _load_token_major(f_ref, h_ref.shape[0])
    o_ref[...] = _layer_norm(DEEPNORM_ALPHA * h_ref[...] + gf_ref[...] * f, lng_ref[...], lnb_ref[...])


def _final_norm(h, f, gate_f, lng, lnb, ts):
    b, s, _ = h.shape
    tok = lambda: pl.BlockSpec((None, ts, D_MODEL), lambda bi, si: (bi, si, 0))
    return pl.pallas_call(
        _final_kernel,
        grid=(b, s // ts),
        in_specs=[tok(), pl.BlockSpec((None, ts * TOKEN_TILE_ROWS, LANES), lambda bi, si: (bi, si, 0)),
                  pl.BlockSpec((None, 1, D_MODEL), lambda bi, si: (bi, 0, 0)),
                  pl.BlockSpec((1, D_MODEL), lambda bi, si: (0, 0)),
                  pl.BlockSpec((1, D_MODEL), lambda bi, si: (0, 0))],
        out_specs=tok(),
        out_shape=jax.ShapeDtypeStruct((b, s, D_MODEL), F32),
        compiler_params=_cparams(("arbitrary", "arbitrary")),
        name="final_norm",
    )(h, f, gate_f, lng, lnb)


def _rope_lane_tables(seq):
    rows = seq // GRID_W
    row = jnp.repeat(jnp.arange(rows, dtype=F32), GRID_W)
    col = jnp.tile(jnp.arange(GRID_W, dtype=F32), rows)
    axis_dim = HEAD_DIM // 2
    inv_freq = ROPE_THETA ** (-jnp.arange(0, axis_dim, 2, dtype=F32) / axis_dim)
    ang = jnp.concatenate([row[:, None] * inv_freq, col[:, None] * inv_freq], axis=-1)
    cos = jnp.repeat(jnp.cos(ang), 2, axis=-1)
    sin = jnp.repeat(jnp.sin(ang), 2, axis=-1)
    sign = jnp.tile(jnp.array([-1.0, 1.0], F32), HEAD_DIM // 2)
    return jnp.tile(cos, (1, N_Q_HEADS)), jnp.tile(sin * sign, (1, N_Q_HEADS))


def kernel(x, c, ctx, c_ctx, w_mod, b_mod, ln_mix_g, ln_mix_b, ln_ffn_g, ln_ffn_b, w_mix_in, q_norm_g, k_norm_g, w_pool_grp, pool_scale, w_mix_out, w_conv_in, conv_w, w_conv_out, w_router, w_exp_gate, w_exp_up, w_exp_down):
    b, s, d = x.shape
    lc = ctx.shape[1]
    ts = 512

    cond_rows = jnp.zeros((SUBLANES, d), F32).at[:b].set(c).at[b].set(c_ctx)
    mods = _modulation(cond_rows, w_mod, b_mod).reshape(DEPTH, SUBLANES, 6, d)
    m_lat = lambda layer, k: mods[layer, :b, k][:, None, :]
    m_ctx = lambda layer, k: mods[layer, b:b + 1, k][:, None, :]
    vec = lambda t: t.reshape(1, -1)

    w_in_bf = w_mix_in[0].astype(BF16)
    grp = lax.broadcasted_iota(I32, (ATTN_WIDTH, ATTN_WIDTH), 0) // HEAD_DIM
    bd = jnp.where(grp == grp.T, 1.0 / HEAD_DIM, 0.0).astype(BF16)
    qg = jnp.tile(q_norm_g[0], N_Q_HEADS).reshape(1, -1)
    kg = jnp.tile(k_norm_g[0], N_KV_HEADS).reshape(1, -1)
    cos_t, sin_t = _rope_lane_tables(s)
    q, k, v, p = _l0_in_proj(x, m_lat(0, 0), m_lat(0, 1), w_in_bf, bd, qg, kg, cos_t, sin_t, ts)
    ones = jnp.ones((lc, ATTN_WIDTH), F32)
    _, kc, vc, _ = _l0_in_proj(ctx, m_ctx(0, 0), m_ctx(0, 1), w_in_bf, bd, qg, kg, ones, 0.0 * ones, lc)
    k_all = jnp.concatenate([kc, k], axis=1)
    v_t = jnp.swapaxes(jnp.concatenate([vc, v], axis=1), 1, 2).reshape(b, N_KV_HEADS, HEAD_DIM, lc + s)
    v_t = jnp.concatenate([v_t, jnp.ones((b, N_KV_HEADS, _VT_ROWS - HEAD_DIM, lc + s), BF16)], axis=2)
    score_bound = (1.02 * HEAD_DIM ** 0.5 * LOG2_E) * jnp.max(jnp.abs(q_norm_g[0])) * jnp.max(jnp.abs(k_norm_g[0]))
    a = _attention(q, k_all, v_t, score_bound, 1024)
    h, u, aff = _l0_out_proj(a, p, w_pool_grp[0].astype(BF16), vec(pool_scale[0]), w_mix_out[0].astype(BF16),
                             x, m_lat(0, 2), vec(ln_mix_g[0]), vec(ln_mix_b[0]), m_lat(0, 3), m_lat(0, 4),
                             _router_weights(w_router[0]), ts)
    f = _moe(u, aff, 0, w_exp_gate, w_exp_up, w_exp_down)

    h, bg, cx = _l1_in_proj(h, f, m_lat(0, 5), vec(ln_ffn_g[0]), vec(ln_ffn_b[0]), m_lat(1, 0), m_lat(1, 1),
                            w_conv_in[0].astype(BF16), ts)
    h, u, aff = _l1_out_proj(cx, bg, conv_w[0], w_conv_out[0].astype(BF16),
                             h, m_lat(1, 2), vec(ln_mix_g[1]), vec(ln_mix_b[1]), m_lat(1, 3), m_lat(1, 4),
                             _router_weights(w_router[1]), ts)
    f = _moe(u, aff, 1, w_exp_gate, w_exp_up, w_exp_down)
    return _final_norm(h, f, m_lat(1, 5), vec(ln_ffn_g[1]), vec(ln_ffn_b[1]), 2 * ts)
```
